```python
import math
import jax
import jax.numpy as jnp
from jax import lax
import numpy as np

D_MODEL = 1024
BATCH = 8
SEQ = 2048
DEPTH = 2
DEC_BATCH = 128
DEC_SEQ = 8
PAST_LEN = 16384
PAGE_SIZE = 128

N_META = 16
HG_EXPAND = 128
HG_HEADS = D_MODEL // HG_EXPAND
HG_DK = HG_EXPAND
HG_DV = D_MODEL // HG_HEADS
HG_CHUNK = 16
GDN_HEADS = 8
GDN_DK = 128
GDN_DV = 128
GDN_CONV = 4
GDN_CHUNK = 64
GDN_QKV = GDN_HEADS * (2 * GDN_DK + GDN_DV)
D_FF = 3584
N_EXPERTS = 8
TOP_K = 2
D_FF_EXPERT = 3584
EPS = 1e-6

kernel_name = 'hgrn2_gdn_meta_moe_decode_step'


def rmsnorm(x, g):
    xf = x.astype(jnp.float32)
    y = xf * lax.rsqrt(jnp.mean(xf * xf, axis=-1, keepdims=True) + EPS)
    return (y * g.astype(jnp.float32)).astype(x.dtype)


def l2norm(x):
    return x * lax.rsqrt(jnp.sum(x * x, axis=-1, keepdims=True) + EPS)


def split_heads(x, h, d):
    return x.reshape(x.shape[:-1] + (h, d))


def run_segments(seg_fn, arrays, s0, segments):
    outs = []
    s = s0
    start = 0
    for seg_len in segments:
        o, s = seg_fn(*[a[:, start:start + seg_len] for a in arrays], s)
        outs.append(o)
        start += seg_len
    return jnp.concatenate(outs, axis=1), s


def hgrn2_segment(q, k, v, logf, s0):
    bn, t, h, kd = q.shape
    vd = v.shape[-1]
    c = math.gcd(t, HG_CHUNK)
    n = t // c
    q, k, logf = [a.reshape(bn, n, c, h, kd) for a in (q, k, logf)]
    v = v.reshape(bn, n, c, h, vd)
    b = jnp.cumsum(logf, axis=2)
    q_dec = q * jnp.exp(b)
    k_inv = k * jnp.exp(-b)
    k_tail = k * jnp.exp(b[:, :, -1:] - b)
    dec_total = jnp.exp(b[:, :, -1])
    incl = jnp.tril(jnp.ones((c, c), dtype=bool))
    attn = jnp.where(incl, jnp.einsum('bnchk,bnjhk->bnhcj', q_dec, k_inv), 0.0)
    o_intra = jnp.einsum('bnhcj,bnjhv->bnchv', attn, v)

    def step(s, xs):
        qd, kt, vv, dt = xs
        o = jnp.einsum('bchk,bhkv->bchv', qd, s)
        s = s * dt[..., None] + jnp.einsum('bchk,bchv->bhkv', kt, vv)
        return s, o

    xs = tuple(jnp.moveaxis(a, 1, 0) for a in (q_dec, k_tail, v, dec_total))
    s, o_inter = lax.scan(step, s0, xs)
    o = o_intra + jnp.moveaxis(o_inter, 0, 1)
    return o.reshape(bn, t, h, vd), s


def gdn_segment(q, k, v, beta, loga, s0):
    bn, t, h, kd = q.shape
    vd = v.shape[-1]
    c = math.gcd(t, GDN_CHUNK)
    n = t // c

    def blocks(a):
        return jnp.moveaxis(a.reshape((bn, n, c) + a.shape[2:]), 3, 2)

    q, k, v, beta, loga = [blocks(a) for a in (q, k, v, beta, loga)]
    g = jnp.cumsum(loga, axis=-1)
    diff = g[..., :, None] - g[..., None, :]
    idx = jnp.arange(c)
    strict = idx[:, None] > idx[None, :]
    incl = idx[:, None] >= idx[None, :]
    dec_strict = jnp.exp(jnp.where(strict, diff, -jnp.inf))
    dec_incl = jnp.exp(jnp.where(incl, diff, -jnp.inf))
    lmat = beta[..., :, None] * jnp.einsum('bnhck,bnhjk->bnhcj', k, k) * dec_strict
    mmat = lmat + jnp.eye(c, dtype=lmat.dtype)
    rhs = jnp.concatenate([(beta * jnp.exp(g))[..., None] * k, beta[..., None] * v], axis=-1)
    sol = lax.linalg.triangular_solve(mmat, rhs, left_side=True, lower=True, unit_diagonal=True)
    w_mat, u_v = sol[..., :kd], sol[..., kd:]
    a_mat = jnp.einsum('bnhck,bnhjk->bnhcj', q, k) * dec_incl
    q_dec = q * jnp.exp(g)[..., None]
    k_tail = k * jnp.exp(g[..., -1:] - g)[..., None]
    dec_total = jnp.exp(g[..., -1])

    def step(s, xs):
        wn, uvn, an, qd, kt, dt = xs
        u = uvn - jnp.einsum('bhck,bhkv->bhcv', wn, s)
        o = jnp.einsum('bhck,bhkv->bhcv', qd, s) + jnp.einsum('bhcj,bhjv->bhcv', an, u)
        s = s * dt[..., None, None] + jnp.einsum('bhck,bhcv->bhkv', kt, u)
        return s, o

    xs = tuple(jnp.moveaxis(a, 1, 0) for a in (w_mat, u_v, a_mat, q_dec, k_tail, dec_total))
    s, o = lax.scan(step, s0, xs)
    o = jnp.transpose(o, (1, 0, 3, 2, 4)).reshape(bn, t, h, vd)
    return o, s


def hgrn2_mixer(xn, s0, segments, layer, w_in, lower_bounds, g_norm, w_o):
    bn, t, _ = xn.shape
    wk = HG_HEADS * HG_DK
    wv = HG_HEADS * HG_DV
    proj = xn @ w_in
    q = jax.nn.silu(proj[..., :wk].astype(jnp.float32)) * HG_DK ** -0.5
    lb = jnp.cumsum(jax.nn.softmax(lower_bounds.astype(jnp.float32), axis=0), axis=0)[layer]
    f = lb + (1.0 - lb) * jax.nn.sigmoid(proj[..., wk:2 * wk].astype(jnp.float32))
    k = 1.0 - f
    logf = jnp.log(f)
    i_in = proj[..., 2 * wk:2 * wk + wv].astype(jnp.float32)
    gate = jax.nn.silu(proj[..., 2 * wk + wv:].astype(jnp.float32))
    arrays = (split_heads(q, HG_HEADS, HG_DK), split_heads(k, HG_HEADS, HG_DK),
              split_heads(i_in, HG_HEADS, HG_DV), split_heads(logf, HG_HEADS, HG_DK))
    o, s = run_segments(hgrn2_segment, arrays, s0, segments)
    o = rmsnorm(o, g_norm).reshape(bn, t, wv) * gate
    return o.astype(xn.dtype) @ w_o, s


def gdn_mixer(xn, s0, conv_buf, segments, w_in, conv_w, a_log, dt_bias, g_norm, w_o):
    bn, t, _ = xn.shape
    h, kd, vd = GDN_HEADS, GDN_DK, GDN_DV
    proj = xn @ w_in
    pre = proj[..., :GDN_QKV]
    z = proj[..., GDN_QKV:GDN_QKV + h * vd]
    b_raw = proj[..., GDN_QKV + h * vd:GDN_QKV + h * vd + h]
    a_raw = proj[..., GDN_QKV + h * vd + h:]
    xp = jnp.concatenate([conv_buf.astype(pre.dtype), pre], axis=1)
    new_buf = xp[:, xp.shape[1] - (GDN_CONV - 1):]
    conv = xp[:, 0:t] * conv_w[0]
    for w in range(1, GDN_CONV):
        conv = conv + xp[:, w:w + t] * conv_w[w]
    act = jax.nn.silu(conv.astype(jnp.float32))
    q = l2norm(split_heads(act[..., :h * kd], h, kd)) * kd ** -0.5
    k = l2norm(split_heads(act[..., h * kd:2 * h * kd], h, kd))
    v = split_heads(act[..., 2 * h * kd:], h, vd)
    beta = jax.nn.sigmoid(b_raw.astype(jnp.float32))
    loga = -jnp.exp(a_log.astype(jnp.float32)) * jax.nn.softplus(a_raw.astype(jnp.float32) + dt_bias.astype(jnp.float32))
    o, s = run_segments(gdn_segment, (q, k, v, beta, loga), s0, segments)
    o = rmsnorm(o, g_norm) * split_heads(jax.nn.silu(z.astype(jnp.float32)), h, vd)
    return o.reshape(bn, t, h * vd).astype(xn.dtype) @ w_o, s, new_buf


def swiglu(x, w1, w3, w2):
    return (jax.nn.silu(x @ w1) * (x @ w3)) @ w2


def moe(x, w_router, w1, w3, w2):
    logits = (x @ w_router).astype(jnp.float32)
    probs = jax.nn.softmax(logits, axis=-1)
    top_p, top_i = lax.top_k(probs, TOP_K)
    top_p = top_p / jnp.sum(top_p, axis=-1, keepdims=True)
    gates = jnp.sum(jax.nn.one_hot(top_i, N_EXPERTS, dtype=jnp.float32) * top_p[..., None], axis=-2)
    out = jnp.zeros(x.shape, jnp.float32)
    for e in range(N_EXPERTS):
        out = out + gates[..., e:e + 1] * swiglu(x, w1[e], w3[e], w2[e]).astype(jnp.float32)
    return out.astype(x.dtype)


def trunk(h, s_hg, s_gdn, s_conv, segments, norm_mix, norm_ffn, norm_final,
          hg_w_in, hg_lower_bounds, hg_g_norm, hg_w_o,
          gdn_w_in, gdn_conv_w, gdn_a_log, gdn_dt_bias, gdn_g_norm, gdn_w_o,
          ffn_w1, ffn_w3, ffn_w2, moe_w_router, moe_w1, moe_w3, moe_w2):
    for layer in range(DEPTH):
        xn = rmsnorm(h, norm_mix[layer])
        if layer % 2 == 0:
            out, s_hg = hgrn2_mixer(xn, s_hg, segments, layer, hg_w_in, hg_lower_bounds, hg_g_norm, hg_w_o)
        else:
            out, s_gdn, s_conv = gdn_mixer(xn, s_gdn, s_conv, segments, gdn_w_in, gdn_conv_w,
                                           gdn_a_log, gdn_dt_bias, gdn_g_norm, gdn_w_o)
        h = h + out
        xn = rmsnorm(h, norm_ffn[layer])
        if layer % 2 == 0:
            h = h + swiglu(xn, ffn_w1, ffn_w3, ffn_w2)
        else:
            h = h + moe(xn, moe_w_router, moe_w1, moe_w3, moe_w2)
    return rmsnorm(h, norm_final), s_hg, s_gdn, s_conv


def setup_inputs(seed: int = 0) -> dict:
    key = jax.random.key(seed)
    ks = jax.random.split(key, 32)
    d = D_MODEL
    hg_in_cols = 3 * HG_HEADS * HG_DK + HG_HEADS * HG_DV
    gdn_in_cols = GDN_QKV + GDN_HEADS * GDN_DV + 2 * GDN_HEADS

    def nrm(k, shape, scale):
        return jax.random.normal(k, shape, jnp.float32) * scale

    dt = jnp.exp(jax.random.uniform(ks[16], (GDN_HEADS,), jnp.float32, math.log(1e-3), math.log(1e-1)))
    return {
        'x_prompt': nrm(ks[0], (BATCH, SEQ, d), 1.0),
        'x_sample': nrm(ks[1], (DEC_BATCH, DEC_SEQ, d), 1.0),
        'state_hgrn': nrm(ks[2], (DEC_BATCH, HG_HEADS, HG_DK, HG_DV), 0.5),
        'state_gdn': nrm(ks[3], (DEC_BATCH, GDN_HEADS, GDN_DK, GDN_DV), GDN_DK ** -0.5),
        'state_gdn_conv': nrm(ks[4], (DEC_BATCH, GDN_CONV - 1, GDN_QKV), 1.0),
        'meta_tokens': nrm(ks[5], (N_META, d), 1.0),
        'norm_mix': 1.0 + nrm(ks[6], (DEPTH, d), 0.02),
        'norm_ffn': 1.0 + nrm(ks[7], (DEPTH, d), 0.02),
        'norm_final': 1.0 + nrm(ks[8], (d,), 0.02),
        'hg_w_in': nrm(ks[9], (d, hg_in_cols), d ** -0.5),
        'hg_lower_bounds': nrm(ks[10], (DEPTH, HG_HEADS * HG_DK), 0.1),
        'hg_g_norm': 1.0 + nrm(ks[11], (HG_DV,), 0.02),
        'hg_w_o': nrm(ks[12], (HG_HEADS * HG_DV, d), (HG_HEADS * HG_DV) ** -0.5),
        'gdn_w_in': nrm(ks[13], (d, gdn_in_cols), d ** -0.5),
        'gdn_conv_w': nrm(ks[14], (GDN_CONV, GDN_QKV), GDN_CONV ** -0.5),
        'gdn_a_log': jnp.log(jax.random.uniform(ks[15], (GDN_HEADS,), jnp.float32, 1.0, 16.0)),
        'gdn_dt_bias': dt + jnp.log(-jnp.expm1(-dt)),
        'gdn_g_norm': 1.0 + nrm(ks[17], (GDN_DV,), 0.02),
        'gdn_w_o': nrm(ks[18], (GDN_HEADS * GDN_DV, d), (GDN_HEADS * GDN_DV) ** -0.5),
        'ffn_w1': nrm(ks[19], (d, D_FF), d ** -0.5),
        'ffn_w3': nrm(ks[20], (d, D_FF), d ** -0.5),
        'ffn_w2': nrm(ks[21], (D_FF, d), D_FF ** -0.5),
        'moe_w_router': nrm(ks[22], (d, N_EXPERTS), d ** -0.5),
        'moe_w1': nrm(ks[23], (N_EXPERTS, d, D_FF_EXPERT), d ** -0.5),
        'moe_w3': nrm(ks[24], (N_EXPERTS, d, D_FF_EXPERT), d ** -0.5),
        'moe_w2': nrm(ks[25], (N_EXPERTS, D_FF_EXPERT, d), D_FF_EXPERT ** -0.5),
    }


def reference(x_prompt, x_sample, state_hgrn, state_gdn, state_gdn_conv, meta_tokens,
              norm_mix, norm_ffn, norm_final,
              hg_w_in, hg_lower_bounds, hg_g_norm, hg_w_o,
              gdn_w_in, gdn_conv_w, gdn_a_log, gdn_dt_bias, gdn_g_norm, gdn_w_o,
              ffn_w1, ffn_w3, ffn_w2, moe_w_router, moe_w1, moe_w3, moe_w2):
    weights = (norm_mix, norm_ffn, norm_final, hg_w_in, hg_lower_bounds, hg_g_norm, hg_w_o,
               gdn_w_in, gdn_conv_w, gdn_a_log, gdn_dt_bias, gdn_g_norm, gdn_w_o,
               ffn_w1, ffn_w3, ffn_w2, moe_w_router, moe_w1, moe_w3, moe_w2)

    bp, tp, d = x_prompt.shape
    meta = jnp.broadcast_to(meta_tokens.astype(x_prompt.dtype)[None], (bp, N_META, d))
    h_p = jnp.concatenate([meta, x_prompt], axis=1)
    hg0 = jnp.zeros((bp, HG_HEADS, HG_DK, HG_DV), jnp.float32)
    gdn0 = jnp.zeros((bp, GDN_HEADS, GDN_DK, GDN_DV), jnp.float32)
    conv0 = jnp.zeros((bp, GDN_CONV - 1, GDN_QKV), x_prompt.dtype)
    yp, hg_p, gdn_p, conv_p = trunk(h_p, hg0, gdn0, conv0, (N_META, tp), *weights)
    y_prompt = yp[:, N_META:]

    ys, hg_s, gdn_s, conv_s = trunk(x_sample, state_hgrn.astype(jnp.float32), state_gdn.astype(jnp.float32),
                                    state_gdn_conv, (x_sample.shape[1],), *weights)

    return (y_prompt, ys,
            hg_p.astype(state_hgrn.dtype), hg_s.astype(state_hgrn.dtype),
            gdn_p.astype(state_gdn.dtype), gdn_s.astype(state_gdn.dtype),
            conv_p.astype(state_gdn_conv.dtype), conv_s.astype(state_gdn_conv.dtype))
```

```python
import functools

import jax
import jax.numpy as jnp
from jax import lax
from jax.experimental import pallas as pl
from jax.experimental.pallas import tpu as pltpu

D = 1024
H = 8
HD = 128
QKV = 3 * H * HD
CONV = 4
DFF = 3584
NE = 8
N_META = 16
EPS = 1e-6
LANES = 128
FB = 512
TM = 256
V7X_VMEM_LIMIT = 56 * 1024 * 1024

F32 = jnp.float32
BF16 = jnp.bfloat16
HI = lax.Precision.HIGHEST


def _dot(a, b, precision=None):
    return jnp.dot(a, b, preferred_element_type=F32, precision=precision)


def _dot_nt(a, b, precision=None):
    return lax.dot_general(a, b, (((1,), (1,)), ((), ())), preferred_element_type=F32, precision=precision)


def _dot_tn(a, b, precision=None):
    return lax.dot_general(a, b, (((0,), (0,)), ((), ())), preferred_element_type=F32, precision=precision)


def _rms(x, g):
    return x * lax.rsqrt(jnp.mean(x * x, axis=-1, keepdims=True) + EPS) * g


def _silu(x):
    return x * jax.nn.sigmoid(x)


def _iota2(shape, dim):
    return lax.broadcasted_iota(jnp.int32, shape, dim)


def _head_rms_gate(o, g, gate):
    parts = []
    for h in range(H):
        oh = o[:, h * HD:(h + 1) * HD]
        parts.append(_rms(oh, g))
    return jnp.concatenate(parts, axis=1) * gate


def _const_spec(shape):
    nd = len(shape)
    return pl.BlockSpec(shape, lambda *_: (0,) * nd, pipeline_mode=pl.Buffered(1))


def _hgrn_kernel(x_ref, s0_ref, nw_ref, win_ref, lb_ref, gn_ref, wo_ref,
                 o_ref, sout_ref,
                 st_scr, q_scr, k_scr, lf_scr, v_scr, g_scr, *, bb, tt, c):
    t = pl.program_id(1)
    rows = bb * tt
    ncs = tt // c

    @pl.when(t == 0)
    def _():
        for b in range(bb):
            for h in range(H):
                st_scr[b, h] = s0_ref[b, h].T

    x = x_ref[...]
    xn = _rms(x, nw_ref[...]).astype(BF16)
    lbn = lb_ref[...]
    e = jnp.exp(lbn - jnp.max(lbn, axis=0, keepdims=True))
    lb = e[0:1] / jnp.sum(e, axis=0, keepdims=True)
    q_scr[...] = _silu(_dot(xn, win_ref[:, 0:D])) * HD ** -0.5
    f = lb + (1.0 - lb) * jax.nn.sigmoid(_dot(xn, win_ref[:, D:2 * D]))
    k_scr[...] = 1.0 - f
    lf_scr[...] = jnp.log(f)
    v_scr[...] = _dot(xn, win_ref[:, 2 * D:3 * D])
    g_scr[...] = _silu(_dot(xn, win_ref[:, 3 * D:4 * D]))

    tri = _iota2((c, c), 0) >= _iota2((c, c), 1)
    tri_f = tri.astype(F32)

    def chunk(idx, carry):
        r0 = pl.multiple_of(idx * c, c)
        sq = idx // ncs
        bc = _dot(tri_f, lf_scr[pl.ds(r0, c), :], HI)
        bl = bc[c - 1:c, :]
        kk = k_scr[pl.ds(r0, c), :]
        qd = q_scr[pl.ds(r0, c), :] * jnp.exp(bc)
        kinv = kk * jnp.exp(-bc)
        ktail = kk * jnp.exp(bl - bc)
        dec = jnp.exp(bl)
        vv = v_scr[pl.ds(r0, c), :]
        for h in range(H):
            sl = slice(h * HD, (h + 1) * HD)
            sc = jnp.where(tri, _dot_nt(qd[:, sl], kinv[:, sl]), 0.0)
            st = st_scr[sq, h]
            o = _dot(sc, vv[:, sl]) + _dot_nt(qd[:, sl], st)
            st_scr[sq, h] = st * dec[:, sl] + _dot_tn(vv[:, sl], ktail[:, sl])
            q_scr[pl.ds(r0, c), sl] = o
        return carry

    lax.fori_loop(0, rows // c, chunk, 0)

    on = _head_rms_gate(q_scr[...], gn_ref[...], g_scr[...])
    o_ref[...] = x + _dot(on.astype(BF16), wo_ref[...])

    @pl.when(t == pl.num_programs(1) - 1)
    def _():
        for b in range(bb):
            for h in range(H):
                sout_ref[b, h] = st_scr[b, h].T


def _hgrn_call(x2d, s0, nw, win, lbnd, gn, wo, *, nseq, seqlen, bb, tt, row0=0, bcast_state=False):
    assert bb == 1 or tt == seqlen
    assert nseq % bb == 0 and seqlen % tt == 0 and row0 % (bb * tt) == 0
    c = 16 if tt % 16 == 0 else 8
    rows = bb * tt
    nt = seqlen // tt
    blk0 = row0 // rows
    s_idx = (lambda b, t: (0, 0, 0, 0)) if bcast_state else (lambda b, t: (b, 0, 0, 0))
    slab = pltpu.VMEM((rows, D), F32)
    return pl.pallas_call(
        functools.partial(_hgrn_kernel, bb=bb, tt=tt, c=c),
        grid=(nseq // bb, nt),
        in_specs=[
            pl.BlockSpec((rows, D), lambda b, t: (blk0 + b * nt + t, 0)),
            pl.BlockSpec((bb, H, HD, HD), s_idx),
            _const_spec((1, D)),
            _const_spec((D, 4 * D)),
            _const_spec(lbnd.shape),
            _const_spec((1, HD)),
            _const_spec((D, D)),
        ],
        out_specs=[
            pl.BlockSpec((rows, D), lambda b, t: (b * nt + t, 0)),
            pl.BlockSpec((bb, H, HD, HD), lambda b, t: (b, 0, 0, 0)),
        ],
        out_shape=[
            jax.ShapeDtypeStruct((nseq * seqlen, D), F32),
            jax.ShapeDtypeStruct((nseq, H, HD, HD), F32),
        ],
        scratch_shapes=[pltpu.VMEM((bb, H, HD, HD), F32), slab, slab, slab, slab, slab],
        compiler_params=pltpu.CompilerParams(
            dimension_semantics=("arbitrary", "arbitrary"), vmem_limit_bytes=V7X_VMEM_LIMIT),
        name=f"hgrn_mixer_{nseq}x{seqlen}",
    )(x2d, s0, nw, win, lbnd, gn, wo)


def _neumann_inv(l, eye, levels):
    x = eye - l
    p = l
    for _ in range(levels - 1):
        p = _dot(p, p)
        x = x + _dot(x, p)
    return x


def _unit_lower_inv(l, c):
    row = _iota2((c, c), 0)
    col = _iota2((c, c), 1)
    eye = (row == col).astype(F32)
    if c <= 16:
        return _neumann_inv(l, eye, c.bit_length() - 1)
    sub = 16
    assert c == 4 * sub
    diag = (row // sub) == (col // sub)
    ld = jnp.where(diag, l, 0.0)
    dinv = _neumann_inv(ld, eye, 4)
    m = _dot(dinv, l - ld)
    y = eye - m
    y = y + _dot(y, _dot(m, m))
    return _dot(y, dinv)


def _gdn_kernel(x_ref, s0_ref, c0_ref, nw_ref, win_ref, cw_ref, par_ref, eb_ref, ea_ref, gn_ref, wo_ref,
                o_ref, sout_ref, cout_ref,
                s_scr, xp_scr, act_scr, z_scr, ba_scr, bx_scr, lx_scr, o_scr, *, bb, tt, c):
    t = pl.program_id(1)
    rows = bb * tt
    ncs = tt // c

    @pl.when(t == 0)
    def _():
        s_scr[...] = s0_ref[...]
        for b in range(bb):
            xp_scr[b, 8 - (CONV - 1):8, :] = c0_ref[b]

    x = x_ref[...]
    xn = _rms(x, nw_ref[...]).astype(BF16)
    pre = _dot(xn, win_ref[:, 0:QKV])
    for b in range(bb):
        xp_scr[b, 8:8 + tt, :] = pre[b * tt:(b + 1) * tt]
    z_scr[...] = _silu(_dot(xn, win_ref[:, QKV:QKV + D]))
    ba = _dot(xn, win_ref[:, QKV + D:QKV + D + LANES])
    par = par_ref[...]
    sp = ba + par[0:1]
    softplus = jnp.maximum(sp, 0.0) + jnp.log(1.0 + jnp.exp(-jnp.abs(sp)))
    loga = -jnp.exp(par[1:2]) * softplus
    ba_scr[...] = loga
    bx_scr[...] = _dot(jax.nn.sigmoid(ba), eb_ref[...], HI)
    lx_scr[...] = _dot(loga, ea_ref[...], HI)

    cw = cw_ref[...]
    for b in range(bb):
        conv = xp_scr[b, 8 - (CONV - 1):8 - (CONV - 1) + tt, :] * cw[0:1]
        for w in range(1, CONV):
            conv = conv + xp_scr[b, 8 - (CONV - 1) + w:8 - (CONV - 1) + w + tt, :] * cw[w:w + 1]
        act = _silu(conv)
        for j in range(2 * H):
            a = act[:, j * HD:(j + 1) * HD]
            scale = HD ** -0.5 if j < H else 1.0
            act_scr[b * tt:(b + 1) * tt, j * HD:(j + 1) * HD] = (
                a * lax.rsqrt(jnp.sum(a * a, axis=-1, keepdims=True) + EPS) * scale)
        act_scr[b * tt:(b + 1) * tt, 2 * H * HD:] = act[:, 2 * H * HD:]
        tail = xp_scr[b, tt:tt + 8, :]
        xp_scr[b, 0:8, :] = tail

    row = _iota2((c, c), 0)
    col = _iota2((c, c), 1)
    incl = row >= col
    strict = row > col
    tri_f = incl.astype(F32)
    eye_l = (_iota2((LANES, LANES), 0) == _iota2((LANES, LANES), 1)).astype(F32)

    def chunk(idx, carry):
        r0 = pl.multiple_of(idx * c, c)
        sq = idx // ncs
        gx = _dot(tri_f, lx_scr[pl.ds(r0, c), :], HI)
        gc = _dot(tri_f, ba_scr[pl.ds(r0, c), :], HI)
        gt = _dot_nt(eye_l, gc, HI)
        bx = bx_scr[pl.ds(r0, c), :]
        eg = jnp.exp(gx)
        gl = gx[c - 1:c, :]
        qq = act_scr[pl.ds(r0, c), 0:H * HD]
        kk = act_scr[pl.ds(r0, c), H * HD:2 * H * HD]
        vv = act_scr[pl.ds(r0, c), 2 * H * HD:]
        qd = qq * eg
        ktail = kk * jnp.exp(gl - gx)
        rw = kk * (bx * eg)
        ru = vv * bx
        dtot = jnp.exp(gl)
        for h in range(H):
            sl = slice(h * HD, (h + 1) * HD)
            kh = kk[:, sl]
            kkt = _dot_nt(kh, kh)
            qkt = _dot_nt(qq[:, sl], kh)
            diff = gx[:, h * HD:h * HD + c] - gt[8 + h:9 + h, :]
            ed = jnp.exp(diff)
            lmat = bx[:, h * HD:h * HD + c] * kkt * jnp.where(strict, ed, 0.0)
            amat = qkt * jnp.where(incl, ed, 0.0)
            tinv = _unit_lower_inv(lmat, c)
            s = s_scr[sq, h]
            u = _dot(tinv, ru[:, sl]) - _dot(_dot(tinv, rw[:, sl]), s)
            o_scr[pl.ds(r0, c), sl] = _dot(qd[:, sl], s) + _dot(amat, u)
            s_scr[sq, h] = s * dtot[:, sl] + _dot_tn(ktail[:, sl], u)
        return carry

    lax.fori_loop(0, rows // c, chunk, 0)

    on = _head_rms_gate(o_scr[...], gn_ref[...], z_scr[...])
    o_ref[...] = x + _dot(on.astype(BF16), wo_ref[...])

    @pl.when(t == pl.num_programs(1) - 1)
    def _():
        sout_ref[...] = s_scr[...]
        for b in range(bb):
            cout_ref[b] = xp_scr[b, 8 - (CONV - 1):8, :]


def _gdn_call(x2d, s0, c0, nw, win, cw, par, eb, ea, gn, wo, *, nseq, seqlen, bb, tt, c, row0=0, bcast_state=False):
    assert bb == 1 or tt == seqlen
    assert nseq % bb == 0 and seqlen % tt == 0 and tt % c == 0 and tt % 8 == 0 and row0 % (bb * tt) == 0
    rows = bb * tt
    nt = seqlen // tt
    blk0 = row0 // rows
    s_idx = (lambda b, t: (0, 0, 0, 0)) if bcast_state else (lambda b, t: (b, 0, 0, 0))
    c_idx = (lambda b, t: (0, 0, 0)) if bcast_state else (lambda b, t: (b, 0, 0))
    slab = pltpu.VMEM((rows, D), F32)
    return pl.pallas_call(
        functools.partial(_gdn_kernel, bb=bb, tt=tt, c=c),
        grid=(nseq // bb, nt),
        in_specs=[
            pl.BlockSpec((rows, D), lambda b, t: (blk0 + b * nt + t, 0)),
            pl.BlockSpec((bb, H, HD, HD), s_idx),
            pl.BlockSpec((bb, CONV - 1, QKV), c_idx),
            _const_spec((1, D)),
            _const_spec(win.shape),
            _const_spec((CONV, QKV)),
            _const_spec((2, LANES)),
            _const_spec((LANES, D)),
            _const_spec((LANES, D)),
            _const_spec((1, HD)),
            _const_spec((D, D)),
        ],
        out_specs=[
            pl.BlockSpec((rows, D), lambda b, t: (b * nt + t, 0)),
            pl.BlockSpec((bb, H, HD, HD), lambda b, t: (b, 0, 0, 0)),
            pl.BlockSpec((bb, CONV - 1, QKV), lambda b, t: (b, 0, 0)),
        ],
        out_shape=[
            jax.ShapeDtypeStruct((nseq * seqlen, D), F32),
            jax.ShapeDtypeStruct((nseq, H, HD, HD), F32),
            jax.ShapeDtypeStruct((nseq, CONV - 1, QKV), F32),
        ],
        scratch_shapes=[
            pltpu.VMEM((bb, H, HD, HD), F32),
            pltpu.VMEM((bb, tt + 8, QKV), F32),
            pltpu.VMEM((rows, QKV), F32),
            slab,
            pltpu.VMEM((rows, LANES), F32),
            slab, slab, slab,
        ],
        compiler_params=pltpu.CompilerParams(
            dimension_semantics=("arbitrary", "arbitrary"), vmem_limit_bytes=V7X_VMEM_LIMIT),
        name=f"gdn_mixer_{nseq}x{seqlen}",
    )(x2d, s0, c0, nw, win, cw, par, eb, ea, gn, wo)


def _swiglu(xb, w1_ref, w3_ref, w2_ref):
    acc = None
    for j in range(DFF // FB):
        fs = slice(j * FB, (j + 1) * FB)
        hh = (_silu(_dot(xb, w1_ref[:, fs])) * _dot(xb, w3_ref[:, fs])).astype(BF16)
        part = _dot(hh, w2_ref[fs, :])
        acc = part if acc is None else acc + part
    return acc


def _ffn_kernel(*refs, n_in, nblk_first):
    x_refs = refs[:n_in]
    nw_ref, w1_ref, w3_ref, w2_ref, o_ref = refs[n_in:]
    x = x_refs[0][...]
    if n_in == 2:
        x = jnp.where(pl.program_id(0) < nblk_first, x, x_refs[1][...])
    xb = _rms(x, nw_ref[...]).astype(BF16)
    o_ref[...] = x + _swiglu(xb, w1_ref, w3_ref, w2_ref)


def _two_group_specs(tm, nblk_first, nblk_total):
    last_first = nblk_first - 1
    return [
        pl.BlockSpec((tm, D), lambda i, *_: (jnp.minimum(i, last_first), 0)),
        pl.BlockSpec((tm, D), lambda i, *_: (jnp.maximum(i - nblk_first, 0), 0)),
    ]


def _ffn_call(xs, nw, w1, w3, w2, *, tm):
    nblks = [x.shape[0] // tm for x in xs]
    assert all(x.shape[0] % tm == 0 for x in xs)
    total = sum(nblks)
    if len(xs) == 2:
        x_specs = _two_group_specs(tm, nblks[0], total)
    else:
        x_specs = [pl.BlockSpec((tm, D), lambda i: (i, 0))]
    return pl.pallas_call(
        functools.partial(_ffn_kernel, n_in=len(xs), nblk_first=nblks[0]),
        grid=(total,),
        in_specs=x_specs + [_const_spec((1, D)), _const_spec((D, DFF)), _const_spec((D, DFF)), _const_spec((DFF, D))],
        out_specs=pl.BlockSpec((tm, D), lambda i: (i, 0)),
        out_shape=jax.ShapeDtypeStruct((total * tm, D), F32),
        compiler_params=pltpu.CompilerParams(dimension_semantics=("arbitrary",), vmem_limit_bytes=V7X_VMEM_LIMIT),
        name=f"ffn_dense_{total * tm}",
    )(*xs, nw, w1, w3, w2)


def _router_kernel(xa_ref, xb_ref, nw_ref, wr_ref, eid_ref, gate_ref, rank_ref, cnt_ref, base_scr, *,
                   tm, nblk_first):
    i = pl.program_id(0)

    @pl.when(i == 0)
    def _():
        base_scr[...] = jnp.zeros_like(base_scr)

    x = jnp.where(i < nblk_first, xa_ref[...], xb_ref[...])
    xn = _rms(x, nw_ref[...])
    logits = _dot(xn, wr_ref[...], HI)
    lane = _iota2((tm, LANES), 1).astype(F32)
    neg = jnp.float32(-jnp.inf)
    l1 = jnp.where(lane < NE, logits, neg)
    m1 = jnp.max(l1, axis=-1, keepdims=True)
    i1 = jnp.min(jnp.where(l1 == m1, lane, float(LANES)), axis=-1, keepdims=True)
    l2 = jnp.where(lane == i1, neg, l1)
    m2 = jnp.max(l2, axis=-1, keepdims=True)
    i2 = jnp.min(jnp.where(l2 == m2, lane, float(LANES)), axis=-1, keepdims=True)
    e2 = jnp.exp(m2 - m1)
    g1 = 1.0 / (1.0 + e2)
    g2 = e2 / (1.0 + e2)

    oh1 = (lane == i1).astype(F32)
    oh2 = (lane == i2).astype(F32)
    below = (_iota2((tm, tm), 0) > _iota2((tm, tm), 1)).astype(BF16)
    c1 = _dot(below, oh1.astype(BF16))
    c2 = _dot(below, oh2.astype(BF16))
    n1 = jnp.sum(oh1, axis=0, keepdims=True)
    n2 = jnp.sum(oh2, axis=0, keepdims=True)
    base = base_scr[...]
    r1 = jnp.sum(oh1 * (c1 + base), axis=-1, keepdims=True)
    r2 = jnp.sum(oh2 * (c2 + base + n1), axis=-1, keepdims=True)
    base = base + n1 + n2
    base_scr[...] = base
    cnt_ref[...] = base

    eid_ref[:, 0:1] = i1.astype(jnp.int32)
    eid_ref[:, 1:2] = i2.astype(jnp.int32)
    gate_ref[:, 0:1] = g1
    gate_ref[:, 1:2] = g2
    rank_ref[:, 0:1] = r1.astype(jnp.int32)
    rank_ref[:, 1:2] = r2.astype(jnp.int32)


def _router_call(xa, xb, nw, wr, *, tm):
    na, nb = xa.shape[0] // tm, xb.shape[0] // tm
    n = (na + nb) * tm
    small = lambda dt: jax.ShapeDtypeStruct((n, 2), dt)
    return pl.pallas_call(
        functools.partial(_router_kernel, tm=tm, nblk_first=na),
        grid=(na + nb,),
        in_specs=_two_group_specs(tm, na, na + nb) + [_const_spec((1, D)), _const_spec((D, LANES))],
        out_specs=[
            pl.BlockSpec((tm, 2), lambda i: (i, 0)),
            pl.BlockSpec((tm, 2), lambda i: (i, 0)),
            pl.BlockSpec((tm, 2), lambda i: (i, 0)),
            pl.BlockSpec((1, LANES), lambda i: (0, 0)),
        ],
        out_shape=[
            small(jnp.int32), small(F32), small(jnp.int32),
            jax.ShapeDtypeStruct((1, LANES), F32),
        ],
        scratch_shapes=[pltpu.VMEM((1, LANES), F32)],
        compiler_params=pltpu.CompilerParams(dimension_semantics=("arbitrary",), vmem_limit_bytes=V7X_VMEM_LIMIT),
        name="moe_router",
    )(xa, xb, nw, wr)


def _row_copy(src_ref, src_row, dst_ref, dst_row, sem):
    return pltpu.make_async_copy(src_ref.at[pl.ds(src_row, 1)], dst_ref.at[pl.ds(dst_row, 1)], sem)


def _dispatch_kernel(pos_ref, zst_ref, xa_ref, xb_ref, xs_ref, zero_scr, sem, zsem, *, tm, nblk_first):
    i = pl.program_id(0)

    @pl.when(i == 0)
    def _():
        zero_scr[...] = jnp.zeros_like(zero_scr)

        def zero_copy(e):
            start = pl.multiple_of(zst_ref[e], TM)
            return pltpu.make_async_copy(zero_scr, xs_ref.at[pl.ds(start, TM)], zsem)

        for e in range(2 * NE):
            pl.when(zst_ref[e] >= 0)(lambda e=e: zero_copy(e).start())
        for e in range(2 * NE):
            pl.when(zst_ref[e] >= 0)(lambda e=e: zero_copy(e).wait())

    def scatter_rows(src_ref):
        def issue(r, carry):
            tok = i * tm + r
            _row_copy(src_ref, r, xs_ref, pos_ref[2 * tok], sem).start()
            _row_copy(src_ref, r, xs_ref, pos_ref[2 * tok + 1], sem).start()
            return carry

        lax.fori_loop(0, tm, issue, 0)

        def drain(r, carry):
            _row_copy(src_ref, 0, xs_ref, 0, sem).wait()
            return carry

        lax.fori_loop(0, 2 * tm, drain, 0)

    pl.when(i < nblk_first)(lambda: scatter_rows(xa_ref))
    pl.when(i >= nblk_first)(lambda: scatter_rows(xb_ref))


def _dispatch_call(pos, zst, xa, xb, *, tm, rmax):
    na, nb = xa.shape[0] // tm, xb.shape[0] // tm
    return pl.pallas_call(
        functools.partial(_dispatch_kernel, tm=tm, nblk_first=na),
        grid_spec=pltpu.PrefetchScalarGridSpec(
            num_scalar_prefetch=2,
            grid=(na + nb,),
            in_specs=_two_group_specs(tm, na, na + nb),
            out_specs=pl.BlockSpec(memory_space=pl.ANY),
            scratch_shapes=[pltpu.VMEM((TM, D), F32), pltpu.SemaphoreType.DMA, pltpu.SemaphoreType.DMA],
        ),
        out_shape=jax.ShapeDtypeStruct((rmax, D), F32),
        compiler_params=pltpu.CompilerParams(dimension_semantics=("arbitrary",)),
        name="moe_dispatch",
    )(pos, zst, xa, xb)


def _expert_kernel(texp_ref, tsrc_ref, nact_ref, xs_ref, nw_ref, w1_ref, w3_ref, w2_ref, ys_ref):
    i = pl.program_id(0)

    @pl.when(i < nact_ref[0])
    def _():
        xb = _rms(xs_ref[...], nw_ref[...]).astype(BF16)
        ys_ref[...] = _swiglu(xb, w1_ref, w3_ref, w2_ref)

    @pl.when(i >= nact_ref[0])
    def _():
        ys_ref[...] = jnp.zeros_like(ys_ref)


def _expert_call(texp, tsrc, nact, xs, nw, w1, w3, w2):
    ntile = xs.shape[0] // TM
    return pl.pallas_call(
        _expert_kernel,
        grid_spec=pltpu.PrefetchScalarGridSpec(
            num_scalar_prefetch=3,
            grid=(ntile,),
            in_specs=[
                pl.BlockSpec((TM, D), lambda i, te, ts, na: (ts[i], 0)),
                pl.BlockSpec((1, D), lambda i, te, ts, na: (0, 0)),
                pl.BlockSpec((None, D, DFF), lambda i, te, ts, na: (te[i], 0, 0)),
                pl.BlockSpec((None, D, DFF), lambda i, te, ts, na: (te[i], 0, 0)),
                pl.BlockSpec((None, DFF, D), lambda i, te, ts, na: (te[i], 0, 0)),
            ],
            out_specs=pl.BlockSpec((TM, D), lambda i, te, ts, na: (i, 0)),
        ),
        out_shape=jax.ShapeDtypeStruct((xs.shape[0], D), F32),
        compiler_params=pltpu.CompilerParams(dimension_semantics=("arbitrary",), vmem_limit_bytes=V7X_VMEM_LIMIT),
        name="moe_experts",
    )(texp, tsrc, nact, xs, nw, w1, w3, w2)


def _combine_kernel(pos_ref, x_ref, gate_ref, nf_ref, ys_ref, o_ref, buf, sem, *, tm, tok0):
    i = pl.program_id(0)

    def issue(r, carry):
        tok = tok0 + i * tm + r
        _row_copy(ys_ref, pos_ref[2 * tok], buf.at[0], r, sem).start()
        _row_copy(ys_ref, pos_ref[2 * tok + 1], buf.at[1], r, sem).start()
        return carry

    lax.fori_loop(0, tm, issue, 0)

    def drain(r, carry):
        _row_copy(ys_ref, 0, buf.at[0], 0, sem).wait()
        return carry

    lax.fori_loop(0, 2 * tm, drain, 0)

    g = gate_ref[...]
    hsum = x_ref[...] + (g[:, 0:1] * buf[0] + g[:, 1:2] * buf[1])
    o_ref[...] = _rms(hsum, nf_ref[...])


def _combine_call(pos, x, gate, nf, ys, *, tm, tok0):
    n = x.shape[0]
    blk0 = tok0 // tm
    return pl.pallas_call(
        functools.partial(_combine_kernel, tm=tm, tok0=tok0),
        grid_spec=pltpu.PrefetchScalarGridSpec(
            num_scalar_prefetch=1,
            grid=(n // tm,),
            in_specs=[
                pl.BlockSpec((tm, D), lambda i, *_: (i, 0)),
                pl.BlockSpec((tm, 2), lambda i, *_: (blk0 + i, 0)),
                pl.BlockSpec((1, D), lambda i, *_: (0, 0)),
                pl.BlockSpec(memory_space=pl.ANY),
            ],
            out_specs=pl.BlockSpec((tm, D), lambda i, *_: (i, 0)),
            scratch_shapes=[pltpu.VMEM((2, tm, D), F32), pltpu.SemaphoreType.DMA],
        ),
        out_shape=jax.ShapeDtypeStruct((n, D), F32),
        compiler_params=pltpu.CompilerParams(dimension_semantics=("arbitrary",), vmem_limit_bytes=V7X_VMEM_LIMIT),
        name=f"moe_combine_{n}",
    )(pos, x, gate, nf, ys)


def kernel(x_prompt, x_sample, state_hgrn, state_gdn, state_gdn_conv, meta_tokens, norm_mix, norm_ffn, norm_final,
           hg_w_in, hg_lower_bounds, hg_g_norm, hg_w_o, gdn_w_in, gdn_conv_w, gdn_a_log, gdn_dt_bias, gdn_g_norm,
           gdn_w_o, ffn_w1, ffn_w3, ffn_w2, moe_w_router, moe_w1, moe_w3, moe_w2):
    bp, tp, _ = x_prompt.shape
    bs, ts, _ = x_sample.shape
    np_, ns_ = bp * tp, bs * ts
    n_tok = np_ + ns_
    row = lambda v: v.reshape(1, -1).astype(F32)

    hg_win = hg_w_in.astype(BF16)
    hg_wo = hg_w_o.astype(BF16)
    gdn_win = jnp.concatenate(
        [gdn_w_in[:, :QKV + D], gdn_w_in[:, QKV + D:], jnp.zeros((D, LANES - 2 * H), gdn_w_in.dtype)],
        axis=1).astype(BF16)
    gdn_wo = gdn_w_o.astype(BF16)
    par = jnp.zeros((2, LANES), F32)
    par = par.at[0, H:2 * H].set(gdn_dt_bias.astype(F32)).at[1, H:2 * H].set(gdn_a_log.astype(F32))
    lane_head = jnp.arange(D) // HD
    eb = (jnp.arange(LANES)[:, None] == lane_head[None, :]).astype(F32)
    ea = (jnp.arange(LANES)[:, None] == lane_head[None, :] + H).astype(F32)
    w1d, w3d, w2d = ffn_w1.astype(BF16), ffn_w3.astype(BF16), ffn_w2.astype(BF16)
    w1e, w3e, w2e = moe_w1.astype(BF16), moe_w3.astype(BF16), moe_w2.astype(BF16)
    wr = jnp.concatenate([moe_w_router.astype(F32), jnp.zeros((D, LANES - NE), F32)], axis=1)
    nm0, nm1 = row(norm_mix[0]), row(norm_mix[1])
    nf0, nf1 = row(norm_ffn[0]), row(norm_ffn[1])
    nfin = row(norm_final)
    hg_gn, gdn_gn = row(hg_g_norm), row(gdn_g_norm)
    lbnd = hg_lower_bounds.astype(F32)
    cw = gdn_conv_w.astype(F32)

    hg_args = (nm0, hg_win, lbnd, hg_gn, hg_wo)
    gdn_args = (nm1, gdn_win, cw, par, eb, ea, gdn_gn, gdn_wo)
    xm = meta_tokens.astype(F32)
    xp2 = x_prompt.reshape(np_, D)
    xs2 = x_sample.reshape(ns_, D)

    zero_state = jnp.zeros((1, H, HD, HD), F32)
    h1m, hg_m = _hgrn_call(xm, zero_state, *hg_args, nseq=1, seqlen=N_META, bb=1, tt=N_META)
    h1p, hg_p = _hgrn_call(xp2, hg_m, *hg_args, nseq=bp, seqlen=tp, bb=1, tt=256, bcast_state=True)
    h1s, hg_s = _hgrn_call(xs2, state_hgrn.astype(F32), *hg_args, nseq=bs, seqlen=ts, bb=8, tt=ts)
    h2m = _ffn_call([h1m], nf0, w1d, w3d, w2d, tm=N_META)
    h2 = _ffn_call([h1p, h1s], nf0, w1d, w3d, w2d, tm=256)

    zero_conv = jnp.zeros((1, CONV - 1, QKV), F32)
    _, gdn_m, conv_m = _gdn_call(h2m, zero_state, zero_conv, *gdn_args, nseq=1, seqlen=N_META, bb=1, tt=N_META, c=16)
    h3p, gdn_p, conv_p = _gdn_call(h2, gdn_m, conv_m, *gdn_args, nseq=bp, seqlen=tp, bb=1, tt=256, c=64,
                                   bcast_state=True)
    h3s, gdn_s, conv_s = _gdn_call(h2, state_gdn.astype(F32), state_gdn_conv.astype(F32), *gdn_args,
                                   nseq=bs, seqlen=ts, bb=8, tt=ts, c=8, row0=np_)

    tm = 256
    eid, gate, rank, cnt = _router_call(h3p, h3s, nf1, wr, tm=tm)
    counts = cnt[0, :NE].astype(jnp.int32)
    padded = ((counts + TM - 1) // TM) * TM
    ends = jnp.cumsum(padded)
    offs = ends - padded
    pos = jnp.sum(jnp.where(eid[..., None] == jnp.arange(NE), offs, 0), axis=-1) + rank
    pos = pos.reshape(-1).astype(jnp.int32)
    ntile = (2 * n_tok + NE * (TM - 1) + TM - 1) // TM
    nact = ends[-1] // TM
    tiles = jnp.arange(ntile, dtype=jnp.int32)
    texp_all = jnp.minimum(jnp.sum(tiles[:, None] * TM >= ends[None, :], axis=1), NE - 1).astype(jnp.int32)
    last = jnp.maximum(nact - 1, 0)
    active = tiles < nact
    texp = jnp.where(active, texp_all, texp_all[last]).astype(jnp.int32)
    tsrc = jnp.where(active, tiles, last).astype(jnp.int32)
    tail = nact + jnp.arange(NE)
    zst = jnp.concatenate([jnp.where(padded > 0, ends - TM, -1),
                           jnp.where(tail < ntile, tail * TM, -1)]).astype(jnp.int32)
    xs_sorted = _dispatch_call(pos, zst, h3p, h3s, tm=tm, rmax=ntile * TM)
    ys = _expert_call(texp, tsrc, nact.reshape(1).astype(jnp.int32), xs_sorted, nf1, w1e, w3e, w2e)
    yp = _combine_call(pos, h3p, gate, nfin, ys, tm=tm, tok0=0)
    ysm = _combine_call(pos, h3s, gate, nfin, ys, tm=tm, tok0=np_)

    sd, gd, cd = state_hgrn.dtype, state_gdn.dtype, state_gdn_conv.dtype
    return (yp.reshape(bp, tp, D), ysm.reshape(bs, ts, D),
            hg_p.astype(sd), hg_s.astype(sd),
            gdn_p.astype(gd), gdn_s.astype(gd), conv_p.astype(cd), conv_s.astype(cd))
```

```python
import functools

import jax
import jax.numpy as jnp
from jax import lax
from jax.experimental import pallas as pl
from jax.experimental.pallas import tpu as pltpu

D = 1024
H = 8
HD = 128
QKV = 3 * H * HD
CONV = 4
DFF = 3584
NE = 8
N_META = 16
EPS = 1e-6
LANES = 128
FB = 512
TM = 256
V7X_VMEM_LIMIT = 56 * 1024 * 1024

F32 = jnp.float32
BF16 = jnp.bfloat16
HI = lax.Precision.HIGHEST


def _dot(a, b, precision=None):
    return jnp.dot(a, b, preferred_element_type=F32, precision=precision)


def _dot_nt(a, b, precision=None):
    return lax.dot_general(a, b, (((1,), (1,)), ((), ())), preferred_element_type=F32, precision=precision)


def _dot_tn(a, b, precision=None):
    return lax.dot_general(a, b, (((0,), (0,)), ((), ())), preferred_element_type=F32, precision=precision)


def _split3(x):
    x1 = x.astype(BF16)
    r1 = x - x1.astype(F32)
    x2 = r1.astype(BF16)
    x3 = (r1 - x2.astype(F32)).astype(BF16)
    return x1, x2, x3


def _sel_dot(sel, x):
    x1, x2, x3 = _split3(x)
    return _dot(sel, x1) + _dot(sel, x2) + _dot(sel, x3)


def _sel_dot_nt(sel, x):
    x1, x2, x3 = _split3(x)
    return _dot_nt(sel, x1) + _dot_nt(sel, x2) + _dot_nt(sel, x3)


def _rms(x, g):
    return x * lax.rsqrt(jnp.mean(x * x, axis=-1, keepdims=True) + EPS) * g


def _silu(x):
    return x * jax.nn.sigmoid(x)


def _iota2(shape, dim):
    return lax.broadcasted_iota(jnp.int32, shape, dim)


def _head_rms_gate(o, g, gate):
    parts = []
    for h in range(H):
        oh = o[:, h * HD:(h + 1) * HD]
        parts.append(_rms(oh, g))
    return jnp.concatenate(parts, axis=1) * gate


def _chunk_row(idx, j, nu, c):
    if isinstance(idx, int):
        return (idx * nu + j) * c
    return pl.multiple_of((idx * nu + j) * c, c)


def _chunk_seq(idx, j, nu, ncs):
    return (idx * nu + j) // ncs


def _chunk_ops(tri, c):
    if c % 16 == 0:
        tri_b = tri.astype(BF16)
        return (lambda x: _sel_dot(tri_b, x)), (lambda x: x.astype(BF16))
    tri_f = tri.astype(F32)
    return (lambda x: _dot(tri_f, x, HI)), (lambda x: x)


def _run_groups(group, n):
    if n == 1:
        group(0, 0)
    else:
        lax.fori_loop(0, n, group, 0)


def _const_spec(shape):
    nd = len(shape)
    return pl.BlockSpec(shape, lambda *_: (0,) * nd, pipeline_mode=pl.Buffered(1))


def _hgrn_kernel(x_ref, s0_ref, nw_ref, win_ref, lb_ref, gn_ref, wo_ref,
                 o_ref, sout_ref,
                 st_scr, q_scr, k_scr, lf_scr, v_scr, g_scr, *, bb, tt, c, nu):
    t = pl.program_id(1)
    rows = bb * tt
    ncs = tt // c
    assert ncs == 1 or ncs % nu == 0

    @pl.when(t == 0)
    def _():
        for b in range(bb):
            for h in range(H):
                st_scr[b, h] = s0_ref[b, h].T

    x = x_ref[...]
    xn = _rms(x, nw_ref[...]).astype(BF16)
    lbn = lb_ref[...]
    e = jnp.exp(lbn - jnp.max(lbn, axis=0, keepdims=True))
    lb = e[0:1] / jnp.sum(e, axis=0, keepdims=True)
    q_scr[...] = _silu(_dot(xn, win_ref[:, 0:D])) * HD ** -0.5
    f = lb + (1.0 - lb) * jax.nn.sigmoid(_dot(xn, win_ref[:, D:2 * D]))
    k_scr[...] = 1.0 - f
    lf_scr[...] = jnp.log(f)
    v_scr[...] = _dot(xn, win_ref[:, 2 * D:3 * D])
    g_scr[...] = _silu(_dot(xn, win_ref[:, 3 * D:4 * D]))

    tri = _iota2((c, c), 0) >= _iota2((c, c), 1)
    cumsum, mxu = _chunk_ops(tri, c)

    heads = [slice(h * HD, (h + 1) * HD) for h in range(H)]

    def group(idx, carry):
        qd, dec, sc, kv, oi, r0s, sqs = [], [], [], [], [], [], []
        for j in range(nu):
            r0 = _chunk_row(idx, j, nu, c)
            bc = cumsum(lf_scr[pl.ds(r0, c), :])
            bl = bc[c - 1:c, :]
            kk = k_scr[pl.ds(r0, c), :]
            vv = mxu(v_scr[pl.ds(r0, c), :])
            qdj = mxu(q_scr[pl.ds(r0, c), :] * jnp.exp(bc))
            kinv = mxu(kk * jnp.exp(-bc))
            ktail = mxu(kk * jnp.exp(bl - bc))
            scj = [mxu(jnp.where(tri, _dot_nt(qdj[:, s], kinv[:, s]), 0.0)) for s in heads]
            kv.append([_dot_tn(vv[:, s], ktail[:, s]) for s in heads])
            oi.append([_dot(scj[h], vv[:, heads[h]]) for h in range(H)])
            qd.append(qdj)
            dec.append(jnp.exp(bl))
            r0s.append(r0)
            sqs.append(_chunk_seq(idx, j, nu, ncs))
        own_state = ncs == 1
        loaded = [[st_scr[sqs[j], h] for h in range(H)] for j in range(nu if own_state else 1)]
        outs, finals = [], []
        st = loaded[0]
        for j in range(nu):
            if own_state:
                st = loaded[j]
            outs.append([oi[j][h] + _dot_nt(qd[j][:, heads[h]], mxu(st[h])) for h in range(H)])
            st = [st[h] * dec[j][:, heads[h]] + kv[j][h] for h in range(H)]
            if own_state or j == nu - 1:
                finals.append((sqs[j], st))
        for j in range(nu):
            q_scr[pl.ds(r0s[j], c), :] = jnp.concatenate(outs[j], axis=1)
        for sq, st in finals:
            for h in range(H):
                st_scr[sq, h] = st[h]
        return carry

    _run_groups(group, rows // (nu * c))

    on = _head_rms_gate(q_scr[...], gn_ref[...], g_scr[...])
    o_ref[...] = x + _dot(on.astype(BF16), wo_ref[...])

    @pl.when(t == pl.num_programs(1) - 1)
    def _():
        for b in range(bb):
            for h in range(H):
                sout_ref[b, h] = st_scr[b, h].T


def _hgrn_call(x2d, s0, nw, win, lbnd, gn, wo, *, nseq, seqlen, bb, tt, row0=0, bcast_state=False):
    assert bb == 1 or tt == seqlen
    assert nseq % bb == 0 and seqlen % tt == 0 and row0 % (bb * tt) == 0
    c = 16 if tt % 16 == 0 else 8
    rows = bb * tt
    nu = min(4, rows // c)
    nt = seqlen // tt
    blk0 = row0 // rows
    s_idx = (lambda b, t: (0, 0, 0, 0)) if bcast_state else (lambda b, t: (b, 0, 0, 0))
    slab = pltpu.VMEM((rows, D), F32)
    return pl.pallas_call(
        functools.partial(_hgrn_kernel, bb=bb, tt=tt, c=c, nu=nu),
        grid=(nseq // bb, nt),
        in_specs=[
            pl.BlockSpec((rows, D), lambda b, t: (blk0 + b * nt + t, 0)),
            pl.BlockSpec((bb, H, HD, HD), s_idx),
            _const_spec((1, D)),
            _const_spec((D, 4 * D)),
            _const_spec(lbnd.shape),
            _const_spec((1, HD)),
            _const_spec((D, D)),
        ],
        out_specs=[
            pl.BlockSpec((rows, D), lambda b, t: (b * nt + t, 0)),
            pl.BlockSpec((bb, H, HD, HD), lambda b, t: (b, 0, 0, 0)),
        ],
        out_shape=[
            jax.ShapeDtypeStruct((nseq * seqlen, D), F32),
            jax.ShapeDtypeStruct((nseq, H, HD, HD), F32),
        ],
        scratch_shapes=[pltpu.VMEM((bb, H, HD, HD), F32), slab, slab, slab, slab, slab],
        compiler_params=pltpu.CompilerParams(
            dimension_semantics=("arbitrary", "arbitrary"), vmem_limit_bytes=V7X_VMEM_LIMIT),
        name=f"hgrn_mixer_{nseq}x{seqlen}",
    )(x2d, s0, nw, win, lbnd, gn, wo)


def _neumann_inv(ls, eye, levels, mxu):
    xs = [eye - l for l in ls]
    ps = [mxu(l) for l in ls]
    for _ in range(levels - 1):
        ps = [mxu(_dot(p, p)) for p in ps]
        xs = [x + _dot(mxu(x), p) for x, p in zip(xs, ps)]
    return xs


def _unit_lower_inv(ls, c, mxu):
    row = _iota2((c, c), 0)
    col = _iota2((c, c), 1)
    eye = (row == col).astype(F32)
    if c <= 16:
        return _neumann_inv(ls, eye, c.bit_length() - 1, mxu)
    sub = 16
    assert c == 4 * sub
    diag = (row // sub) == (col // sub)
    lds = [jnp.where(diag, l, 0.0) for l in ls]
    dinvs = [mxu(d) for d in _neumann_inv(lds, eye, 4, mxu)]
    ms = [_dot(d, mxu(l - ld)) for d, l, ld in zip(dinvs, ls, lds)]
    mbs = [mxu(m) for m in ms]
    m2s = [mxu(_dot(m, m)) for m in mbs]
    ys = [(eye - m) + _dot(mxu(eye - m), m2) for m, m2 in zip(ms, m2s)]
    return [_dot(mxu(y), d) for y, d in zip(ys, dinvs)]


def _gdn_kernel(x_ref, s0_ref, c0_ref, nw_ref, win_ref, cw_ref, par_ref, gn_ref, wo_ref,
                o_ref, sout_ref, cout_ref,
                s_scr, xp_scr, act_scr, z_scr, la_scr, be_scr, o_scr, *, bb, tt, c, nu):
    t = pl.program_id(1)
    rows = bb * tt
    ncs = tt // c
    assert ncs == 1 or ncs % nu == 0

    @pl.when(t == 0)
    def _():
        s_scr[...] = s0_ref[...]
        for b in range(bb):
            xp_scr[b, 8 - (CONV - 1):8, :] = c0_ref[b]

    x = x_ref[...]
    xn = _rms(x, nw_ref[...]).astype(BF16)
    pre = _dot(xn, win_ref[:, 0:QKV])
    for b in range(bb):
        xp_scr[b, 8:8 + tt, :] = pre[b * tt:(b + 1) * tt]
    z_scr[...] = _silu(_dot(xn, win_ref[:, QKV:QKV + D]))
    ba = _dot(xn, win_ref[:, QKV + D:QKV + D + LANES])
    par = par_ref[...]
    sp = ba + par[0:1]
    softplus = jnp.maximum(sp, 0.0) + jnp.log(1.0 + jnp.exp(-jnp.abs(sp)))
    la_scr[...] = -jnp.exp(par[1:2]) * softplus
    be_scr[...] = jax.nn.sigmoid(ba)

    cw = cw_ref[...]
    for b in range(bb):
        conv = xp_scr[b, 8 - (CONV - 1):8 - (CONV - 1) + tt, :] * cw[0:1]
        for w in range(1, CONV):
            conv = conv + xp_scr[b, 8 - (CONV - 1) + w:8 - (CONV - 1) + w + tt, :] * cw[w:w + 1]
        act = _silu(conv)
        for j in range(2 * H):
            a = act[:, j * HD:(j + 1) * HD]
            scale = HD ** -0.5 if j < H else 1.0
            act_scr[b * tt:(b + 1) * tt, j * HD:(j + 1) * HD] = (
                a * lax.rsqrt(jnp.sum(a * a, axis=-1, keepdims=True) + EPS) * scale)
        act_scr[b * tt:(b + 1) * tt, 2 * H * HD:] = act[:, 2 * H * HD:]
        tail = xp_scr[b, tt:tt + 8, :]
        xp_scr[b, 0:8, :] = tail

    row = _iota2((c, c), 0)
    col = _iota2((c, c), 1)
    incl = row >= col
    strict = row > col
    cumsum, mxu = _chunk_ops(incl, c)
    if c % 16 == 0:
        eye_l = (_iota2((LANES, LANES), 0) == _iota2((LANES, LANES), 1)).astype(BF16)
        transpose = lambda m: _sel_dot_nt(eye_l, m)
    else:
        eye_l = (_iota2((LANES, LANES), 0) == _iota2((LANES, LANES), 1)).astype(F32)
        transpose = lambda m: _dot_nt(eye_l, m, HI)

    heads = [slice(h * HD, (h + 1) * HD) for h in range(H)]

    def group(idx, carry):
        qd, ktail, dtot, lmat, amat, rwu, r0s, sqs = [], [], [], [], [], [], [], []
        for j in range(nu):
            r0 = _chunk_row(idx, j, nu, c)
            gc = cumsum(la_scr[pl.ds(r0, c), :])
            gt = transpose(gc)
            be = be_scr[pl.ds(r0, c), :]
            eg = jnp.exp(gc)
            gl = gc[c - 1:c, :]
            et = jnp.exp(gl - gc)
            dt = jnp.exp(gl)
            qq = act_scr[pl.ds(r0, c), 0:H * HD]
            kk = act_scr[pl.ds(r0, c), H * HD:2 * H * HD]
            vv = act_scr[pl.ds(r0, c), 2 * H * HD:]
            qd_j, ktail_j, dtot_j = [], [], []
            for h, s in enumerate(heads):
                g_col = gc[:, 8 + h:9 + h]
                eg_col = eg[:, 8 + h:9 + h]
                be_col = be[:, h:h + 1]
                kh = mxu(kk[:, s])
                qk = _dot_nt(jnp.concatenate([mxu(qq[:, s]), kh], axis=0), kh)
                ed = jnp.exp(g_col - gt[8 + h:9 + h, :])
                lmat.append(be_col * qk[c:] * jnp.where(strict, ed, 0.0))
                amat.append(mxu(qk[:c] * jnp.where(incl, ed, 0.0)))
                rwu.append(mxu(jnp.concatenate([kk[:, s] * (be_col * eg_col), vv[:, s] * be_col], axis=1)))
                qd_j.append(mxu(qq[:, s] * eg_col))
                ktail_j.append(mxu(kk[:, s] * et[:, 8 + h:9 + h]))
                dtot_j.append(dt[:, 8 + h:9 + h])
            qd.append(qd_j)
            ktail.append(ktail_j)
            dtot.append(dtot_j)
            r0s.append(r0)
            sqs.append(_chunk_seq(idx, j, nu, ncs))
        tinv = _unit_lower_inv(lmat, c, mxu)
        wu = [_dot(mxu(t_), r_) for t_, r_ in zip(tinv, rwu)]

        own_state = ncs == 1
        loaded = [[s_scr[sqs[j], h] for h in range(H)] for j in range(nu if own_state else 1)]
        outs, finals = [], []
        st = loaded[0]
        for j in range(nu):
            if own_state:
                st = loaded[j]
            o_j, st_next = [], []
            for h in range(H):
                wuh = wu[j * H + h]
                qw = _dot(jnp.concatenate([qd[j][h], mxu(wuh[:, :HD])], axis=0), mxu(st[h]))
                u = wuh[:, HD:] - qw[c:]
                ub = mxu(u)
                o_j.append(qw[:c] + _dot(amat[j * H + h], ub))
                st_next.append(st[h] * dtot[j][h] + _dot_tn(ktail[j][h], ub))
            st = st_next
            outs.append(o_j)
            if own_state or j == nu - 1:
                finals.append((sqs[j], st))
        for j in range(nu):
            o_scr[pl.ds(r0s[j], c), :] = jnp.concatenate(outs[j], axis=1)
        for sq, st in finals:
            for h in range(H):
                s_scr[sq, h] = st[h]
        return carry

    _run_groups(group, rows // (nu * c))

    on = _head_rms_gate(o_scr[...], gn_ref[...], z_scr[...])
    o_ref[...] = x + _dot(on.astype(BF16), wo_ref[...])

    @pl.when(t == pl.num_programs(1) - 1)
    def _():
        sout_ref[...] = s_scr[...]
        for b in range(bb):
            cout_ref[b] = xp_scr[b, 8 - (CONV - 1):8, :]


def _gdn_call(x2d, s0, c0, nw, win, cw, par, gn, wo, *, nseq, seqlen, bb, tt, c, nu=2, row0=0,
              bcast_state=False):
    assert bb == 1 or tt == seqlen
    assert nseq % bb == 0 and seqlen % tt == 0 and tt % c == 0 and tt % 8 == 0 and row0 % (bb * tt) == 0
    rows = bb * tt
    nt = seqlen // tt
    blk0 = row0 // rows
    s_idx = (lambda b, t: (0, 0, 0, 0)) if bcast_state else (lambda b, t: (b, 0, 0, 0))
    c_idx = (lambda b, t: (0, 0, 0)) if bcast_state else (lambda b, t: (b, 0, 0))
    slab = pltpu.VMEM((rows, D), F32)
    return pl.pallas_call(
        functools.partial(_gdn_kernel, bb=bb, tt=tt, c=c, nu=min(nu, bb * tt // c)),
        grid=(nseq // bb, nt),
        in_specs=[
            pl.BlockSpec((rows, D), lambda b, t: (blk0 + b * nt + t, 0)),
            pl.BlockSpec((bb, H, HD, HD), s_idx),
            pl.BlockSpec((bb, CONV - 1, QKV), c_idx),
            _const_spec((1, D)),
            _const_spec(win.shape),
            _const_spec((CONV, QKV)),
            _const_spec((2, LANES)),
            _const_spec((1, HD)),
            _const_spec((D, D)),
        ],
        out_specs=[
            pl.BlockSpec((rows, D), lambda b, t: (b * nt + t, 0)),
            pl.BlockSpec((bb, H, HD, HD), lambda b, t: (b, 0, 0, 0)),
            pl.BlockSpec((bb, CONV - 1, QKV), lambda b, t: (b, 0, 0)),
        ],
        out_shape=[
            jax.ShapeDtypeStruct((nseq * seqlen, D), F32),
            jax.ShapeDtypeStruct((nseq, H, HD, HD), F32),
            jax.ShapeDtypeStruct((nseq, CONV - 1, QKV), F32),
        ],
        scratch_shapes=[
            pltpu.VMEM((bb, H, HD, HD), F32),
            pltpu.VMEM((bb, tt + 8, QKV), F32),
            pltpu.VMEM((rows, QKV), F32),
            slab,
            pltpu.VMEM((rows, LANES), F32),
            pltpu.VMEM((rows, LANES), F32),
            slab,
        ],
        compiler_params=pltpu.CompilerParams(
            dimension_semantics=("arbitrary", "arbitrary"), vmem_limit_bytes=V7X_VMEM_LIMIT),
        name=f"gdn_mixer_{nseq}x{seqlen}",
    )(x2d, s0, c0, nw, win, cw, par, gn, wo)


def _swiglu(xb, w1_ref, w3_ref, w2_ref):
    acc = None
    for j in range(DFF // FB):
        fs = slice(j * FB, (j + 1) * FB)
        hh = (_silu(_dot(xb, w1_ref[:, fs])) * _dot(xb, w3_ref[:, fs])).astype(BF16)
        part = _dot(hh, w2_ref[fs, :])
        acc = part if acc is None else acc + part
    return acc


def _ffn_kernel(*refs, n_in, nblk_first):
    x_refs = refs[:n_in]
    nw_ref, w1_ref, w3_ref, w2_ref, o_ref = refs[n_in:]
    x = x_refs[0][...]
    if n_in == 2:
        x = jnp.where(pl.program_id(0) < nblk_first, x, x_refs[1][...])
    xb = _rms(x, nw_ref[...]).astype(BF16)
    o_ref[...] = x + _swiglu(xb, w1_ref, w3_ref, w2_ref)


def _two_group_specs(tm, nblk_first, nblk_total):
    last_first = nblk_first - 1
    return [
        pl.BlockSpec((tm, D), lambda i, *_: (jnp.minimum(i, last_first), 0)),
        pl.BlockSpec((tm, D), lambda i, *_: (jnp.maximum(i - nblk_first, 0), 0)),
    ]


def _ffn_call(xs, nw, w1, w3, w2, *, tm):
    nblks = [x.shape[0] // tm for x in xs]
    assert all(x.shape[0] % tm == 0 for x in xs)
    total = sum(nblks)
    if len(xs) == 2:
        x_specs = _two_group_specs(tm, nblks[0], total)
    else:
        x_specs = [pl.BlockSpec((tm, D), lambda i: (i, 0))]
    return pl.pallas_call(
        functools.partial(_ffn_kernel, n_in=len(xs), nblk_first=nblks[0]),
        grid=(total,),
        in_specs=x_specs + [_const_spec((1, D)), _const_spec((D, DFF)), _const_spec((D, DFF)), _const_spec((DFF, D))],
        out_specs=pl.BlockSpec((tm, D), lambda i: (i, 0)),
        out_shape=jax.ShapeDtypeStruct((total * tm, D), F32),
        compiler_params=pltpu.CompilerParams(dimension_semantics=("arbitrary",), vmem_limit_bytes=V7X_VMEM_LIMIT),
        name=f"ffn_dense_{total * tm}",
    )(*xs, nw, w1, w3, w2)


def _router_kernel(xa_ref, xb_ref, nw_ref, wr_ref, eid_ref, gate_ref, rank_ref, cnt_ref, base_scr, *,
                   tm, nblk_first):
    i = pl.program_id(0)

    @pl.when(i == 0)
    def _():
        base_scr[...] = jnp.zeros_like(base_scr)

    x = jnp.where(i < nblk_first, xa_ref[...], xb_ref[...])
    xn = _rms(x, nw_ref[...])
    logits = _dot(xn, wr_ref[...], HI)
    lane = _iota2((tm, LANES), 1).astype(F32)
    neg = jnp.float32(-jnp.inf)
    l1 = jnp.where(lane < NE, logits, neg)
    m1 = jnp.max(l1, axis=-1, keepdims=True)
    i1 = jnp.min(jnp.where(l1 == m1, lane, float(LANES)), axis=-1, keepdims=True)
    l2 = jnp.where(lane == i1, neg, l1)
    m2 = jnp.max(l2, axis=-1, keepdims=True)
    i2 = jnp.min(jnp.where(l2 == m2, lane, float(LANES)), axis=-1, keepdims=True)
    e2 = jnp.exp(m2 - m1)
    g1 = 1.0 / (1.0 + e2)
    g2 = e2 / (1.0 + e2)

    oh1 = (lane == i1).astype(F32)
    oh2 = (lane == i2).astype(F32)
    below = (_iota2((tm, tm), 0) > _iota2((tm, tm), 1)).astype(BF16)
    c1 = _dot(below, oh1.astype(BF16))
    c2 = _dot(below, oh2.astype(BF16))
    n1 = jnp.sum(oh1, axis=0, keepdims=True)
    n2 = jnp.sum(oh2, axis=0, keepdims=True)
    base = base_scr[...]
    r1 = jnp.sum(oh1 * (c1 + base), axis=-1, keepdims=True)
    r2 = jnp.sum(oh2 * (c2 + base + n1), axis=-1, keepdims=True)
    base = base + n1 + n2
    base_scr[...] = base
    cnt_ref[...] = base

    eid_ref[:, 0:1] = i1.astype(jnp.int32)
    eid_ref[:, 1:2] = i2.astype(jnp.int32)
    gate_ref[:, 0:1] = g1
    gate_ref[:, 1:2] = g2
    rank_ref[:, 0:1] = r1.astype(jnp.int32)
    rank_ref[:, 1:2] = r2.astype(jnp.int32)


def _router_call(xa, xb, nw, wr, *, tm):
    na, nb = xa.shape[0] // tm, xb.shape[0] // tm
    n = (na + nb) * tm
    small = lambda dt: jax.ShapeDtypeStruct((n, 2), dt)
    return pl.pallas_call(
        functools.partial(_router_kernel, tm=tm, nblk_first=na),
        grid=(na + nb,),
        in_specs=_two_group_specs(tm, na, na + nb) + [_const_spec((1, D)), _const_spec((D, LANES))],
        out_specs=[
            pl.BlockSpec((tm, 2), lambda i: (i, 0)),
            pl.BlockSpec((tm, 2), lambda i: (i, 0)),
            pl.BlockSpec((tm, 2), lambda i: (i, 0)),
            pl.BlockSpec((1, LANES), lambda i: (0, 0)),
        ],
        out_shape=[
            small(jnp.int32), small(F32), small(jnp.int32),
            jax.ShapeDtypeStruct((1, LANES), F32),
        ],
        scratch_shapes=[pltpu.VMEM((1, LANES), F32)],
        compiler_params=pltpu.CompilerParams(dimension_semantics=("arbitrary",), vmem_limit_bytes=V7X_VMEM_LIMIT),
        name="moe_router",
    )(xa, xb, nw, wr)


def _row_copy(src_ref, src_row, dst_ref, dst_row, sem):
    return pltpu.make_async_copy(src_ref.at[pl.ds(src_row, 1)], dst_ref.at[pl.ds(dst_row, 1)], sem)


def _dispatch_kernel(pos_ref, zst_ref, xa_ref, xb_ref, xs_ref, zero_scr, sem, zsem, *, tm, nblk_first):
    i = pl.program_id(0)

    @pl.when(i == 0)
    def _():
        zero_scr[...] = jnp.zeros_like(zero_scr)

        def zero_copy(e):
            start = pl.multiple_of(zst_ref[e], TM)
            return pltpu.make_async_copy(zero_scr, xs_ref.at[pl.ds(start, TM)], zsem)

        for e in range(2 * NE):
            pl.when(zst_ref[e] >= 0)(lambda e=e: zero_copy(e).start())
        for e in range(2 * NE):
            pl.when(zst_ref[e] >= 0)(lambda e=e: zero_copy(e).wait())

    def scatter_rows(src_ref):
        def issue(r, carry):
            tok = i * tm + r
            _row_copy(src_ref, r, xs_ref, pos_ref[2 * tok], sem).start()
            _row_copy(src_ref, r, xs_ref, pos_ref[2 * tok + 1], sem).start()
            return carry

        lax.fori_loop(0, tm, issue, 0)

        def drain(r, carry):
            _row_copy(src_ref, 0, xs_ref, 0, sem).wait()
            return carry

        lax.fori_loop(0, 2 * tm, drain, 0)

    pl.when(i < nblk_first)(lambda: scatter_rows(xa_ref))
    pl.when(i >= nblk_first)(lambda: scatter_rows(xb_ref))


def _dispatch_call(pos, zst, xa, xb, *, tm, rmax):
    na, nb = xa.shape[0] // tm, xb.shape[0] // tm
    return pl.pallas_call(
        functools.partial(_dispatch_kernel, tm=tm, nblk_first=na),
        grid_spec=pltpu.PrefetchScalarGridSpec(
            num_scalar_prefetch=2,
            grid=(na + nb,),
            in_specs=_two_group_specs(tm, na, na + nb),
            out_specs=pl.BlockSpec(memory_space=pl.ANY),
            scratch_shapes=[pltpu.VMEM((TM, D), F32), pltpu.SemaphoreType.DMA, pltpu.SemaphoreType.DMA],
        ),
        out_shape=jax.ShapeDtypeStruct((rmax, D), F32),
        compiler_params=pltpu.CompilerParams(dimension_semantics=("arbitrary",)),
        name="moe_dispatch",
    )(pos, zst, xa, xb)


def _expert_kernel(texp_ref, tsrc_ref, nact_ref, xs_ref, nw_ref, w1_ref, w3_ref, w2_ref, ys_ref):
    i = pl.program_id(0)

    @pl.when(i < nact_ref[0])
    def _():
        xb = _rms(xs_ref[...], nw_ref[...]).astype(BF16)
        ys_ref[...] = _swiglu(xb, w1_ref, w3_ref, w2_ref)

    @pl.when(i >= nact_ref[0])
    def _():
        ys_ref[...] = jnp.zeros_like(ys_ref)


def _expert_call(texp, tsrc, nact, xs, nw, w1, w3, w2):
    ntile = xs.shape[0] // TM
    return pl.pallas_call(
        _expert_kernel,
        grid_spec=pltpu.PrefetchScalarGridSpec(
            num_scalar_prefetch=3,
            grid=(ntile,),
            in_specs=[
                pl.BlockSpec((TM, D), lambda i, te, ts, na: (ts[i], 0)),
                pl.BlockSpec((1, D), lambda i, te, ts, na: (0, 0)),
                pl.BlockSpec((None, D, DFF), lambda i, te, ts, na: (te[i], 0, 0)),
                pl.BlockSpec((None, D, DFF), lambda i, te, ts, na: (te[i], 0, 0)),
                pl.BlockSpec((None, DFF, D), lambda i, te, ts, na: (te[i], 0, 0)),
            ],
            out_specs=pl.BlockSpec((TM, D), lambda i, te, ts, na: (i, 0)),
        ),
        out_shape=jax.ShapeDtypeStruct((xs.shape[0], D), F32),
        compiler_params=pltpu.CompilerParams(dimension_semantics=("arbitrary",), vmem_limit_bytes=V7X_VMEM_LIMIT),
        name="moe_experts",
    )(texp, tsrc, nact, xs, nw, w1, w3, w2)


def _combine_kernel(pos_ref, x_ref, gate_ref, nf_ref, ys_ref, o_ref, buf, sem, *, tm, tok0):
    i = pl.program_id(0)

    def issue(r, carry):
        tok = tok0 + i * tm + r
        _row_copy(ys_ref, pos_ref[2 * tok], buf.at[0], r, sem).start()
        _row_copy(ys_ref, pos_ref[2 * tok + 1], buf.at[1], r, sem).start()
        return carry

    lax.fori_loop(0, tm, issue, 0)

    def drain(r, carry):
        _row_copy(ys_ref, 0, buf.at[0], 0, sem).wait()
        return carry

    lax.fori_loop(0, 2 * tm, drain, 0)

    g = gate_ref[...]
    hsum = x_ref[...] + (g[:, 0:1] * buf[0] + g[:, 1:2] * buf[1])
    o_ref[...] = _rms(hsum, nf_ref[...])


def _combine_call(pos, x, gate, nf, ys, *, tm, tok0):
    n = x.shape[0]
    blk0 = tok0 // tm
    return pl.pallas_call(
        functools.partial(_combine_kernel, tm=tm, tok0=tok0),
        grid_spec=pltpu.PrefetchScalarGridSpec(
            num_scalar_prefetch=1,
            grid=(n // tm,),
            in_specs=[
                pl.BlockSpec((tm, D), lambda i, *_: (i, 0)),
                pl.BlockSpec((tm, 2), lambda i, *_: (blk0 + i, 0)),
                pl.BlockSpec((1, D), lambda i, *_: (0, 0)),
                pl.BlockSpec(memory_space=pl.ANY),
            ],
            out_specs=pl.BlockSpec((tm, D), lambda i, *_: (i, 0)),
            scratch_shapes=[pltpu.VMEM((2, tm, D), F32), pltpu.SemaphoreType.DMA],
        ),
        out_shape=jax.ShapeDtypeStruct((n, D), F32),
        compiler_params=pltpu.CompilerParams(dimension_semantics=("arbitrary",), vmem_limit_bytes=V7X_VMEM_LIMIT),
        name=f"moe_combine_{n}",
    )(pos, x, gate, nf, ys)


def kernel(x_prompt, x_sample, state_hgrn, state_gdn, state_gdn_conv, meta_tokens, norm_mix, norm_ffn, norm_final,
           hg_w_in, hg_lower_bounds, hg_g_norm, hg_w_o, gdn_w_in, gdn_conv_w, gdn_a_log, gdn_dt_bias, gdn_g_norm,
           gdn_w_o, ffn_w1, ffn_w3, ffn_w2, moe_w_router, moe_w1, moe_w3, moe_w2):
    bp, tp, _ = x_prompt.shape
    bs, ts, _ = x_sample.shape
    np_, ns_ = bp * tp, bs * ts
    n_tok = np_ + ns_
    row = lambda v: v.reshape(1, -1).astype(F32)

    hg_win = hg_w_in.astype(BF16)
    hg_wo = hg_w_o.astype(BF16)
    gdn_win = jnp.concatenate(
        [gdn_w_in[:, :QKV + D], gdn_w_in[:, QKV + D:], jnp.zeros((D, LANES - 2 * H), gdn_w_in.dtype)],
        axis=1).astype(BF16)
    gdn_wo = gdn_w_o.astype(BF16)
    par = jnp.zeros((2, LANES), F32)
    par = par.at[0, H:2 * H].set(gdn_dt_bias.astype(F32)).at[1, H:2 * H].set(gdn_a_log.astype(F32))
    w1d, w3d, w2d = ffn_w1.astype(BF16), ffn_w3.astype(BF16), ffn_w2.astype(BF16)
    w1e, w3e, w2e = moe_w1.astype(BF16), moe_w3.astype(BF16), moe_w2.astype(BF16)
    wr = jnp.concatenate([moe_w_router.astype(F32), jnp.zeros((D, LANES - NE), F32)], axis=1)
    nm0, nm1 = row(norm_mix[0]), row(norm_mix[1])
    nf0, nf1 = row(norm_ffn[0]), row(norm_ffn[1])
    nfin = row(norm_final)
    hg_gn, gdn_gn = row(hg_g_norm), row(gdn_g_norm)
    lbnd = hg_lower_bounds.astype(F32)
    cw = gdn_conv_w.astype(F32)

    hg_args = (nm0, hg_win, lbnd, hg_gn, hg_wo)
    gdn_args = (nm1, gdn_win, cw, par, gdn_gn, gdn_wo)
    xm = meta_tokens.astype(F32)
    xp2 = x_prompt.reshape(np_, D)
    xs2 = x_sample.reshape(ns_, D)

    zero_state = jnp.zeros((1, H, HD, HD), F32)
    h1m, hg_m = _hgrn_call(xm, zero_state, *hg_args, nseq=1, seqlen=N_META, bb=1, tt=N_META)
    h1p, hg_p = _hgrn_call(xp2, hg_m, *hg_args, nseq=bp, seqlen=tp, bb=1, tt=256, bcast_state=True)
    h1s, hg_s = _hgrn_call(xs2, state_hgrn.astype(F32), *hg_args, nseq=bs, seqlen=ts, bb=8, tt=ts)
    h2m = _ffn_call([h1m], nf0, w1d, w3d, w2d, tm=N_META)
    h2 = _ffn_call([h1p, h1s], nf0, w1d, w3d, w2d, tm=256)

    zero_conv = jnp.zeros((1, CONV - 1, QKV), F32)
    _, gdn_m, conv_m = _gdn_call(h2m, zero_state, zero_conv, *gdn_args, nseq=1, seqlen=N_META, bb=1, tt=N_META, c=16)
    h3p, gdn_p, conv_p = _gdn_call(h2, gdn_m, conv_m, *gdn_args, nseq=bp, seqlen=tp, bb=1, tt=256, c=64, nu=4,
                                   bcast_state=True)
    h3s, gdn_s, conv_s = _gdn_call(h2, state_gdn.astype(F32), state_gdn_conv.astype(F32), *gdn_args,
                                   nseq=bs, seqlen=ts, bb=8, tt=ts, c=8, nu=4, row0=np_)

    tm = 256
    eid, gate, rank, cnt = _router_call(h3p, h3s, nf1, wr, tm=tm)
    counts = cnt[0, :NE].astype(jnp.int32)
    padded = ((counts + TM - 1) // TM) * TM
    ends = jnp.cumsum(padded)
    offs = ends - padded
    pos = jnp.sum(jnp.where(eid[..., None] == jnp.arange(NE), offs, 0), axis=-1) + rank
    pos = pos.reshape(-1).astype(jnp.int32)
    ntile = (2 * n_tok + NE * (TM - 1) + TM - 1) // TM
    nact = ends[-1] // TM
    tiles = jnp.arange(ntile, dtype=jnp.int32)
    texp_all = jnp.minimum(jnp.sum(tiles[:, None] * TM >= ends[None, :], axis=1), NE - 1).astype(jnp.int32)
    last = jnp.maximum(nact - 1, 0)
    active = tiles < nact
    texp = jnp.where(active, texp_all, texp_all[last]).astype(jnp.int32)
    tsrc = jnp.where(active, tiles, last).astype(jnp.int32)
    tail = nact + jnp.arange(NE)
    zst = jnp.concatenate([jnp.where(padded > 0, ends - TM, -1),
                           jnp.where(tail < ntile, tail * TM, -1)]).astype(jnp.int32)
    xs_sorted = _dispatch_call(pos, zst, h3p, h3s, tm=tm, rmax=ntile * TM)
    ys = _expert_call(texp, tsrc, nact.reshape(1).astype(jnp.int32), xs_sorted, nf1, w1e, w3e, w2e)
    yp = _combine_call(pos, h3p, gate, nfin, ys, tm=tm, tok0=0)
    ysm = _combine_call(pos, h3s, gate, nfin, ys, tm=tm, tok0=np_)

    sd, gd, cd = state_hgrn.dtype, state_gdn.dtype, state_gdn_conv.dtype
    return (yp.reshape(bp, tp, D), ysm.reshape(bs, ts, D),
            hg_p.astype(sd), hg_s.astype(sd),
            gdn_p.astype(gd), gdn_s.astype(gd), conv_p.astype(cd), conv_s.astype(cd))
```

```python
import functools

import jax
import jax.numpy as jnp
from jax import lax
from jax.experimental import pallas as pl
from jax.experimental.pallas import tpu as pltpu

D = 1024
H = 8
HD = 128
QKV = 3 * H * HD
CONV = 4
DFF = 3584
NE = 8
N_META = 16
EPS = 1e-6
LANES = 128
FB = 512
TM = 256
SEG = 8
V7X_VMEM_LIMIT = 56 * 1024 * 1024

F32 = jnp.float32
BF16 = jnp.bfloat16
HI = lax.Precision.HIGHEST


def _dot(a, b, precision=None):
    return jnp.dot(a, b, preferred_element_type=F32, precision=precision)


def _dot_nt(a, b, precision=None):
    return lax.dot_general(a, b, (((1,), (1,)), ((), ())), preferred_element_type=F32, precision=precision)


def _dot_tn(a, b, precision=None):
    return lax.dot_general(a, b, (((0,), (0,)), ((), ())), preferred_element_type=F32, precision=precision)


def _split3(x):
    x1 = x.astype(BF16)
    r1 = x - x1.astype(F32)
    x2 = r1.astype(BF16)
    x3 = (r1 - x2.astype(F32)).astype(BF16)
    return x1, x2, x3


def _sel_dot(sel, x):
    x1, x2, x3 = _split3(x)
    return _dot(sel, x1) + _dot(sel, x2) + _dot(sel, x3)


def _sel_dot_nt(sel, x):
    x1, x2, x3 = _split3(x)
    return _dot_nt(sel, x1) + _dot_nt(sel, x2) + _dot_nt(sel, x3)


def _rms(x, g):
    return x * lax.rsqrt(jnp.mean(x * x, axis=-1, keepdims=True) + EPS) * g


def _silu(x):
    return x * jax.nn.sigmoid(x)


def _iota2(shape, dim):
    return lax.broadcasted_iota(jnp.int32, shape, dim)


def _head_rms_gate(o, g, gate):
    parts = []
    for h in range(H):
        oh = o[:, h * HD:(h + 1) * HD]
        parts.append(_rms(oh, g))
    return jnp.concatenate(parts, axis=1) * gate


def _chunk_row(idx, j, nu, c):
    if isinstance(idx, int):
        return (idx * nu + j) * c
    return pl.multiple_of((idx * nu + j) * c, c)


def _chunk_seq(idx, j, nu, ncs):
    return (idx * nu + j) // ncs


def _chunk_ops(tri, c):
    if c % 16 == 0:
        tri_b = tri.astype(BF16)
        return (lambda x: _sel_dot(tri_b, x)), (lambda x: x.astype(BF16))
    tri_f = tri.astype(F32)
    return (lambda x: _dot(tri_f, x, HI)), (lambda x: x)


def _run_groups(group, n):
    if n == 1:
        group(0, 0)
    else:
        lax.fori_loop(0, n, group, 0)


def _const_spec(shape):
    nd = len(shape)
    return pl.BlockSpec(shape, lambda *_: (0,) * nd, pipeline_mode=pl.Buffered(1))


def _hgrn_kernel(x_ref, s0_ref, nw_ref, win_ref, lb_ref, gn_ref, wo_ref,
                 o_ref, sout_ref,
                 st_scr, q_scr, k_scr, lf_scr, v_scr, g_scr, *, bb, tt, c, nu):
    t = pl.program_id(1)
    rows = bb * tt
    ncs = tt // c
    assert ncs == 1 or ncs % nu == 0

    @pl.when(t == 0)
    def _():
        for b in range(bb):
            for h in range(H):
                st_scr[b, h] = s0_ref[b, h].T

    x = x_ref[...]
    xn = _rms(x, nw_ref[...]).astype(BF16)
    lbn = lb_ref[...]
    e = jnp.exp(lbn - jnp.max(lbn, axis=0, keepdims=True))
    lb = e[0:1] / jnp.sum(e, axis=0, keepdims=True)
    q_scr[...] = _silu(_dot(xn, win_ref[:, 0:D])) * HD ** -0.5
    f = lb + (1.0 - lb) * jax.nn.sigmoid(_dot(xn, win_ref[:, D:2 * D]))
    k_scr[...] = 1.0 - f
    lf_scr[...] = jnp.log(f)
    v_scr[...] = _dot(xn, win_ref[:, 2 * D:3 * D])
    g_scr[...] = _silu(_dot(xn, win_ref[:, 3 * D:4 * D]))

    tri = _iota2((c, c), 0) >= _iota2((c, c), 1)
    cumsum, mxu = _chunk_ops(tri, c)

    heads = [slice(h * HD, (h + 1) * HD) for h in range(H)]

    def group(idx, carry):
        qd, dec, sc, kv, oi, r0s, sqs = [], [], [], [], [], [], []
        for j in range(nu):
            r0 = _chunk_row(idx, j, nu, c)
            bc = cumsum(lf_scr[pl.ds(r0, c), :])
            bl = bc[c - 1:c, :]
            kk = k_scr[pl.ds(r0, c), :]
            vv = mxu(v_scr[pl.ds(r0, c), :])
            qdj = mxu(q_scr[pl.ds(r0, c), :] * jnp.exp(bc))
            kinv = mxu(kk * jnp.exp(-bc))
            ktail = mxu(kk * jnp.exp(bl - bc))
            scj = [mxu(jnp.where(tri, _dot_nt(qdj[:, s], kinv[:, s]), 0.0)) for s in heads]
            kv.append([_dot_tn(vv[:, s], ktail[:, s]) for s in heads])
            oi.append([_dot(scj[h], vv[:, heads[h]]) for h in range(H)])
            qd.append(qdj)
            dec.append(jnp.exp(bl))
            r0s.append(r0)
            sqs.append(_chunk_seq(idx, j, nu, ncs))
        own_state = ncs == 1
        loaded = [[st_scr[sqs[j], h] for h in range(H)] for j in range(nu if own_state else 1)]
        outs, finals = [], []
        st = loaded[0]
        for j in range(nu):
            if own_state:
                st = loaded[j]
            outs.append([oi[j][h] + _dot_nt(qd[j][:, heads[h]], mxu(st[h])) for h in range(H)])
            st = [st[h] * dec[j][:, heads[h]] + kv[j][h] for h in range(H)]
            if own_state or j == nu - 1:
                finals.append((sqs[j], st))
        for j in range(nu):
            q_scr[pl.ds(r0s[j], c), :] = jnp.concatenate(outs[j], axis=1)
        for sq, st in finals:
            for h in range(H):
                st_scr[sq, h] = st[h]
        return carry

    _run_groups(group, rows // (nu * c))

    on = _head_rms_gate(q_scr[...], gn_ref[...], g_scr[...])
    o_ref[...] = x + _dot(on.astype(BF16), wo_ref[...])

    @pl.when(t == pl.num_programs(1) - 1)
    def _():
        for b in range(bb):
            for h in range(H):
                sout_ref[b, h] = st_scr[b, h].T


def _hgrn_call(x2d, s0, nw, win, lbnd, gn, wo, *, nseq, seqlen, bb, tt, row0=0, bcast_state=False):
    assert bb == 1 or tt == seqlen
    assert nseq % bb == 0 and seqlen % tt == 0 and row0 % (bb * tt) == 0
    c = 16 if tt % 16 == 0 else 8
    rows = bb * tt
    nu = min(4, rows // c)
    nt = seqlen // tt
    blk0 = row0 // rows
    s_idx = (lambda b, t: (0, 0, 0, 0)) if bcast_state else (lambda b, t: (b, 0, 0, 0))
    slab = pltpu.VMEM((rows, D), F32)
    return pl.pallas_call(
        functools.partial(_hgrn_kernel, bb=bb, tt=tt, c=c, nu=nu),
        grid=(nseq // bb, nt),
        in_specs=[
            pl.BlockSpec((rows, D), lambda b, t: (blk0 + b * nt + t, 0)),
            pl.BlockSpec((bb, H, HD, HD), s_idx),
            _const_spec((1, D)),
            _const_spec((D, 4 * D)),
            _const_spec(lbnd.shape),
            _const_spec((1, HD)),
            _const_spec((D, D)),
        ],
        out_specs=[
            pl.BlockSpec((rows, D), lambda b, t: (b * nt + t, 0)),
            pl.BlockSpec((bb, H, HD, HD), lambda b, t: (b, 0, 0, 0)),
        ],
        out_shape=[
            jax.ShapeDtypeStruct((nseq * seqlen, D), F32),
            jax.ShapeDtypeStruct((nseq, H, HD, HD), F32),
        ],
        scratch_shapes=[pltpu.VMEM((bb, H, HD, HD), F32), slab, slab, slab, slab, slab],
        compiler_params=pltpu.CompilerParams(
            dimension_semantics=("arbitrary", "arbitrary"), vmem_limit_bytes=V7X_VMEM_LIMIT),
        name=f"hgrn_mixer_{nseq}x{seqlen}",
    )(x2d, s0, nw, win, lbnd, gn, wo)


def _neumann_inv(ls, eye, levels, mxu):
    xs = [eye - l for l in ls]
    ps = [mxu(l) for l in ls]
    for _ in range(levels - 1):
        ps = [mxu(_dot(p, p)) for p in ps]
        xs = [x + _dot(mxu(x), p) for x, p in zip(xs, ps)]
    return xs


def _unit_lower_inv(ls, c, mxu):
    row = _iota2((c, c), 0)
    col = _iota2((c, c), 1)
    eye = (row == col).astype(F32)
    if c <= 16:
        return _neumann_inv(ls, eye, c.bit_length() - 1, mxu)
    sub = 16
    assert c == 4 * sub
    diag = (row // sub) == (col // sub)
    lds = [jnp.where(diag, l, 0.0) for l in ls]
    dinvs = [mxu(d) for d in _neumann_inv(lds, eye, 4, mxu)]
    ms = [_dot(d, mxu(l - ld)) for d, l, ld in zip(dinvs, ls, lds)]
    mbs = [mxu(m) for m in ms]
    m2s = [mxu(_dot(m, m)) for m in mbs]
    ys = [(eye - m) + _dot(mxu(eye - m), m2) for m, m2 in zip(ms, m2s)]
    return [_dot(mxu(y), d) for y, d in zip(ys, dinvs)]


def _gdn_kernel(x_ref, s0_ref, c0_ref, nw_ref, win_ref, cw_ref, par_ref, gn_ref, wo_ref,
                o_ref, sout_ref, cout_ref,
                s_scr, xp_scr, act_scr, z_scr, la_scr, be_scr, o_scr, *, bb, tt, c, nu):
    t = pl.program_id(1)
    rows = bb * tt
    ncs = tt // c
    assert ncs == 1 or ncs % nu == 0

    @pl.when(t == 0)
    def _():
        s_scr[...] = s0_ref[...]
        for b in range(bb):
            xp_scr[b, 8 - (CONV - 1):8, :] = c0_ref[b]

    x = x_ref[...]
    xn = _rms(x, nw_ref[...]).astype(BF16)
    pre = _dot(xn, win_ref[:, 0:QKV])
    for b in range(bb):
        xp_scr[b, 8:8 + tt, :] = pre[b * tt:(b + 1) * tt]
    z_scr[...] = _silu(_dot(xn, win_ref[:, QKV:QKV + D]))
    ba = _dot(xn, win_ref[:, QKV + D:QKV + D + LANES])
    par = par_ref[...]
    sp = ba + par[0:1]
    softplus = jnp.maximum(sp, 0.0) + jnp.log(1.0 + jnp.exp(-jnp.abs(sp)))
    la_scr[...] = -jnp.exp(par[1:2]) * softplus
    be_scr[...] = jax.nn.sigmoid(ba)

    cw = cw_ref[...]
    for b in range(bb):
        conv = xp_scr[b, 8 - (CONV - 1):8 - (CONV - 1) + tt, :] * cw[0:1]
        for w in range(1, CONV):
            conv = conv + xp_scr[b, 8 - (CONV - 1) + w:8 - (CONV - 1) + w + tt, :] * cw[w:w + 1]
        act = _silu(conv)
        for j in range(2 * H):
            a = act[:, j * HD:(j + 1) * HD]
            scale = HD ** -0.5 if j < H else 1.0
            act_scr[b * tt:(b + 1) * tt, j * HD:(j + 1) * HD] = (
                a * lax.rsqrt(jnp.sum(a * a, axis=-1, keepdims=True) + EPS) * scale)
        act_scr[b * tt:(b + 1) * tt, 2 * H * HD:] = act[:, 2 * H * HD:]
        tail = xp_scr[b, tt:tt + 8, :]
        xp_scr[b, 0:8, :] = tail

    row = _iota2((c, c), 0)
    col = _iota2((c, c), 1)
    incl = row >= col
    strict = row > col
    cumsum, mxu = _chunk_ops(incl, c)
    if c % 16 == 0:
        eye_l = (_iota2((LANES, LANES), 0) == _iota2((LANES, LANES), 1)).astype(BF16)
        transpose = lambda m: _sel_dot_nt(eye_l, m)
    else:
        eye_l = (_iota2((LANES, LANES), 0) == _iota2((LANES, LANES), 1)).astype(F32)
        transpose = lambda m: _dot_nt(eye_l, m, HI)

    heads = [slice(h * HD, (h + 1) * HD) for h in range(H)]

    def group(idx, carry):
        qd, ktail, dtot, lmat, amat, rwu, r0s, sqs = [], [], [], [], [], [], [], []
        for j in range(nu):
            r0 = _chunk_row(idx, j, nu, c)
            gc = cumsum(la_scr[pl.ds(r0, c), :])
            gt = transpose(gc)
            be = be_scr[pl.ds(r0, c), :]
            eg = jnp.exp(gc)
            gl = gc[c - 1:c, :]
            et = jnp.exp(gl - gc)
            dt = jnp.exp(gl)
            qq = act_scr[pl.ds(r0, c), 0:H * HD]
            kk = act_scr[pl.ds(r0, c), H * HD:2 * H * HD]
            vv = act_scr[pl.ds(r0, c), 2 * H * HD:]
            qd_j, ktail_j, dtot_j = [], [], []
            for h, s in enumerate(heads):
                g_col = gc[:, 8 + h:9 + h]
                eg_col = eg[:, 8 + h:9 + h]
                be_col = be[:, h:h + 1]
                kh = mxu(kk[:, s])
                qk = _dot_nt(jnp.concatenate([mxu(qq[:, s]), kh], axis=0), kh)
                ed = jnp.exp(g_col - gt[8 + h:9 + h, :])
                lmat.append(be_col * qk[c:] * jnp.where(strict, ed, 0.0))
                amat.append(mxu(qk[:c] * jnp.where(incl, ed, 0.0)))
                rwu.append(mxu(jnp.concatenate([kk[:, s] * (be_col * eg_col), vv[:, s] * be_col], axis=1)))
                qd_j.append(mxu(qq[:, s] * eg_col))
                ktail_j.append(mxu(kk[:, s] * et[:, 8 + h:9 + h]))
                dtot_j.append(dt[:, 8 + h:9 + h])
            qd.append(qd_j)
            ktail.append(ktail_j)
            dtot.append(dtot_j)
            r0s.append(r0)
            sqs.append(_chunk_seq(idx, j, nu, ncs))
        tinv = _unit_lower_inv(lmat, c, mxu)
        wu = [_dot(mxu(t_), r_) for t_, r_ in zip(tinv, rwu)]

        own_state = ncs == 1
        loaded = [[s_scr[sqs[j], h] for h in range(H)] for j in range(nu if own_state else 1)]
        outs, finals = [], []
        st = loaded[0]
        for j in range(nu):
            if own_state:
                st = loaded[j]
            o_j, st_next = [], []
            for h in range(H):
                wuh = wu[j * H + h]
                qw = _dot(jnp.concatenate([qd[j][h], mxu(wuh[:, :HD])], axis=0), mxu(st[h]))
                u = wuh[:, HD:] - qw[c:]
                ub = mxu(u)
                o_j.append(qw[:c] + _dot(amat[j * H + h], ub))
                st_next.append(st[h] * dtot[j][h] + _dot_tn(ktail[j][h], ub))
            st = st_next
            outs.append(o_j)
            if own_state or j == nu - 1:
                finals.append((sqs[j], st))
        for j in range(nu):
            o_scr[pl.ds(r0s[j], c), :] = jnp.concatenate(outs[j], axis=1)
        for sq, st in finals:
            for h in range(H):
                s_scr[sq, h] = st[h]
        return carry

    _run_groups(group, rows // (nu * c))

    on = _head_rms_gate(o_scr[...], gn_ref[...], z_scr[...])
    o_ref[...] = x + _dot(on.astype(BF16), wo_ref[...])

    @pl.when(t == pl.num_programs(1) - 1)
    def _():
        sout_ref[...] = s_scr[...]
        for b in range(bb):
            cout_ref[b] = xp_scr[b, 8 - (CONV - 1):8, :]


def _gdn_call(x2d, s0, c0, nw, win, cw, par, gn, wo, *, nseq, seqlen, bb, tt, c, nu=2, row0=0,
              bcast_state=False):
    assert bb == 1 or tt == seqlen
    assert nseq % bb == 0 and seqlen % tt == 0 and tt % c == 0 and tt % 8 == 0 and row0 % (bb * tt) == 0
    rows = bb * tt
    nt = seqlen // tt
    blk0 = row0 // rows
    s_idx = (lambda b, t: (0, 0, 0, 0)) if bcast_state else (lambda b, t: (b, 0, 0, 0))
    c_idx = (lambda b, t: (0, 0, 0)) if bcast_state else (lambda b, t: (b, 0, 0))
    slab = pltpu.VMEM((rows, D), F32)
    return pl.pallas_call(
        functools.partial(_gdn_kernel, bb=bb, tt=tt, c=c, nu=min(nu, bb * tt // c)),
        grid=(nseq // bb, nt),
        in_specs=[
            pl.BlockSpec((rows, D), lambda b, t: (blk0 + b * nt + t, 0)),
            pl.BlockSpec((bb, H, HD, HD), s_idx),
            pl.BlockSpec((bb, CONV - 1, QKV), c_idx),
            _const_spec((1, D)),
            _const_spec(win.shape),
            _const_spec((CONV, QKV)),
            _const_spec((2, LANES)),
            _const_spec((1, HD)),
            _const_spec((D, D)),
        ],
        out_specs=[
            pl.BlockSpec((rows, D), lambda b, t: (b * nt + t, 0)),
            pl.BlockSpec((bb, H, HD, HD), lambda b, t: (b, 0, 0, 0)),
            pl.BlockSpec((bb, CONV - 1, QKV), lambda b, t: (b, 0, 0)),
        ],
        out_shape=[
            jax.ShapeDtypeStruct((nseq * seqlen, D), F32),
            jax.ShapeDtypeStruct((nseq, H, HD, HD), F32),
            jax.ShapeDtypeStruct((nseq, CONV - 1, QKV), F32),
        ],
        scratch_shapes=[
            pltpu.VMEM((bb, H, HD, HD), F32),
            pltpu.VMEM((bb, tt + 8, QKV), F32),
            pltpu.VMEM((rows, QKV), F32),
            slab,
            pltpu.VMEM((rows, LANES), F32),
            pltpu.VMEM((rows, LANES), F32),
            slab,
        ],
        compiler_params=pltpu.CompilerParams(
            dimension_semantics=("arbitrary", "arbitrary"), vmem_limit_bytes=V7X_VMEM_LIMIT),
        name=f"gdn_mixer_{nseq}x{seqlen}",
    )(x2d, s0, c0, nw, win, cw, par, gn, wo)


def _swiglu(xb, w1_ref, w3_ref, w2_ref):
    acc = None
    for j in range(DFF // FB):
        fs = slice(j * FB, (j + 1) * FB)
        hh = (_silu(_dot(xb, w1_ref[:, fs])) * _dot(xb, w3_ref[:, fs])).astype(BF16)
        part = _dot(hh, w2_ref[fs, :])
        acc = part if acc is None else acc + part
    return acc


def _ffn_kernel(*refs, n_in, nblk_first):
    x_refs = refs[:n_in]
    nw_ref, w1_ref, w3_ref, w2_ref, o_ref = refs[n_in:]
    x = x_refs[0][...]
    if n_in == 2:
        x = jnp.where(pl.program_id(0) < nblk_first, x, x_refs[1][...])
    xb = _rms(x, nw_ref[...]).astype(BF16)
    o_ref[...] = x + _swiglu(xb, w1_ref, w3_ref, w2_ref)


def _two_group_specs(tm, nblk_first, nblk_total):
    last_first = nblk_first - 1
    return [
        pl.BlockSpec((tm, D), lambda i, *_: (jnp.minimum(i, last_first), 0)),
        pl.BlockSpec((tm, D), lambda i, *_: (jnp.maximum(i - nblk_first, 0), 0)),
    ]


def _ffn_call(xs, nw, w1, w3, w2, *, tm):
    nblks = [x.shape[0] // tm for x in xs]
    assert all(x.shape[0] % tm == 0 for x in xs)
    total = sum(nblks)
    if len(xs) == 2:
        x_specs = _two_group_specs(tm, nblks[0], total)
    else:
        x_specs = [pl.BlockSpec((tm, D), lambda i: (i, 0))]
    return pl.pallas_call(
        functools.partial(_ffn_kernel, n_in=len(xs), nblk_first=nblks[0]),
        grid=(total,),
        in_specs=x_specs + [_const_spec((1, D)), _const_spec((D, DFF)), _const_spec((D, DFF)), _const_spec((DFF, D))],
        out_specs=pl.BlockSpec((tm, D), lambda i: (i, 0)),
        out_shape=jax.ShapeDtypeStruct((total * tm, D), F32),
        compiler_params=pltpu.CompilerParams(dimension_semantics=("arbitrary",), vmem_limit_bytes=V7X_VMEM_LIMIT),
        name=f"ffn_dense_{total * tm}",
    )(*xs, nw, w1, w3, w2)


def _router_kernel(xa_ref, xb_ref, nw_ref, wr_ref, gate_ref, lpos_ref, lpost_ref, tab_ref, tot_ref, seg_scr, *,
                   tm, nblk_first):
    i = pl.program_id(0)

    @pl.when(i == 0)
    def _():
        seg_scr[...] = jnp.zeros_like(seg_scr)

    x = jnp.where(i < nblk_first, xa_ref[...], xb_ref[...])
    xn = _rms(x, nw_ref[...])
    logits = _dot(xn, wr_ref[...], HI)
    lane = _iota2((tm, LANES), 1).astype(F32)
    neg = jnp.float32(-jnp.inf)
    l1 = jnp.where(lane < NE, logits, neg)
    m1 = jnp.max(l1, axis=-1, keepdims=True)
    i1 = jnp.min(jnp.where(l1 == m1, lane, float(LANES)), axis=-1, keepdims=True)
    l2 = jnp.where(lane == i1, neg, l1)
    m2 = jnp.max(l2, axis=-1, keepdims=True)
    i2 = jnp.min(jnp.where(l2 == m2, lane, float(LANES)), axis=-1, keepdims=True)
    e2 = jnp.exp(m2 - m1)
    g1 = 1.0 / (1.0 + e2)
    g2 = e2 / (1.0 + e2)

    oh1 = (lane == i1).astype(F32)
    oh2 = (lane == i2).astype(F32)
    below = (_iota2((tm, tm), 0) > _iota2((tm, tm), 1)).astype(BF16)
    c1 = _dot(below, oh1.astype(BF16))
    c2 = _dot(below, oh2.astype(BF16))
    n1 = jnp.sum(oh1, axis=0, keepdims=True)
    n2 = jnp.sum(oh2, axis=0, keepdims=True)
    cnt = jnp.floor((n1 + n2 + (SEG - 1)) * (1.0 / SEG)) * SEG
    before = (_iota2((LANES, LANES), 0) < _iota2((LANES, LANES), 1)).astype(BF16)
    start = _dot(jnp.broadcast_to(cnt, (16, LANES)).astype(BF16), before)[0:1]
    p1 = jnp.sum(oh1 * (start + c1), axis=-1, keepdims=True)
    p2 = jnp.sum(oh2 * (start + n1 + c2), axis=-1, keepdims=True)
    used = seg_scr[...]
    seg_scr[...] = used + cnt
    tot_ref[...] = used + cnt

    gate_ref[:, 0:1] = g1
    gate_ref[:, 1:2] = g2
    lpos_ref[:, 0:1] = p1
    lpos_ref[:, 1:2] = p2
    sub = _iota2((8, LANES), 0)
    tab_ref[...] = jnp.where(sub == 0, cnt, jnp.where(sub == 1, start, jnp.where(sub == 2, used, 0.0)))
    slab = jnp.where(lane == 0.0, p1, jnp.where(lane == 1.0, p2, 0.0))
    eye16 = (_iota2((16, LANES), 0) == _iota2((16, LANES), 1)).astype(BF16)
    lpost_ref[...] = _sel_dot_nt(eye16, slab)[0:8]


def _router_call(xa, xb, nw, wr, *, tm):
    na, nb = xa.shape[0] // tm, xb.shape[0] // tm
    nblk = na + nb
    n = nblk * tm
    return pl.pallas_call(
        functools.partial(_router_kernel, tm=tm, nblk_first=na),
        grid=(nblk,),
        in_specs=_two_group_specs(tm, na, nblk) + [_const_spec((1, D)), _const_spec((D, LANES))],
        out_specs=[
            pl.BlockSpec((tm, 2), lambda i: (i, 0)),
            pl.BlockSpec((tm, 2), lambda i: (i, 0)),
            pl.BlockSpec((8, tm), lambda i: (i, 0)),
            pl.BlockSpec((8, LANES), lambda i: (i, 0)),
            pl.BlockSpec((1, LANES), lambda i: (0, 0)),
        ],
        out_shape=[
            jax.ShapeDtypeStruct((n, 2), F32),
            jax.ShapeDtypeStruct((n, 2), F32),
            jax.ShapeDtypeStruct((nblk * 8, tm), F32),
            jax.ShapeDtypeStruct((nblk * 8, LANES), F32),
            jax.ShapeDtypeStruct((1, LANES), F32),
        ],
        scratch_shapes=[pltpu.VMEM((1, LANES), F32)],
        compiler_params=pltpu.CompilerParams(dimension_semantics=("arbitrary",), vmem_limit_bytes=V7X_VMEM_LIMIT),
        name="moe_router",
    )(xa, xb, nw, wr)


def _local_rows(tm):
    return -(-(2 * tm + NE * (SEG - 1)) // LANES) * LANES


def _segment_copies(tab_ref, tile, local_ref, global_ref, sem, *, to_global, wait):
    for e in range(NE):
        base = (tile * NE + e) * 3
        npiece, lstart, gstart = tab_ref[base], tab_ref[base + 1], tab_ref[base + 2]

        def piece(k, carry, lstart=lstart, gstart=gstart):
            loc = local_ref.at[pl.ds(pl.multiple_of(lstart + k * SEG, SEG), SEG)]
            glo = global_ref.at[pl.ds(pl.multiple_of(gstart + k * SEG, SEG), SEG)]
            cp = pltpu.make_async_copy(loc, glo, sem) if to_global else pltpu.make_async_copy(glo, loc, sem)
            if wait:
                cp.wait()
            else:
                cp.start()
            return carry

        lax.fori_loop(0, npiece, piece, 0)


def _dispatch_kernel(tab_ref, zst_ref, xa_ref, xb_ref, lpost_ref, xs_ref, buf, zero_scr, sem, zsem, *,
                     tm, nblk_first, lr):
    i = pl.program_id(0)
    slot = lax.rem(i, 2)

    @pl.when(i == 0)
    def _():
        zero_scr[...] = jnp.zeros_like(zero_scr)

        def zero_copy(e):
            start = pl.multiple_of(zst_ref[e], TM)
            return pltpu.make_async_copy(zero_scr, xs_ref.at[pl.ds(start, TM)], zsem)

        for e in range(zst_ref.shape[0]):
            pl.when(zst_ref[e] >= 0)(lambda e=e: zero_copy(e).start())
        for e in range(zst_ref.shape[0]):
            pl.when(zst_ref[e] >= 0)(lambda e=e: zero_copy(e).wait())

    x = jnp.where(i < nblk_first, xa_ref[...], xb_ref[...])
    lpt = lpost_ref[...]
    rio = _iota2((lr, tm), 0).astype(F32)
    sel = jnp.where(rio == lpt[0:1, :], 1.0, jnp.where(rio == lpt[1:2, :], 1.0, 0.0)).astype(BF16)
    buf[slot] = _sel_dot(sel, x)

    copies = functools.partial(_segment_copies, tab_ref, global_ref=xs_ref, to_global=True)
    copies(i, local_ref=buf.at[slot], sem=sem.at[slot], wait=False)
    pl.when(i > 0)(lambda: copies(i - 1, local_ref=buf.at[1 - slot], sem=sem.at[1 - slot], wait=True))
    pl.when(i == pl.num_programs(0) - 1)(
        lambda: copies(i, local_ref=buf.at[slot], sem=sem.at[slot], wait=True))


def _dispatch_call(tab, zst, xa, xb, lpost, *, tm, rmax):
    na, nb = xa.shape[0] // tm, xb.shape[0] // tm
    lr = _local_rows(tm)
    return pl.pallas_call(
        functools.partial(_dispatch_kernel, tm=tm, nblk_first=na, lr=lr),
        grid_spec=pltpu.PrefetchScalarGridSpec(
            num_scalar_prefetch=2,
            grid=(na + nb,),
            in_specs=_two_group_specs(tm, na, na + nb) + [pl.BlockSpec((8, tm), lambda i, *_: (i, 0))],
            out_specs=pl.BlockSpec(memory_space=pl.ANY),
            scratch_shapes=[pltpu.VMEM((2, lr, D), F32), pltpu.VMEM((TM, D), F32),
                            pltpu.SemaphoreType.DMA((2,)), pltpu.SemaphoreType.DMA],
        ),
        out_shape=jax.ShapeDtypeStruct((rmax, D), F32),
        compiler_params=pltpu.CompilerParams(dimension_semantics=("arbitrary",), vmem_limit_bytes=V7X_VMEM_LIMIT),
        name="moe_dispatch",
    )(tab, zst, xa, xb, lpost)


def _expert_kernel(texp_ref, tsrc_ref, nact_ref, xs_ref, nw_ref, w1_ref, w3_ref, w2_ref, ys_ref):
    i = pl.program_id(0)

    @pl.when(i < nact_ref[0])
    def _():
        xb = _rms(xs_ref[...], nw_ref[...]).astype(BF16)
        ys_ref[...] = _swiglu(xb, w1_ref, w3_ref, w2_ref)

    @pl.when(i >= nact_ref[0])
    def _():
        ys_ref[...] = jnp.zeros_like(ys_ref)


def _expert_call(texp, tsrc, nact, xs, nw, w1, w3, w2):
    ntile = xs.shape[0] // TM
    return pl.pallas_call(
        _expert_kernel,
        grid_spec=pltpu.PrefetchScalarGridSpec(
            num_scalar_prefetch=3,
            grid=(ntile,),
            in_specs=[
                pl.BlockSpec((TM, D), lambda i, te, ts, na: (ts[i], 0)),
                pl.BlockSpec((1, D), lambda i, te, ts, na: (0, 0)),
                pl.BlockSpec((None, D, DFF), lambda i, te, ts, na: (te[i], 0, 0)),
                pl.BlockSpec((None, D, DFF), lambda i, te, ts, na: (te[i], 0, 0)),
                pl.BlockSpec((None, DFF, D), lambda i, te, ts, na: (te[i], 0, 0)),
            ],
            out_specs=pl.BlockSpec((TM, D), lambda i, te, ts, na: (i, 0)),
        ),
        out_shape=jax.ShapeDtypeStruct((xs.shape[0], D), F32),
        compiler_params=pltpu.CompilerParams(dimension_semantics=("arbitrary",), vmem_limit_bytes=V7X_VMEM_LIMIT),
        name="moe_experts",
    )(texp, tsrc, nact, xs, nw, w1, w3, w2)


def _combine_kernel(tab_ref, x_ref, gate_ref, lpos_ref, nf_ref, ys_ref, o_ref, buf, sem, *, tm, blk0, lr):
    i = pl.program_id(0)
    slot = lax.rem(i, 2)

    def fetch(step, slot_):
        tile = blk0 + step
        last = (tile * NE + NE - 1) * 3
        used = pl.multiple_of(tab_ref[last + 1] + tab_ref[last] * SEG, SEG)

        def clear(k, carry):
            buf[slot_, pl.ds(pl.multiple_of(used + k * SEG, SEG), SEG), :] = jnp.zeros((SEG, D), F32)
            return carry

        lax.fori_loop(0, (lr - used) // SEG, clear, 0)
        _segment_copies(tab_ref, tile, buf.at[slot_], ys_ref, sem.at[slot_], to_global=False, wait=False)

    pl.when(i == 0)(lambda: fetch(i, slot))
    pl.when(i + 1 < pl.num_programs(0))(lambda: fetch(i + 1, 1 - slot))
    _segment_copies(tab_ref, blk0 + i, buf.at[slot], ys_ref, sem.at[slot], to_global=False, wait=True)

    y = buf[slot]
    y_hi = y.astype(BF16)
    y_lo = (y - y_hi.astype(F32)).astype(BF16)
    lp = lpos_ref[...]
    cio = _iota2((tm, lr), 1).astype(F32)
    g = gate_ref[...]
    moe = None
    for s in range(2):
        sel = jnp.where(cio == lp[:, s:s + 1], 1.0, 0.0).astype(BF16)
        picked = _dot(sel, y_hi) + _dot(sel, y_lo)
        moe = g[:, s:s + 1] * picked if moe is None else moe + g[:, s:s + 1] * picked
    o_ref[...] = _rms(x_ref[...] + moe, nf_ref[...])


def _combine_call(tab, x, gate, lpos, nf, ys, *, tm, tok0):
    n = x.shape[0]
    blk0 = tok0 // tm
    lr = _local_rows(tm)
    return pl.pallas_call(
        functools.partial(_combine_kernel, tm=tm, blk0=blk0, lr=lr),
        grid_spec=pltpu.PrefetchScalarGridSpec(
            num_scalar_prefetch=1,
            grid=(n // tm,),
            in_specs=[
                pl.BlockSpec((tm, D), lambda i, *_: (i, 0)),
                pl.BlockSpec((tm, 2), lambda i, *_: (blk0 + i, 0)),
                pl.BlockSpec((tm, 2), lambda i, *_: (blk0 + i, 0)),
                pl.BlockSpec((1, D), lambda i, *_: (0, 0)),
                pl.BlockSpec(memory_space=pl.ANY),
            ],
            out_specs=pl.BlockSpec((tm, D), lambda i, *_: (i, 0)),
            scratch_shapes=[pltpu.VMEM((2, lr, D), F32), pltpu.SemaphoreType.DMA((2,))],
        ),
        out_shape=jax.ShapeDtypeStruct((n, D), F32),
        compiler_params=pltpu.CompilerParams(dimension_semantics=("arbitrary",), vmem_limit_bytes=V7X_VMEM_LIMIT),
        name=f"moe_combine_{n}",
    )(tab, x, gate, lpos, nf, ys)


def kernel(x_prompt, x_sample, state_hgrn, state_gdn, state_gdn_conv, meta_tokens, norm_mix, norm_ffn, norm_final,
           hg_w_in, hg_lower_bounds, hg_g_norm, hg_w_o, gdn_w_in, gdn_conv_w, gdn_a_log, gdn_dt_bias, gdn_g_norm,
           gdn_w_o, ffn_w1, ffn_w3, ffn_w2, moe_w_router, moe_w1, moe_w3, moe_w2):
    bp, tp, _ = x_prompt.shape
    bs, ts, _ = x_sample.shape
    np_, ns_ = bp * tp, bs * ts
    n_tok = np_ + ns_
    row = lambda v: v.reshape(1, -1).astype(F32)

    hg_win = hg_w_in.astype(BF16)
    hg_wo = hg_w_o.astype(BF16)
    gdn_win = jnp.concatenate(
        [gdn_w_in[:, :QKV + D], gdn_w_in[:, QKV + D:], jnp.zeros((D, LANES - 2 * H), gdn_w_in.dtype)],
        axis=1).astype(BF16)
    gdn_wo = gdn_w_o.astype(BF16)
    par = jnp.zeros((2, LANES), F32)
    par = par.at[0, H:2 * H].set(gdn_dt_bias.astype(F32)).at[1, H:2 * H].set(gdn_a_log.astype(F32))
    w1d, w3d, w2d = ffn_w1.astype(BF16), ffn_w3.astype(BF16), ffn_w2.astype(BF16)
    w1e, w3e, w2e = moe_w1.astype(BF16), moe_w3.astype(BF16), moe_w2.astype(BF16)
    wr = jnp.concatenate([moe_w_router.astype(F32), jnp.zeros((D, LANES - NE), F32)], axis=1)
    nm0, nm1 = row(norm_mix[0]), row(norm_mix[1])
    nf0, nf1 = row(norm_ffn[0]), row(norm_ffn[1])
    nfin = row(norm_final)
    hg_gn, gdn_gn = row(hg_g_norm), row(gdn_g_norm)
    lbnd = hg_lower_bounds.astype(F32)
    cw = gdn_conv_w.astype(F32)

    hg_args = (nm0, hg_win, lbnd, hg_gn, hg_wo)
    gdn_args = (nm1, gdn_win, cw, par, gdn_gn, gdn_wo)
    xm = meta_tokens.astype(F32)
    xp2 = x_prompt.reshape(np_, D)
    xs2 = x_sample.reshape(ns_, D)

    zero_state = jnp.zeros((1, H, HD, HD), F32)
    h1m, hg_m = _hgrn_call(xm, zero_state, *hg_args, nseq=1, seqlen=N_META, bb=1, tt=N_META)
    h1p, hg_p = _hgrn_call(xp2, hg_m, *hg_args, nseq=bp, seqlen=tp, bb=1, tt=256, bcast_state=True)
    h1s, hg_s = _hgrn_call(xs2, state_hgrn.astype(F32), *hg_args, nseq=bs, seqlen=ts, bb=8, tt=ts)
    h2m = _ffn_call([h1m], nf0, w1d, w3d, w2d, tm=N_META)
    h2 = _ffn_call([h1p, h1s], nf0, w1d, w3d, w2d, tm=256)

    zero_conv = jnp.zeros((1, CONV - 1, QKV), F32)
    _, gdn_m, conv_m = _gdn_call(h2m, zero_state, zero_conv, *gdn_args, nseq=1, seqlen=N_META, bb=1, tt=N_META, c=16)
    h3p, gdn_p, conv_p = _gdn_call(h2, gdn_m, conv_m, *gdn_args, nseq=bp, seqlen=tp, bb=1, tt=256, c=64, nu=4,
                                   bcast_state=True)
    h3s, gdn_s, conv_s = _gdn_call(h2, state_gdn.astype(F32), state_gdn_conv.astype(F32), *gdn_args,
                                   nseq=bs, seqlen=ts, bb=8, tt=ts, c=8, nu=4, row0=np_)

    yp, ysm = _moe_final(h3p, h3s, nf1, wr, w1e, w3e, w2e, nfin, tm=256)

    sd, gd, cd = state_hgrn.dtype, state_gdn.dtype, state_gdn_conv.dtype
    return (yp.reshape(bp, tp, D), ysm.reshape(bs, ts, D),
            hg_p.astype(sd), hg_s.astype(sd),
            gdn_p.astype(gd), gdn_s.astype(gd), conv_p.astype(cd), conv_s.astype(cd))


def _moe_final(h3p, h3s, nf1, wr, w1e, w3e, w2e, nfin, *, tm):
    np_ = h3p.shape[0]
    n_tok = np_ + h3s.shape[0]
    nblk = n_tok // tm
    gate, lpos, lpost, tab, tot = _router_call(h3p, h3s, nf1, wr, tm=tm)
    totals = tot[0, :NE].astype(jnp.int32)
    padded = ((totals + TM - 1) // TM) * TM
    ends = jnp.cumsum(padded)
    offs = ends - padded
    tab3 = tab.reshape(nblk, 8, LANES)[:, :3, :NE].astype(jnp.int32)
    seg_tab = jnp.stack([tab3[:, 0] // SEG, tab3[:, 1], tab3[:, 2] + offs[None, :]], axis=-1).reshape(-1)
    ntile = -(-(2 * n_tok + nblk * NE * (SEG - 1) + NE * (TM - 1)) // TM)
    nact = ends[-1] // TM
    tiles = jnp.arange(ntile, dtype=jnp.int32)
    texp_all = jnp.minimum(jnp.sum(tiles[:, None] * TM >= ends[None, :], axis=1), NE - 1).astype(jnp.int32)
    last = jnp.maximum(nact - 1, 0)
    active = tiles < nact
    texp = jnp.where(active, texp_all, texp_all[last]).astype(jnp.int32)
    tsrc = jnp.where(active, tiles, last).astype(jnp.int32)
    tail = nact + jnp.arange(ntile - 2 * n_tok // TM)
    zst = jnp.concatenate([jnp.where(padded > 0, ends - TM, -1),
                           jnp.where(tail < ntile, tail * TM, -1)]).astype(jnp.int32)
    xs_sorted = _dispatch_call(seg_tab, zst, h3p, h3s, lpost, tm=tm, rmax=ntile * TM)
    ys = _expert_call(texp, tsrc, nact.reshape(1).astype(jnp.int32), xs_sorted, nf1, w1e, w3e, w2e)
    yp = _combine_call(seg_tab, h3p, gate, lpos, nfin, ys, tm=tm, tok0=0)
    ysm = _combine_call(seg_tab, h3s, gate, lpos, nfin, ys, tm=tm, tok0=np_)
    return yp, ysm
```

```python
import functools

import jax
import jax.numpy as jnp
from jax import lax
from jax.experimental import pallas as pl
from jax.experimental.pallas import tpu as pltpu

D = 1024
H = 8
HD = 128
QKV = 3 * H * HD
CONV = 4
DFF = 3584
NE = 8
N_META = 16
EPS = 1e-6
LANES = 128
FB = 512
TM = 256
HG_SUB = 16
SEG = 8
V7X_VMEM_LIMIT = 56 * 1024 * 1024

F32 = jnp.float32
BF16 = jnp.bfloat16
HI = lax.Precision.HIGHEST


def _dot(a, b, precision=None):
    return jnp.dot(a, b, preferred_element_type=F32, precision=precision)


def _dot_nt(a, b, precision=None):
    return lax.dot_general(a, b, (((1,), (1,)), ((), ())), preferred_element_type=F32, precision=precision)


def _dot_tn(a, b, precision=None):
    return lax.dot_general(a, b, (((0,), (0,)), ((), ())), preferred_element_type=F32, precision=precision)


def _split3(x):
    x1 = x.astype(BF16)
    r1 = x - x1.astype(F32)
    x2 = r1.astype(BF16)
    x3 = (r1 - x2.astype(F32)).astype(BF16)
    return x1, x2, x3


def _sel_dot(sel, x):
    x1, x2, x3 = _split3(x)
    return _dot(sel, x1) + _dot(sel, x2) + _dot(sel, x3)


def _sel_dot_nt(sel, x):
    x1, x2, x3 = _split3(x)
    return _dot_nt(sel, x1) + _dot_nt(sel, x2) + _dot_nt(sel, x3)


def _rms(x, g):
    return x * lax.rsqrt(jnp.mean(x * x, axis=-1, keepdims=True) + EPS) * g


def _silu(x):
    return x * jax.nn.sigmoid(x)


def _iota2(shape, dim):
    return lax.broadcasted_iota(jnp.int32, shape, dim)


def _head_rms_gate(o, g, gate):
    parts = []
    for h in range(H):
        oh = o[:, h * HD:(h + 1) * HD]
        parts.append(_rms(oh, g))
    return jnp.concatenate(parts, axis=1) * gate


def _chunk_row(idx, j, nu, c):
    if isinstance(idx, int):
        return (idx * nu + j) * c
    return pl.multiple_of((idx * nu + j) * c, c)


def _chunk_seq(idx, j, nu, ncs):
    return (idx * nu + j) // ncs


def _chunk_ops(tri, c):
    if c % 16 == 0:
        tri_b = tri.astype(BF16)
        return (lambda x: _sel_dot(tri_b, x)), (lambda x: x.astype(BF16))
    tri_f = tri.astype(F32)
    return (lambda x: _dot(tri_f, x, HI)), (lambda x: x)


def _run_groups(group, n):
    if n == 1:
        group(0, 0)
    else:
        lax.fori_loop(0, n, group, 0)


def _const_spec(shape):
    nd = len(shape)
    return pl.BlockSpec(shape, lambda *_: (0,) * nd, pipeline_mode=pl.Buffered(1))


def _hgrn_kernel(x_ref, s0_ref, nw_ref, win_ref, lb_ref, gn_ref, wo_ref,
                 o_ref, sout_ref,
                 st_scr, q_scr, k_scr, lf_scr, v_scr, g_scr, *, bb, tt, c, nu):
    t = pl.program_id(1)
    rows = bb * tt
    ncs = tt // c
    assert ncs == 1 or ncs % nu == 0

    @pl.when(t == 0)
    def _():
        for b in range(bb):
            for h in range(H):
                st_scr[b, h] = s0_ref[b, h].T

    x = x_ref[...]
    xn = _rms(x, nw_ref[...]).astype(BF16)
    lbn = lb_ref[...]
    e = jnp.exp(lbn - jnp.max(lbn, axis=0, keepdims=True))
    lb = e[0:1] / jnp.sum(e, axis=0, keepdims=True)
    q_scr[...] = _silu(_dot(xn, win_ref[:, 0:D])) * HD ** -0.5
    f = lb + (1.0 - lb) * jax.nn.sigmoid(_dot(xn, win_ref[:, D:2 * D]))
    k_scr[...] = 1.0 - f
    lf_scr[...] = jnp.log(f)
    v_scr[...] = _dot(xn, win_ref[:, 2 * D:3 * D])
    g_scr[...] = _silu(_dot(xn, win_ref[:, 3 * D:4 * D]))

    row = _iota2((c, c), 0)
    col = _iota2((c, c), 1)
    tri = row >= col
    cumsum, mxu = _chunk_ops(tri, c)
    heads = [slice(h * HD, (h + 1) * HD) for h in range(H)]
    nsub = c // HG_SUB if c > HG_SUB else 1
    if nsub > 1:
        local_tri = (tri & ((row // HG_SUB) == (col // HG_SUB))).astype(BF16)
        slab_row = _iota2((c, D), 0)

    def chunk_terms(r0):
        lf = lf_scr[pl.ds(r0, c), :]
        bc = cumsum(lf)
        tot = bc[c - 1:c, :]
        qq = q_scr[pl.ds(r0, c), :]
        kk = k_scr[pl.ds(r0, c), :]
        vv = mxu(v_scr[pl.ds(r0, c), :])
        qdj = mxu(qq * jnp.exp(bc))
        ktail = mxu(kk * jnp.exp(tot - bc))
        if nsub == 1:
            kinv = mxu(kk * jnp.exp(-bc))
            scj = [_dot_nt(qdj[:, s], kinv[:, s]) for s in heads]
        else:
            bl = _sel_dot(local_tri, lf)
            qloc = mxu(qq * jnp.exp(bl))
            parts = [[] for _ in heads]
            for i in range(nsub):
                lo, hi = i * HG_SUB, (i + 1) * HG_SUB
                ref_pt = bc[lo - 1:lo, :] if i else 0.0
                kci = mxu(jnp.where(slab_row < hi, kk * jnp.exp(ref_pt - bc), 0.0))
                for h, s in enumerate(heads):
                    parts[h].append(_dot_nt(qloc[lo:hi, s], kci[:, s]))
            scj = [jnp.concatenate(p, axis=0) for p in parts]
        scj = [mxu(jnp.where(tri, sc, 0.0)) for sc in scj]
        kvj = [_dot_tn(vv[:, s], ktail[:, s]) for s in heads]
        oij = [_dot(scj[h], vv[:, heads[h]]) for h in range(H)]
        return qdj, jnp.exp(tot), kvj, oij

    def group(idx, carry):
        qd, dec, kv, oi, r0s, sqs = [], [], [], [], [], []
        for j in range(nu):
            r0 = _chunk_row(idx, j, nu, c)
            qdj, decj, kvj, oij = chunk_terms(r0)
            kv.append(kvj)
            oi.append(oij)
            qd.append(qdj)
            dec.append(decj)
            r0s.append(r0)
            sqs.append(_chunk_seq(idx, j, nu, ncs))
        own_state = ncs == 1
        loaded = [[st_scr[sqs[j], h] for h in range(H)] for j in range(nu if own_state else 1)]
        outs, finals = [], []
        st = loaded[0]
        for j in range(nu):
            if own_state:
                st = loaded[j]
            outs.append([oi[j][h] + _dot_nt(qd[j][:, heads[h]], mxu(st[h])) for h in range(H)])
            st = [st[h] * dec[j][:, heads[h]] + kv[j][h] for h in range(H)]
            if own_state or j == nu - 1:
                finals.append((sqs[j], st))
        for j in range(nu):
            q_scr[pl.ds(r0s[j], c), :] = jnp.concatenate(outs[j], axis=1)
        for sq, st in finals:
            for h in range(H):
                st_scr[sq, h] = st[h]
        return carry

    _run_groups(group, rows // (nu * c))

    on = _head_rms_gate(q_scr[...], gn_ref[...], g_scr[...])
    o_ref[...] = x + _dot(on.astype(BF16), wo_ref[...])

    @pl.when(t == pl.num_programs(1) - 1)
    def _():
        for b in range(bb):
            for h in range(H):
                sout_ref[b, h] = st_scr[b, h].T


def _hgrn_call(x2d, s0, nw, win, lbnd, gn, wo, *, nseq, seqlen, bb, tt, nu=4, row0=0, bcast_state=False):
    assert bb == 1 or tt == seqlen
    assert nseq % bb == 0 and seqlen % tt == 0 and row0 % (bb * tt) == 0
    c = next(cc for cc in (4 * HG_SUB, HG_SUB, 8) if tt % cc == 0)
    rows = bb * tt
    nu = min(nu, rows // c)
    nt = seqlen // tt
    blk0 = row0 // rows
    s_idx = (lambda b, t: (0, 0, 0, 0)) if bcast_state else (lambda b, t: (b, 0, 0, 0))
    slab = pltpu.VMEM((rows, D), F32)
    return pl.pallas_call(
        functools.partial(_hgrn_kernel, bb=bb, tt=tt, c=c, nu=nu),
        grid=(nseq // bb, nt),
        in_specs=[
            pl.BlockSpec((rows, D), lambda b, t: (blk0 + b * nt + t, 0)),
            pl.BlockSpec((bb, H, HD, HD), s_idx),
            _const_spec((1, D)),
            _const_spec((D, 4 * D)),
            _const_spec(lbnd.shape),
            _const_spec((1, HD)),
            _const_spec((D, D)),
        ],
        out_specs=[
            pl.BlockSpec((rows, D), lambda b, t: (b * nt + t, 0)),
            pl.BlockSpec((bb, H, HD, HD), lambda b, t: (b, 0, 0, 0)),
        ],
        out_shape=[
            jax.ShapeDtypeStruct((nseq * seqlen, D), F32),
            jax.ShapeDtypeStruct((nseq, H, HD, HD), F32),
        ],
        scratch_shapes=[pltpu.VMEM((bb, H, HD, HD), F32), slab, slab, slab, slab, slab],
        compiler_params=pltpu.CompilerParams(
            dimension_semantics=("arbitrary", "arbitrary"), vmem_limit_bytes=V7X_VMEM_LIMIT),
        name=f"hgrn_mixer_{nseq}x{seqlen}",
    )(x2d, s0, nw, win, lbnd, gn, wo)


def _neumann_inv(ls, eye, levels, mxu):
    xs = [eye - l for l in ls]
    ps = [mxu(l) for l in ls]
    for _ in range(levels - 1):
        ps = [mxu(_dot(p, p)) for p in ps]
        xs = [x + _dot(mxu(x), p) for x, p in zip(xs, ps)]
    return xs


def _unit_lower_inv(ls, c, mxu):
    row = _iota2((c, c), 0)
    col = _iota2((c, c), 1)
    eye = (row == col).astype(F32)
    if c <= 16:
        return _neumann_inv(ls, eye, c.bit_length() - 1, mxu)
    sub = 16
    nblk = c // sub
    assert c == nblk * sub and nblk & (nblk - 1) == 0
    diag = (row // sub) == (col // sub)
    lds = [jnp.where(diag, l, 0.0) for l in ls]
    dinvs = [mxu(d) for d in _neumann_inv(lds, eye, 4, mxu)]
    ms = [_dot(d, mxu(l - ld)) for d, l, ld in zip(dinvs, ls, lds)]
    ys = _neumann_inv(ms, eye, nblk.bit_length() - 1, mxu)
    return [_dot(mxu(y), d) for y, d in zip(ys, dinvs)]


def _gdn_kernel(x_ref, s0_ref, c0_ref, nw_ref, win_ref, cw_ref, par_ref, gn_ref, wo_ref,
                o_ref, sout_ref, cout_ref,
                s_scr, xp_scr, act_scr, z_scr, la_scr, be_scr, o_scr, *, bb, tt, c, nu):
    t = pl.program_id(1)
    rows = bb * tt
    ncs = tt // c
    assert ncs == 1 or ncs % nu == 0

    @pl.when(t == 0)
    def _():
        s_scr[...] = s0_ref[...]
        for b in range(bb):
            xp_scr[b, 8 - (CONV - 1):8, :] = c0_ref[b]

    x = x_ref[...]
    xn = _rms(x, nw_ref[...]).astype(BF16)
    pre = _dot(xn, win_ref[:, 0:QKV])
    for b in range(bb):
        xp_scr[b, 8:8 + tt, :] = pre[b * tt:(b + 1) * tt]
    z_scr[...] = _silu(_dot(xn, win_ref[:, QKV:QKV + D]))
    ba = _dot(xn, win_ref[:, QKV + D:QKV + D + LANES])
    par = par_ref[...]
    sp = ba + par[0:1]
    softplus = jnp.maximum(sp, 0.0) + jnp.log(1.0 + jnp.exp(-jnp.abs(sp)))
    la_scr[...] = -jnp.exp(par[1:2]) * softplus
    be_scr[...] = jax.nn.sigmoid(ba)

    cw = cw_ref[...]
    for b in range(bb):
        conv = xp_scr[b, 8 - (CONV - 1):8 - (CONV - 1) + tt, :] * cw[0:1]
        for w in range(1, CONV):
            conv = conv + xp_scr[b, 8 - (CONV - 1) + w:8 - (CONV - 1) + w + tt, :] * cw[w:w + 1]
        act = _silu(conv)
        for j in range(2 * H):
            a = act[:, j * HD:(j + 1) * HD]
            scale = HD ** -0.5 if j < H else 1.0
            act_scr[b * tt:(b + 1) * tt, j * HD:(j + 1) * HD] = (
                a * lax.rsqrt(jnp.sum(a * a, axis=-1, keepdims=True) + EPS) * scale)
        act_scr[b * tt:(b + 1) * tt, 2 * H * HD:] = act[:, 2 * H * HD:]
        tail = xp_scr[b, tt:tt + 8, :]
        xp_scr[b, 0:8, :] = tail

    row = _iota2((c, c), 0)
    col = _iota2((c, c), 1)
    incl = row >= col
    strict = row > col
    cumsum, mxu = _chunk_ops(incl, c)
    if c % 16 == 0:
        eye_l = (_iota2((LANES, LANES), 0) == _iota2((LANES, LANES), 1)).astype(BF16)
        transpose = lambda m: _sel_dot_nt(eye_l, m)
    else:
        eye_l = (_iota2((LANES, LANES), 0) == _iota2((LANES, LANES), 1)).astype(F32)
        transpose = lambda m: _dot_nt(eye_l, m, HI)

    heads = [slice(h * HD, (h + 1) * HD) for h in range(H)]

    def group(idx, carry):
        qd, ktail, dtot, lmat, amat, rwu, r0s, sqs = [], [], [], [], [], [], [], []
        for j in range(nu):
            r0 = _chunk_row(idx, j, nu, c)
            gc = cumsum(la_scr[pl.ds(r0, c), :])
            gt = transpose(gc)
            be = be_scr[pl.ds(r0, c), :]
            eg = jnp.exp(gc)
            gl = gc[c - 1:c, :]
            et = jnp.exp(gl - gc)
            dt = jnp.exp(gl)
            qq = act_scr[pl.ds(r0, c), 0:H * HD]
            kk = act_scr[pl.ds(r0, c), H * HD:2 * H * HD]
            vv = act_scr[pl.ds(r0, c), 2 * H * HD:]
            qd_j, ktail_j, dtot_j = [], [], []
            for h, s in enumerate(heads):
                g_col = gc[:, 8 + h:9 + h]
                eg_col = eg[:, 8 + h:9 + h]
                be_col = be[:, h:h + 1]
                kh = mxu(kk[:, s])
                qk = _dot_nt(jnp.concatenate([mxu(qq[:, s]), kh], axis=0), kh)
                ed = jnp.exp(g_col - gt[8 + h:9 + h, :])
                lmat.append(be_col * qk[c:] * jnp.where(strict, ed, 0.0))
                amat.append(mxu(qk[:c] * jnp.where(incl, ed, 0.0)))
                rwu.append(mxu(jnp.concatenate([kk[:, s] * (be_col * eg_col), vv[:, s] * be_col], axis=1)))
                qd_j.append(mxu(qq[:, s] * eg_col))
                ktail_j.append(mxu(kk[:, s] * et[:, 8 + h:9 + h]))
                dtot_j.append(dt[:, 8 + h:9 + h])
            qd.append(qd_j)
            ktail.append(ktail_j)
            dtot.append(dtot_j)
            r0s.append(r0)
            sqs.append(_chunk_seq(idx, j, nu, ncs))
        tinv = _unit_lower_inv(lmat, c, mxu)
        wu = [_dot(mxu(t_), r_) for t_, r_ in zip(tinv, rwu)]

        own_state = ncs == 1
        loaded = [[s_scr[sqs[j], h] for h in range(H)] for j in range(nu if own_state else 1)]
        outs, finals = [], []
        st = loaded[0]
        for j in range(nu):
            if own_state:
                st = loaded[j]
            o_j, st_next = [], []
            for h in range(H):
                wuh = wu[j * H + h]
                qw = _dot(jnp.concatenate([qd[j][h], mxu(wuh[:, :HD])], axis=0), mxu(st[h]))
                u = wuh[:, HD:] - qw[c:]
                ub = mxu(u)
                o_j.append(qw[:c] + _dot(amat[j * H + h], ub))
                st_next.append(st[h] * dtot[j][h] + _dot_tn(ktail[j][h], ub))
            st = st_next
            outs.append(o_j)
            if own_state or j == nu - 1:
                finals.append((sqs[j], st))
        for j in range(nu):
            o_scr[pl.ds(r0s[j], c), :] = jnp.concatenate(outs[j], axis=1)
        for sq, st in finals:
            for h in range(H):
                s_scr[sq, h] = st[h]
        return carry

    _run_groups(group, rows // (nu * c))

    on = _head_rms_gate(o_scr[...], gn_ref[...], z_scr[...])
    o_ref[...] = x + _dot(on.astype(BF16), wo_ref[...])

    @pl.when(t == pl.num_programs(1) - 1)
    def _():
        sout_ref[...] = s_scr[...]
        for b in range(bb):
            cout_ref[b] = xp_scr[b, 8 - (CONV - 1):8, :]


def _gdn_call(x2d, s0, c0, nw, win, cw, par, gn, wo, *, nseq, seqlen, bb, tt, c, nu=2, row0=0,
              bcast_state=False):
    assert bb == 1 or tt == seqlen
    assert nseq % bb == 0 and seqlen % tt == 0 and tt % c == 0 and tt % 8 == 0 and row0 % (bb * tt) == 0
    rows = bb * tt
    nt = seqlen // tt
    blk0 = row0 // rows
    s_idx = (lambda b, t: (0, 0, 0, 0)) if bcast_state else (lambda b, t: (b, 0, 0, 0))
    c_idx = (lambda b, t: (0, 0, 0)) if bcast_state else (lambda b, t: (b, 0, 0))
    slab = pltpu.VMEM((rows, D), F32)
    return pl.pallas_call(
        functools.partial(_gdn_kernel, bb=bb, tt=tt, c=c, nu=min(nu, bb * tt // c)),
        grid=(nseq // bb, nt),
        in_specs=[
            pl.BlockSpec((rows, D), lambda b, t: (blk0 + b * nt + t, 0)),
            pl.BlockSpec((bb, H, HD, HD), s_idx),
            pl.BlockSpec((bb, CONV - 1, QKV), c_idx),
            _const_spec((1, D)),
            _const_spec(win.shape),
            _const_spec((CONV, QKV)),
            _const_spec((2, LANES)),
            _const_spec((1, HD)),
            _const_spec((D, D)),
        ],
        out_specs=[
            pl.BlockSpec((rows, D), lambda b, t: (b * nt + t, 0)),
            pl.BlockSpec((bb, H, HD, HD), lambda b, t: (b, 0, 0, 0)),
            pl.BlockSpec((bb, CONV - 1, QKV), lambda b, t: (b, 0, 0)),
        ],
        out_shape=[
            jax.ShapeDtypeStruct((nseq * seqlen, D), F32),
            jax.ShapeDtypeStruct((nseq, H, HD, HD), F32),
            jax.ShapeDtypeStruct((nseq, CONV - 1, QKV), F32),
        ],
        scratch_shapes=[
            pltpu.VMEM((bb, H, HD, HD), F32),
            pltpu.VMEM((bb, tt + 8, QKV), F32),
            pltpu.VMEM((rows, QKV), F32),
            slab,
            pltpu.VMEM((rows, LANES), F32),
            pltpu.VMEM((rows, LANES), F32),
            slab,
        ],
        compiler_params=pltpu.CompilerParams(
            dimension_semantics=("arbitrary", "arbitrary"), vmem_limit_bytes=V7X_VMEM_LIMIT),
        name=f"gdn_mixer_{nseq}x{seqlen}",
    )(x2d, s0, c0, nw, win, cw, par, gn, wo)


def _swiglu(xb, w1_ref, w3_ref, w2_ref):
    acc = None
    for j in range(DFF // FB):
        fs = slice(j * FB, (j + 1) * FB)
        hh = (_silu(_dot(xb, w1_ref[:, fs])) * _dot(xb, w3_ref[:, fs])).astype(BF16)
        part = _dot(hh, w2_ref[fs, :])
        acc = part if acc is None else acc + part
    return acc


def _ffn_kernel(*refs, n_in, nblk_first):
    x_refs = refs[:n_in]
    nw_ref, w1_ref, w3_ref, w2_ref, o_ref = refs[n_in:]
    x = x_refs[0][...]
    if n_in == 2:
        x = jnp.where(pl.program_id(0) < nblk_first, x, x_refs[1][...])
    xb = _rms(x, nw_ref[...]).astype(BF16)
    o_ref[...] = x + _swiglu(xb, w1_ref, w3_ref, w2_ref)


def _two_group_specs(tm, nblk_first, nblk_total):
    last_first = nblk_first - 1
    return [
        pl.BlockSpec((tm, D), lambda i, *_: (jnp.minimum(i, last_first), 0)),
        pl.BlockSpec((tm, D), lambda i, *_: (jnp.maximum(i - nblk_first, 0), 0)),
    ]


def _ffn_call(xs, nw, w1, w3, w2, *, tm):
    nblks = [x.shape[0] // tm for x in xs]
    assert all(x.shape[0] % tm == 0 for x in xs)
    total = sum(nblks)
    if len(xs) == 2:
        x_specs = _two_group_specs(tm, nblks[0], total)
    else:
        x_specs = [pl.BlockSpec((tm, D), lambda i: (i, 0))]
    return pl.pallas_call(
        functools.partial(_ffn_kernel, n_in=len(xs), nblk_first=nblks[0]),
        grid=(total,),
        in_specs=x_specs + [_const_spec((1, D)), _const_spec((D, DFF)), _const_spec((D, DFF)), _const_spec((DFF, D))],
        out_specs=pl.BlockSpec((tm, D), lambda i: (i, 0)),
        out_shape=jax.ShapeDtypeStruct((total * tm, D), F32),
        compiler_params=pltpu.CompilerParams(dimension_semantics=("arbitrary",), vmem_limit_bytes=V7X_VMEM_LIMIT),
        name=f"ffn_dense_{total * tm}",
    )(*xs, nw, w1, w3, w2)


def _router_kernel(xa_ref, xb_ref, nw_ref, wr_ref, gate_ref, lpos_ref, lpost_ref, tab_ref, tot_ref, seg_scr, *,
                   tm, nblk_first):
    i = pl.program_id(0)

    @pl.when(i == 0)
    def _():
        seg_scr[...] = jnp.zeros_like(seg_scr)

    x = jnp.where(i < nblk_first, xa_ref[...], xb_ref[...])
    xn = _rms(x, nw_ref[...])
    logits = _dot(xn, wr_ref[...], HI)
    lane = _iota2((tm, LANES), 1).astype(F32)
    neg = jnp.float32(-jnp.inf)
    l1 = jnp.where(lane < NE, logits, neg)
    m1 = jnp.max(l1, axis=-1, keepdims=True)
    i1 = jnp.min(jnp.where(l1 == m1, lane, float(LANES)), axis=-1, keepdims=True)
    l2 = jnp.where(lane == i1, neg, l1)
    m2 = jnp.max(l2, axis=-1, keepdims=True)
    i2 = jnp.min(jnp.where(l2 == m2, lane, float(LANES)), axis=-1, keepdims=True)
    e2 = jnp.exp(m2 - m1)
    g1 = 1.0 / (1.0 + e2)
    g2 = e2 / (1.0 + e2)

    oh1 = (lane == i1).astype(F32)
    oh2 = (lane == i2).astype(F32)
    below = (_iota2((tm, tm), 0) > _iota2((tm, tm), 1)).astype(BF16)
    c1 = _dot(below, oh1.astype(BF16))
    c2 = _dot(below, oh2.astype(BF16))
    n1 = jnp.sum(oh1, axis=0, keepdims=True)
    n2 = jnp.sum(oh2, axis=0, keepdims=True)
    cnt = jnp.floor((n1 + n2 + (SEG - 1)) * (1.0 / SEG)) * SEG
    before = (_iota2((LANES, LANES), 0) < _iota2((LANES, LANES), 1)).astype(BF16)
    start = _dot(jnp.broadcast_to(cnt, (16, LANES)).astype(BF16), before)[0:1]
    p1 = jnp.sum(oh1 * (start + c1), axis=-1, keepdims=True)
    p2 = jnp.sum(oh2 * (start + n1 + c2), axis=-1, keepdims=True)
    used = seg_scr[...]
    seg_scr[...] = used + cnt
    tot_ref[...] = used + cnt

    gate_ref[:, 0:1] = g1
    gate_ref[:, 1:2] = g2
    lpos_ref[:, 0:1] = p1
    lpos_ref[:, 1:2] = p2
    sub = _iota2((8, LANES), 0)
    tab_ref[...] = jnp.where(sub == 0, cnt, jnp.where(sub == 1, start, jnp.where(sub == 2, used, 0.0)))
    slab = jnp.where(lane == 0.0, p1, jnp.where(lane == 1.0, p2, 0.0))
    eye16 = (_iota2((16, LANES), 0) == _iota2((16, LANES), 1)).astype(BF16)
    lpost_ref[...] = _sel_dot_nt(eye16, slab)[0:8]


def _router_call(xa, xb, nw, wr, *, tm):
    na, nb = xa.shape[0] // tm, xb.shape[0] // tm
    nblk = na + nb
    n = nblk * tm
    return pl.pallas_call(
        functools.partial(_router_kernel, tm=tm, nblk_first=na),
        grid=(nblk,),
        in_specs=_two_group_specs(tm, na, nblk) + [_const_spec((1, D)), _const_spec((D, LANES))],
        out_specs=[
            pl.BlockSpec((tm, 2), lambda i: (i, 0)),
            pl.BlockSpec((tm, 2), lambda i: (i, 0)),
            pl.BlockSpec((8, tm), lambda i: (i, 0)),
            pl.BlockSpec((8, LANES), lambda i: (i, 0)),
            pl.BlockSpec((1, LANES), lambda i: (0, 0)),
        ],
        out_shape=[
            jax.ShapeDtypeStruct((n, 2), F32),
            jax.ShapeDtypeStruct((n, 2), F32),
            jax.ShapeDtypeStruct((nblk * 8, tm), F32),
            jax.ShapeDtypeStruct((nblk * 8, LANES), F32),
            jax.ShapeDtypeStruct((1, LANES), F32),
        ],
        scratch_shapes=[pltpu.VMEM((1, LANES), F32)],
        compiler_params=pltpu.CompilerParams(dimension_semantics=("arbitrary",), vmem_limit_bytes=V7X_VMEM_LIMIT),
        name="moe_router",
    )(xa, xb, nw, wr)


def _local_rows(tm):
    return -(-(2 * tm + NE * (SEG - 1)) // LANES) * LANES


def _segment_copies(tab_ref, tile, local_ref, global_ref, sem, *, to_global, wait):
    for e in range(NE):
        base = (tile * NE + e) * 3
        npiece, lstart, gstart = tab_ref[base], tab_ref[base + 1], tab_ref[base + 2]

        def piece(k, carry, lstart=lstart, gstart=gstart):
            loc = local_ref.at[pl.ds(pl.multiple_of(lstart + k * SEG, SEG), SEG)]
            glo = global_ref.at[pl.ds(pl.multiple_of(gstart + k * SEG, SEG), SEG)]
            cp = pltpu.make_async_copy(loc, glo, sem) if to_global else pltpu.make_async_copy(glo, loc, sem)
            if wait:
                cp.wait()
            else:
                cp.start()
            return carry

        lax.fori_loop(0, npiece, piece, 0)


def _dispatch_kernel(tab_ref, zst_ref, xa_ref, xb_ref, lpost_ref, xs_ref, buf, zero_scr, sem, zsem, *,
                     tm, nblk_first, lr):
    i = pl.program_id(0)
    slot = lax.rem(i, 2)

    @pl.when(i == 0)
    def _():
        zero_scr[...] = jnp.zeros_like(zero_scr)

        def zero_copy(e):
            start = pl.multiple_of(zst_ref[e], TM)
            return pltpu.make_async_copy(zero_scr, xs_ref.at[pl.ds(start, TM)], zsem)

        for e in range(zst_ref.shape[0]):
            pl.when(zst_ref[e] >= 0)(lambda e=e: zero_copy(e).start())
        for e in range(zst_ref.shape[0]):
            pl.when(zst_ref[e] >= 0)(lambda e=e: zero_copy(e).wait())

    x = jnp.where(i < nblk_first, xa_ref[...], xb_ref[...])
    lpt = lpost_ref[...]
    rio = _iota2((lr, tm), 0).astype(F32)
    sel = jnp.where(rio == lpt[0:1, :], 1.0, jnp.where(rio == lpt[1:2, :], 1.0, 0.0)).astype(BF16)
    buf[slot] = _sel_dot(sel, x)

    copies = functools.partial(_segment_copies, tab_ref, global_ref=xs_ref, to_global=True)
    copies(i, local_ref=buf.at[slot], sem=sem.at[slot], wait=False)
    pl.when(i > 0)(lambda: copies(i - 1, local_ref=buf.at[1 - slot], sem=sem.at[1 - slot], wait=True))
    pl.when(i == pl.num_programs(0) - 1)(
        lambda: copies(i, local_ref=buf.at[slot], sem=sem.at[slot], wait=True))


def _dispatch_call(tab, zst, xa, xb, lpost, *, tm, rmax):
    na, nb = xa.shape[0] // tm, xb.shape[0] // tm
    lr = _local_rows(tm)
    return pl.pallas_call(
        functools.partial(_dispatch_kernel, tm=tm, nblk_first=na, lr=lr),
        grid_spec=pltpu.PrefetchScalarGridSpec(
            num_scalar_prefetch=2,
            grid=(na + nb,),
            in_specs=_two_group_specs(tm, na, na + nb) + [pl.BlockSpec((8, tm), lambda i, *_: (i, 0))],
            out_specs=pl.BlockSpec(memory_space=pl.ANY),
            scratch_shapes=[pltpu.VMEM((2, lr, D), F32), pltpu.VMEM((TM, D), F32),
                            pltpu.SemaphoreType.DMA((2,)), pltpu.SemaphoreType.DMA],
        ),
        out_shape=jax.ShapeDtypeStruct((rmax, D), F32),
        compiler_params=pltpu.CompilerParams(dimension_semantics=("arbitrary",), vmem_limit_bytes=V7X_VMEM_LIMIT),
        name="moe_dispatch",
    )(tab, zst, xa, xb, lpost)


def _expert_kernel(texp_ref, tsrc_ref, nact_ref, xs_ref, nw_ref, w1_ref, w3_ref, w2_ref, ys_ref):
    i = pl.program_id(0)

    @pl.when(i < nact_ref[0])
    def _():
        xb = _rms(xs_ref[...], nw_ref[...]).astype(BF16)
        ys_ref[...] = _swiglu(xb, w1_ref, w3_ref, w2_ref)

    @pl.when(i >= nact_ref[0])
    def _():
        ys_ref[...] = jnp.zeros_like(ys_ref)


def _expert_call(texp, tsrc, nact, xs, nw, w1, w3, w2):
    ntile = xs.shape[0] // TM
    return pl.pallas_call(
        _expert_kernel,
        grid_spec=pltpu.PrefetchScalarGridSpec(
            num_scalar_prefetch=3,
            grid=(ntile,),
            in_specs=[
                pl.BlockSpec((TM, D), lambda i, te, ts, na: (ts[i], 0)),
                pl.BlockSpec((1, D), lambda i, te, ts, na: (0, 0)),
                pl.BlockSpec((None, D, DFF), lambda i, te, ts, na: (te[i], 0, 0)),
                pl.BlockSpec((None, D, DFF), lambda i, te, ts, na: (te[i], 0, 0)),
                pl.BlockSpec((None, DFF, D), lambda i, te, ts, na: (te[i], 0, 0)),
            ],
            out_specs=pl.BlockSpec((TM, D), lambda i, te, ts, na: (i, 0)),
        ),
        out_shape=jax.ShapeDtypeStruct((xs.shape[0], D), F32),
        compiler_params=pltpu.CompilerParams(dimension_semantics=("arbitrary",), vmem_limit_bytes=V7X_VMEM_LIMIT),
        name="moe_experts",
    )(texp, tsrc, nact, xs, nw, w1, w3, w2)


def _combine_kernel(tab_ref, x_ref, gate_ref, lpos_ref, nf_ref, ys_ref, o_ref, buf, sem, *, tm, blk0, lr):
    i = pl.program_id(0)
    slot = lax.rem(i, 2)

    def fetch(step, slot_):
        tile = blk0 + step
        last = (tile * NE + NE - 1) * 3
        used = pl.multiple_of(tab_ref[last + 1] + tab_ref[last] * SEG, SEG)

        def clear(k, carry):
            buf[slot_, pl.ds(pl.multiple_of(used + k * SEG, SEG), SEG), :] = jnp.zeros((SEG, D), F32)
            return carry

        lax.fori_loop(0, (lr - used) // SEG, clear, 0)
        _segment_copies(tab_ref, tile, buf.at[slot_], ys_ref, sem.at[slot_], to_global=False, wait=False)

    pl.when(i == 0)(lambda: fetch(i, slot))
    pl.when(i + 1 < pl.num_programs(0))(lambda: fetch(i + 1, 1 - slot))
    _segment_copies(tab_ref, blk0 + i, buf.at[slot], ys_ref, sem.at[slot], to_global=False, wait=True)

    y = buf[slot]
    y_hi = y.astype(BF16)
    y_lo = (y - y_hi.astype(F32)).astype(BF16)
    lp = lpos_ref[...]
    cio = _iota2((tm, lr), 1).astype(F32)
    g = gate_ref[...]
    moe = None
    for s in range(2):
        sel = jnp.where(cio == lp[:, s:s + 1], 1.0, 0.0).astype(BF16)
        picked = _dot(sel, y_hi) + _dot(sel, y_lo)
        moe = g[:, s:s + 1] * picked if moe is None else moe + g[:, s:s + 1] * picked
    o_ref[...] = _rms(x_ref[...] + moe, nf_ref[...])


def _combine_call(tab, x, gate, lpos, nf, ys, *, tm, tok0):
    n = x.shape[0]
    blk0 = tok0 // tm
    lr = _local_rows(tm)
    return pl.pallas_call(
        functools.partial(_combine_kernel, tm=tm, blk0=blk0, lr=lr),
        grid_spec=pltpu.PrefetchScalarGridSpec(
            num_scalar_prefetch=1,
            grid=(n // tm,),
            in_specs=[
                pl.BlockSpec((tm, D), lambda i, *_: (i, 0)),
                pl.BlockSpec((tm, 2), lambda i, *_: (blk0 + i, 0)),
                pl.BlockSpec((tm, 2), lambda i, *_: (blk0 + i, 0)),
                pl.BlockSpec((1, D), lambda i, *_: (0, 0)),
                pl.BlockSpec(memory_space=pl.ANY),
            ],
            out_specs=pl.BlockSpec((tm, D), lambda i, *_: (i, 0)),
            scratch_shapes=[pltpu.VMEM((2, lr, D), F32), pltpu.SemaphoreType.DMA((2,))],
        ),
        out_shape=jax.ShapeDtypeStruct((n, D), F32),
        compiler_params=pltpu.CompilerParams(dimension_semantics=("arbitrary",), vmem_limit_bytes=V7X_VMEM_LIMIT),
        name=f"moe_combine_{n}",
    )(tab, x, gate, lpos, nf, ys)


def kernel(x_prompt, x_sample, state_hgrn, state_gdn, state_gdn_conv, meta_tokens, norm_mix, norm_ffn, norm_final,
           hg_w_in, hg_lower_bounds, hg_g_norm, hg_w_o, gdn_w_in, gdn_conv_w, gdn_a_log, gdn_dt_bias, gdn_g_norm,
           gdn_w_o, ffn_w1, ffn_w3, ffn_w2, moe_w_router, moe_w1, moe_w3, moe_w2):
    bp, tp, _ = x_prompt.shape
    bs, ts, _ = x_sample.shape
    np_, ns_ = bp * tp, bs * ts
    n_tok = np_ + ns_
    row = lambda v: v.reshape(1, -1).astype(F32)

    hg_win = hg_w_in.astype(BF16)
    hg_wo = hg_w_o.astype(BF16)
    gdn_win = jnp.concatenate(
        [gdn_w_in[:, :QKV + D], gdn_w_in[:, QKV + D:], jnp.zeros((D, LANES - 2 * H), gdn_w_in.dtype)],
        axis=1).astype(BF16)
    gdn_wo = gdn_w_o.astype(BF16)
    par = jnp.zeros((2, LANES), F32)
    par = par.at[0, H:2 * H].set(gdn_dt_bias.astype(F32)).at[1, H:2 * H].set(gdn_a_log.astype(F32))
    w1d, w3d, w2d = ffn_w1.astype(BF16), ffn_w3.astype(BF16), ffn_w2.astype(BF16)
    w1e, w3e, w2e = moe_w1.astype(BF16), moe_w3.astype(BF16), moe_w2.astype(BF16)
    wr = jnp.concatenate([moe_w_router.astype(F32), jnp.zeros((D, LANES - NE), F32)], axis=1)
    nm0, nm1 = row(norm_mix[0]), row(norm_mix[1])
    nf0, nf1 = row(norm_ffn[0]), row(norm_ffn[1])
    nfin = row(norm_final)
    hg_gn, gdn_gn = row(hg_g_norm), row(gdn_g_norm)
    lbnd = hg_lower_bounds.astype(F32)
    cw = gdn_conv_w.astype(F32)

    hg_args = (nm0, hg_win, lbnd, hg_gn, hg_wo)
    gdn_args = (nm1, gdn_win, cw, par, gdn_gn, gdn_wo)
    xm = meta_tokens.astype(F32)
    xp2 = x_prompt.reshape(np_, D)
    xs2 = x_sample.reshape(ns_, D)

    zero_state = jnp.zeros((1, H, HD, HD), F32)
    h1m, hg_m = _hgrn_call(xm, zero_state, *hg_args, nseq=1, seqlen=N_META, bb=1, tt=N_META)
    h1p, hg_p = _hgrn_call(xp2, hg_m, *hg_args, nseq=bp, seqlen=tp, bb=1, tt=256, nu=4, bcast_state=True)
    h1s, hg_s = _hgrn_call(xs2, state_hgrn.astype(F32), *hg_args, nseq=bs, seqlen=ts, bb=8, tt=ts)
    h2m = _ffn_call([h1m], nf0, w1d, w3d, w2d, tm=N_META)
    h2 = _ffn_call([h1p, h1s], nf0, w1d, w3d, w2d, tm=256)

    zero_conv = jnp.zeros((1, CONV - 1, QKV), F32)
    _, gdn_m, conv_m = _gdn_call(h2m, zero_state, zero_conv, *gdn_args, nseq=1, seqlen=N_META, bb=1, tt=N_META, c=16)
    h3p, gdn_p, conv_p = _gdn_call(h2, gdn_m, conv_m, *gdn_args, nseq=bp, seqlen=tp, bb=1, tt=256, c=128, nu=2,
                                   bcast_state=True)
    h3s, gdn_s, conv_s = _gdn_call(h2, state_gdn.astype(F32), state_gdn_conv.astype(F32), *gdn_args,
                                   nseq=bs, seqlen=ts, bb=8, tt=ts, c=8, nu=4, row0=np_)

    yp, ysm = _moe_final(h3p, h3s, nf1, wr, w1e, w3e, w2e, nfin, tm=256)

    sd, gd, cd = state_hgrn.dtype, state_gdn.dtype, state_gdn_conv.dtype
    return (yp.reshape(bp, tp, D), ysm.reshape(bs, ts, D),
            hg_p.astype(sd), hg_s.astype(sd),
            gdn_p.astype(gd), gdn_s.astype(gd), conv_p.astype(cd), conv_s.astype(cd))


def _moe_final(h3p, h3s, nf1, wr, w1e, w3e, w2e, nfin, *, tm):
    np_ = h3p.shape[0]
    n_tok = np_ + h3s.shape[0]
    nblk = n_tok // tm
    gate, lpos, lpost, tab, tot = _router_call(h3p, h3s, nf1, wr, tm=tm)
    totals = tot[0, :NE].astype(jnp.int32)
    padded = ((totals + TM - 1) // TM) * TM
    ends = jnp.cumsum(padded)
    offs = ends - padded
    tab3 = tab.reshape(nblk, 8, LANES)[:, :3, :NE].astype(jnp.int32)
    seg_tab = jnp.stack([tab3[:, 0] // SEG, tab3[:, 1], tab3[:, 2] + offs[None, :]], axis=-1).reshape(-1)
    ntile = -(-(2 * n_tok + nblk * NE * (SEG - 1) + NE * (TM - 1)) // TM)
    nact = ends[-1] // TM
    tiles = jnp.arange(ntile, dtype=jnp.int32)
    texp_all = jnp.minimum(jnp.sum(tiles[:, None] * TM >= ends[None, :], axis=1), NE - 1).astype(jnp.int32)
    last = jnp.maximum(nact - 1, 0)
    active = tiles < nact
    texp = jnp.where(active, texp_all, texp_all[last]).astype(jnp.int32)
    tsrc = jnp.where(active, tiles, last).astype(jnp.int32)
    tail = nact + jnp.arange(ntile - 2 * n_tok // TM)
    zst = jnp.concatenate([jnp.where(padded > 0, ends - TM, -1),
                           jnp.where(tail < ntile, tail * TM, -1)]).astype(jnp.int32)
    xs_sorted = _dispatch_call(seg_tab, zst, h3p, h3s, lpost, tm=tm, rmax=ntile * TM)
    ys = _expert_call(texp, tsrc, nact.reshape(1).astype(jnp.int32), xs_sorted, nf1, w1e, w3e, w2e)
    yp = _combine_call(seg_tab, h3p, gate, lpos, nfin, ys, tm=tm, tok0=0)
    ysm = _combine_call(seg_tab, h3s, gate, lpos, nfin, ys, tm=tm, tok0=np_)
    return yp, ysm
```

```python
import functools

import jax
import jax.numpy as jnp
from jax import lax
from jax.experimental import pallas as pl
from jax.experimental.pallas import tpu as pltpu

D = 1024
H = 8
HD = 128
QKV = 3 * H * HD
CONV = 4
DFF = 3584
NE = 8
N_META = 16
EPS = 1e-6
LANES = 128
FB = 512
TM = 256
HG_SUB = 16
SEG = 8
V7X_VMEM_LIMIT = 56 * 1024 * 1024

F32 = jnp.float32
BF16 = jnp.bfloat16
HI = lax.Precision.HIGHEST


def _dot(a, b, precision=None):
    return jnp.dot(a, b, preferred_element_type=F32, precision=precision)


def _dot_nt(a, b, precision=None):
    return lax.dot_general(a, b, (((1,), (1,)), ((), ())), preferred_element_type=F32, precision=precision)


def _dot_tn(a, b, precision=None):
    return lax.dot_general(a, b, (((0,), (0,)), ((), ())), preferred_element_type=F32, precision=precision)


def _split3(x):
    x1 = x.astype(BF16)
    r1 = x - x1.astype(F32)
    x2 = r1.astype(BF16)
    x3 = (r1 - x2.astype(F32)).astype(BF16)
    return x1, x2, x3


def _sel_dot(sel, x):
    x1, x2, x3 = _split3(x)
    return _dot(sel, x1) + _dot(sel, x2) + _dot(sel, x3)


def _sel_dot2(sel, x):
    hi = x.astype(BF16)
    lo = (x - hi.astype(F32)).astype(BF16)
    return _dot(sel, hi) + _dot(sel, lo)


def _sel_dot_nt(sel, x):
    x1, x2, x3 = _split3(x)
    return _dot_nt(sel, x1) + _dot_nt(sel, x2) + _dot_nt(sel, x3)


def _rms(x, g):
    return x * lax.rsqrt(jnp.mean(x * x, axis=-1, keepdims=True) + EPS) * g


def _silu(x):
    return x * jax.nn.sigmoid(x)


def _iota2(shape, dim):
    return lax.broadcasted_iota(jnp.int32, shape, dim)


def _head_rms_gate(o, g, gate):
    parts = []
    for h in range(H):
        oh = o[:, h * HD:(h + 1) * HD]
        parts.append(_rms(oh, g))
    return jnp.concatenate(parts, axis=1) * gate


def _chunk_row(idx, j, nu, c):
    if isinstance(idx, int):
        return (idx * nu + j) * c
    return pl.multiple_of((idx * nu + j) * c, c)


def _chunk_seq(idx, j, nu, ncs):
    return (idx * nu + j) // ncs


def _chunk_ops(tri, c):
    if c % 16 == 0:
        tri_b = tri.astype(BF16)
        return (lambda x: _sel_dot(tri_b, x)), (lambda x: x.astype(BF16))
    tri_f = tri.astype(F32)
    return (lambda x: _dot(tri_f, x, HI)), (lambda x: x)


def _run_groups(group, n):
    if n == 1:
        group(0, 0)
    else:
        lax.fori_loop(0, n, group, 0)


def _const_spec(shape):
    nd = len(shape)
    return pl.BlockSpec(shape, lambda *_: (0,) * nd, pipeline_mode=pl.Buffered(1))


def _hgrn_kernel(x_ref, s0_ref, nw_ref, win_ref, lb_ref, gn_ref, wo_ref,
                 o_ref, sout_ref,
                 st_scr, q_scr, k_scr, lf_scr, v_scr, g_scr, *, bb, tt, c, nu):
    t = pl.program_id(1)
    rows = bb * tt
    sl = min(c, tt)
    spc = c // sl
    ncs = max(tt // c, 1)
    assert c % sl == 0 and (ncs == 1 or ncs % nu == 0)

    @pl.when(t == 0)
    def _():
        for b in range(bb):
            for h in range(H):
                st_scr[b, h] = s0_ref[b, h].T

    x = x_ref[...]
    xn = _rms(x, nw_ref[...]).astype(BF16)
    lbn = lb_ref[...]
    e = jnp.exp(lbn - jnp.max(lbn, axis=0, keepdims=True))
    lb = e[0:1] / jnp.sum(e, axis=0, keepdims=True)
    q_scr[...] = _silu(_dot(xn, win_ref[:, 0:D])) * HD ** -0.5
    f = lb + (1.0 - lb) * jax.nn.sigmoid(_dot(xn, win_ref[:, D:2 * D]))
    k_scr[...] = 1.0 - f
    lf_scr[...] = jnp.log(f)
    v_scr[...] = _dot(xn, win_ref[:, 2 * D:3 * D])
    g_scr[...] = _silu(_dot(xn, win_ref[:, 3 * D:4 * D]))

    row = _iota2((c, c), 0)
    col = _iota2((c, c), 1)
    tri = (row >= col) & ((row // sl) == (col // sl))
    cumsum, mxu = _chunk_ops(tri, c)
    heads = [slice(h * HD, (h + 1) * HD) for h in range(H)]
    nsub = c // HG_SUB if sl > HG_SUB else 1
    if nsub > 1:
        local_tri = (tri & ((row // HG_SUB) == (col // HG_SUB))).astype(BF16)
        slab_row = _iota2((c, D), 0)
    if spc > 1:
        seq_last = (col == (row // sl) * sl + (sl - 1)).astype(BF16)

    def chunk_terms(r0):
        lf = lf_scr[pl.ds(r0, c), :]
        bc = cumsum(lf)
        tot = bc[c - 1:c, :] if spc == 1 else _sel_dot(seq_last, bc)
        qq = q_scr[pl.ds(r0, c), :]
        kk = k_scr[pl.ds(r0, c), :]
        v32 = v_scr[pl.ds(r0, c), :]
        vv = mxu(v32)
        qd32 = qq * jnp.exp(bc)
        kt32 = kk * jnp.exp(tot - bc)
        qdj = mxu(qd32)
        ktail = mxu(kt32)
        if nsub == 1:
            kinv = mxu(kk * jnp.exp(-bc))
            scj = [_dot_nt(qdj[:, s], kinv[:, s]) for s in heads]
        else:
            bl = _sel_dot(local_tri, lf)
            qloc = mxu(qq * jnp.exp(bl))
            parts = [[] for _ in heads]
            for i in range(nsub):
                lo, hi = i * HG_SUB, (i + 1) * HG_SUB
                ref_pt = bc[lo - 1:lo, :] if i else 0.0
                kci = mxu(jnp.where(slab_row < hi, kk * jnp.exp(ref_pt - bc), 0.0))
                for h, s in enumerate(heads):
                    parts[h].append(_dot_nt(qloc[lo:hi, s], kci[:, s]))
            scj = [jnp.concatenate(p, axis=0) for p in parts]
        scj = [mxu(jnp.where(tri, sc, 0.0)) for sc in scj]
        oij = [_dot(scj[h], vv[:, heads[h]]) for h in range(H)]
        if spc > 1:
            return qd32, jnp.exp(tot), (v32, kt32), oij
        kvj = [_dot_tn(vv[:, s], ktail[:, s]) for s in heads]
        return qdj, jnp.exp(tot), kvj, oij

    def group(idx, carry):
        qd, dec, kv, oi, r0s, sqs = [], [], [], [], [], []
        for j in range(nu):
            r0 = _chunk_row(idx, j, nu, c)
            qdj, decj, kvj, oij = chunk_terms(r0)
            kv.append(kvj)
            oi.append(oij)
            qd.append(qdj)
            dec.append(decj)
            r0s.append(r0)
            sqs.append(_chunk_seq(idx, j, nu, ncs) if spc == 1 else (idx * nu + j) * spc)

        if spc > 1:
            outs, finals = [], []
            for j in range(nu):
                v32, kt32 = kv[j]
                o_j = []
                for h, s in enumerate(heads):
                    parts = []
                    for q in range(spc):
                        rs = slice(q * sl, (q + 1) * sl)
                        st = st_scr[sqs[j] + q, h]
                        parts.append(oi[j][h][rs] + _dot_nt(qd[j][rs, s], st))
                        finals.append((sqs[j] + q, h,
                                       st * dec[j][q * sl:q * sl + 1, s] + _dot_tn(v32[rs, s], kt32[rs, s])))
                    o_j.append(jnp.concatenate(parts, axis=0))
                outs.append(o_j)
            for j in range(nu):
                q_scr[pl.ds(r0s[j], c), :] = jnp.concatenate(outs[j], axis=1)
            for sq, h, st_new in finals:
                st_scr[sq, h] = st_new
            return carry

        own_state = ncs == 1
        loaded = [[st_scr[sqs[j], h] for h in range(H)] for j in range(nu if own_state else 1)]
        outs, finals = [], []
        st = loaded[0]
        for j in range(nu):
            if own_state:
                st = loaded[j]
            outs.append([oi[j][h] + _dot_nt(qd[j][:, heads[h]], mxu(st[h])) for h in range(H)])
            st = [st[h] * dec[j][:, heads[h]] + kv[j][h] for h in range(H)]
            if own_state or j == nu - 1:
                finals.append((sqs[j], st))
        for j in range(nu):
            q_scr[pl.ds(r0s[j], c), :] = jnp.concatenate(outs[j], axis=1)
        for sq, st in finals:
            for h in range(H):
                st_scr[sq, h] = st[h]
        return carry

    _run_groups(group, rows // (nu * c))

    on = _head_rms_gate(q_scr[...], gn_ref[...], g_scr[...])
    o_ref[...] = x + _dot(on.astype(BF16), wo_ref[...])

    @pl.when(t == pl.num_programs(1) - 1)
    def _():
        for b in range(bb):
            for h in range(H):
                sout_ref[b, h] = st_scr[b, h].T


def _hgrn_call(x2d, s0, nw, win, lbnd, gn, wo, *, nseq, seqlen, bb, tt, c, nu=4, row0=0, bcast_state=False):
    assert bb == 1 or tt == seqlen
    assert nseq % bb == 0 and seqlen % tt == 0 and row0 % (bb * tt) == 0
    assert tt % c == 0 or (c % tt == 0 and (bb * tt) % c == 0)
    rows = bb * tt
    nu = min(nu, rows // c)
    nt = seqlen // tt
    blk0 = row0 // rows
    s_idx = (lambda b, t: (0, 0, 0, 0)) if bcast_state else (lambda b, t: (b, 0, 0, 0))
    slab = pltpu.VMEM((rows, D), F32)
    return pl.pallas_call(
        functools.partial(_hgrn_kernel, bb=bb, tt=tt, c=c, nu=nu),
        grid=(nseq // bb, nt),
        in_specs=[
            pl.BlockSpec((rows, D), lambda b, t: (blk0 + b * nt + t, 0)),
            pl.BlockSpec((bb, H, HD, HD), s_idx),
            _const_spec((1, D)),
            _const_spec((D, 4 * D)),
            _const_spec(lbnd.shape),
            _const_spec((1, HD)),
            _const_spec((D, D)),
        ],
        out_specs=[
            pl.BlockSpec((rows, D), lambda b, t: (b * nt + t, 0)),
            pl.BlockSpec((bb, H, HD, HD), lambda b, t: (b, 0, 0, 0)),
        ],
        out_shape=[
            jax.ShapeDtypeStruct((nseq * seqlen, D), F32),
            jax.ShapeDtypeStruct((nseq, H, HD, HD), F32),
        ],
        scratch_shapes=[pltpu.VMEM((bb, H, HD, HD), F32), slab, slab, slab, slab, slab],
        compiler_params=pltpu.CompilerParams(
            dimension_semantics=("arbitrary", "arbitrary"), vmem_limit_bytes=V7X_VMEM_LIMIT),
        name=f"hgrn_mixer_{nseq}x{seqlen}",
    )(x2d, s0, nw, win, lbnd, gn, wo)


def _neumann_inv(ls, eye, levels, mxu):
    xs = [eye - l for l in ls]
    ps = [mxu(l) for l in ls]
    for _ in range(levels - 1):
        ps = [mxu(_dot(p, p)) for p in ps]
        xs = [x + _dot(mxu(x), p) for x, p in zip(xs, ps)]
    return xs


def _unit_lower_inv(ls, c, mxu, span=None):
    span = c if span is None else span
    row = _iota2((c, c), 0)
    col = _iota2((c, c), 1)
    eye = (row == col).astype(F32)
    if span <= 16:
        return _neumann_inv(ls, eye, span.bit_length() - 1, mxu)
    sub = 16
    nblk = c // sub
    assert c == nblk * sub and nblk & (nblk - 1) == 0
    diag = (row // sub) == (col // sub)
    lds = [jnp.where(diag, l, 0.0) for l in ls]
    dinvs = [mxu(d) for d in _neumann_inv(lds, eye, 4, mxu)]
    ms = [_dot(d, mxu(l - ld)) for d, l, ld in zip(dinvs, ls, lds)]
    ys = _neumann_inv(ms, eye, nblk.bit_length() - 1, mxu)
    return [_dot(mxu(y), d) for y, d in zip(ys, dinvs)]


def _gdn_kernel(x_ref, s0_ref, c0_ref, nw_ref, win_ref, cw_ref, par_ref, gn_ref, wo_ref,
                o_ref, sout_ref, cout_ref,
                s_scr, xp_scr, act_scr, z_scr, la_scr, be_scr, o_scr, *, bb, tt, c, nu):
    t = pl.program_id(1)
    rows = bb * tt
    sl = min(c, tt)
    spc = c // sl
    ncs = max(tt // c, 1)
    assert c % sl == 0 and (ncs == 1 or ncs % nu == 0)

    @pl.when(t == 0)
    def _():
        s_scr[...] = s0_ref[...]
        for b in range(bb):
            xp_scr[b, 8 - (CONV - 1):8, :] = c0_ref[b]

    x = x_ref[...]
    xn = _rms(x, nw_ref[...]).astype(BF16)
    pre = _dot(xn, win_ref[:, 0:QKV])
    for b in range(bb):
        xp_scr[b, 8:8 + tt, :] = pre[b * tt:(b + 1) * tt]
    z_scr[...] = _silu(_dot(xn, win_ref[:, QKV:QKV + D]))
    ba = _dot(xn, win_ref[:, QKV + D:QKV + D + LANES])
    par = par_ref[...]
    sp = ba + par[0:1]
    softplus = jnp.maximum(sp, 0.0) + jnp.log(1.0 + jnp.exp(-jnp.abs(sp)))
    la_scr[...] = -jnp.exp(par[1:2]) * softplus
    be_scr[...] = jax.nn.sigmoid(ba)

    cw = cw_ref[...]
    for b in range(bb):
        conv = xp_scr[b, 8 - (CONV - 1):8 - (CONV - 1) + tt, :] * cw[0:1]
        for w in range(1, CONV):
            conv = conv + xp_scr[b, 8 - (CONV - 1) + w:8 - (CONV - 1) + w + tt, :] * cw[w:w + 1]
        act = _silu(conv)
        for j in range(2 * H):
            a = act[:, j * HD:(j + 1) * HD]
            scale = HD ** -0.5 if j < H else 1.0
            act_scr[b * tt:(b + 1) * tt, j * HD:(j + 1) * HD] = (
                a * lax.rsqrt(jnp.sum(a * a, axis=-1, keepdims=True) + EPS) * scale)
        act_scr[b * tt:(b + 1) * tt, 2 * H * HD:] = act[:, 2 * H * HD:]
        tail = xp_scr[b, tt:tt + 8, :]
        xp_scr[b, 0:8, :] = tail

    row = _iota2((c, c), 0)
    col = _iota2((c, c), 1)
    same_seq = (row // sl) == (col // sl)
    incl = (row >= col) & same_seq
    strict = (row > col) & same_seq
    cumsum, mxu = _chunk_ops(incl, c)
    if c % 16 == 0:
        eye_l = (_iota2((LANES, LANES), 0) == _iota2((LANES, LANES), 1)).astype(BF16)
        transpose = lambda m: _sel_dot_nt(eye_l, m)
    else:
        eye_l = (_iota2((LANES, LANES), 0) == _iota2((LANES, LANES), 1)).astype(F32)
        transpose = lambda m: _dot_nt(eye_l, m, HI)
    if spc > 1:
        seq_last = (col == (row // sl) * sl + (sl - 1)).astype(BF16)
    narrow = mxu if spc == 1 else (lambda a: a)

    heads = [slice(h * HD, (h + 1) * HD) for h in range(H)]

    def group(idx, carry):
        qd, ktail, dtot, lmat, amat, rwu, r0s, sqs = [], [], [], [], [], [], [], []
        for j in range(nu):
            r0 = _chunk_row(idx, j, nu, c)
            gc = cumsum(la_scr[pl.ds(r0, c), :])
            gt = transpose(gc)
            be = be_scr[pl.ds(r0, c), :]
            eg = jnp.exp(gc)
            gl = gc[c - 1:c, :] if spc == 1 else _sel_dot(seq_last, gc)
            et = jnp.exp(gl - gc)
            dt = jnp.exp(gl)
            qq = act_scr[pl.ds(r0, c), 0:H * HD]
            kk = act_scr[pl.ds(r0, c), H * HD:2 * H * HD]
            vv = act_scr[pl.ds(r0, c), 2 * H * HD:]
            qd_j, ktail_j, dtot_j = [], [], []
            for h, s in enumerate(heads):
                g_col = gc[:, 8 + h:9 + h]
                eg_col = eg[:, 8 + h:9 + h]
                be_col = be[:, h:h + 1]
                kh = mxu(kk[:, s])
                qk = _dot_nt(jnp.concatenate([mxu(qq[:, s]), kh], axis=0), kh)
                ed = jnp.exp(g_col - gt[8 + h:9 + h, :])
                lmat.append(be_col * qk[c:] * jnp.where(strict, ed, 0.0))
                amat.append(mxu(qk[:c] * jnp.where(incl, ed, 0.0)))
                rwu.append(mxu(jnp.concatenate([kk[:, s] * (be_col * eg_col), vv[:, s] * be_col], axis=1)))
                qd_j.append(narrow(qq[:, s] * eg_col))
                ktail_j.append(narrow(kk[:, s] * et[:, 8 + h:9 + h]))
                dtot_j.append(dt[:, 8 + h:9 + h])
            qd.append(qd_j)
            ktail.append(ktail_j)
            dtot.append(dtot_j)
            r0s.append(r0)
            sqs.append(_chunk_seq(idx, j, nu, ncs) if spc == 1 else (idx * nu + j) * spc)
        tinv = _unit_lower_inv(lmat, c, mxu, span=sl)
        wu = [_dot(mxu(t_), r_) for t_, r_ in zip(tinv, rwu)]

        if spc > 1:
            outs, finals = [], []
            for j in range(nu):
                o_j = []
                for h in range(H):
                    wuh = wu[j * H + h]
                    us, oins = [], []
                    for q in range(spc):
                        rs = slice(q * sl, (q + 1) * sl)
                        s_old = s_scr[sqs[j] + q, h]
                        qw = _dot(mxu(jnp.concatenate([qd[j][h][rs], wuh[rs, :HD]], axis=0)), mxu(s_old))
                        u = wuh[rs, HD:] - qw[sl:]
                        us.append(u)
                        oins.append(qw[:sl])
                        finals.append((sqs[j] + q, h,
                                       s_old * dtot[j][h][q * sl:q * sl + 1] + _dot_tn(ktail[j][h][rs], u)))
                    o_j.append(jnp.concatenate(oins, axis=0)
                               + _dot(amat[j * H + h], mxu(jnp.concatenate(us, axis=0))))
                outs.append(o_j)
            for j in range(nu):
                o_scr[pl.ds(r0s[j], c), :] = jnp.concatenate(outs[j], axis=1)
            for sq, h, s_new in finals:
                s_scr[sq, h] = s_new
            return carry

        own_state = ncs == 1
        loaded = [[s_scr[sqs[j], h] for h in range(H)] for j in range(nu if own_state else 1)]
        outs, finals = [], []
        st = loaded[0]
        for j in range(nu):
            if own_state:
                st = loaded[j]
            o_j, st_next = [], []
            for h in range(H):
                wuh = wu[j * H + h]
                qw = _dot(jnp.concatenate([qd[j][h], mxu(wuh[:, :HD])], axis=0), mxu(st[h]))
                u = wuh[:, HD:] - qw[c:]
                ub = mxu(u)
                o_j.append(qw[:c] + _dot(amat[j * H + h], ub))
                st_next.append(st[h] * dtot[j][h] + _dot_tn(ktail[j][h], ub))
            st = st_next
            outs.append(o_j)
            if own_state or j == nu - 1:
                finals.append((sqs[j], st))
        for j in range(nu):
            o_scr[pl.ds(r0s[j], c), :] = jnp.concatenate(outs[j], axis=1)
        for sq, st in finals:
            for h in range(H):
                s_scr[sq, h] = st[h]
        return carry

    _run_groups(group, rows // (nu * c))

    on = _head_rms_gate(o_scr[...], gn_ref[...], z_scr[...])
    o_ref[...] = x + _dot(on.astype(BF16), wo_ref[...])

    @pl.when(t == pl.num_programs(1) - 1)
    def _():
        sout_ref[...] = s_scr[...]
        for b in range(bb):
            cout_ref[b] = xp_scr[b, 8 - (CONV - 1):8, :]


def _gdn_call(x2d, s0, c0, nw, win, cw, par, gn, wo, *, nseq, seqlen, bb, tt, c, nu=2, row0=0,
              bcast_state=False):
    assert bb == 1 or tt == seqlen
    assert nseq % bb == 0 and seqlen % tt == 0 and tt % 8 == 0 and row0 % (bb * tt) == 0
    assert tt % c == 0 or (c % tt == 0 and (bb * tt) % c == 0)
    rows = bb * tt
    nt = seqlen // tt
    blk0 = row0 // rows
    s_idx = (lambda b, t: (0, 0, 0, 0)) if bcast_state else (lambda b, t: (b, 0, 0, 0))
    c_idx = (lambda b, t: (0, 0, 0)) if bcast_state else (lambda b, t: (b, 0, 0))
    slab = pltpu.VMEM((rows, D), F32)
    return pl.pallas_call(
        functools.partial(_gdn_kernel, bb=bb, tt=tt, c=c, nu=min(nu, bb * tt // c)),
        grid=(nseq // bb, nt),
        in_specs=[
            pl.BlockSpec((rows, D), lambda b, t: (blk0 + b * nt + t, 0)),
            pl.BlockSpec((bb, H, HD, HD), s_idx),
            pl.BlockSpec((bb, CONV - 1, QKV), c_idx),
            _const_spec((1, D)),
            _const_spec(win.shape),
            _const_spec((CONV, QKV)),
            _const_spec((2, LANES)),
            _const_spec((1, HD)),
            _const_spec((D, D)),
        ],
        out_specs=[
            pl.BlockSpec((rows, D), lambda b, t: (b * nt + t, 0)),
            pl.BlockSpec((bb, H, HD, HD), lambda b, t: (b, 0, 0, 0)),
            pl.BlockSpec((bb, CONV - 1, QKV), lambda b, t: (b, 0, 0)),
        ],
        out_shape=[
            jax.ShapeDtypeStruct((nseq * seqlen, D), F32),
            jax.ShapeDtypeStruct((nseq, H, HD, HD), F32),
            jax.ShapeDtypeStruct((nseq, CONV - 1, QKV), F32),
        ],
        scratch_shapes=[
            pltpu.VMEM((bb, H, HD, HD), F32),
            pltpu.VMEM((bb, tt + 8, QKV), F32),
            pltpu.VMEM((rows, QKV), F32),
            slab,
            pltpu.VMEM((rows, LANES), F32),
            pltpu.VMEM((rows, LANES), F32),
            slab,
        ],
        compiler_params=pltpu.CompilerParams(
            dimension_semantics=("arbitrary", "arbitrary"), vmem_limit_bytes=V7X_VMEM_LIMIT),
        name=f"gdn_mixer_{nseq}x{seqlen}",
    )(x2d, s0, c0, nw, win, cw, par, gn, wo)


def _swiglu(xb, w1_ref, w3_ref, w2_ref):
    acc = None
    for j in range(DFF // FB):
        fs = slice(j * FB, (j + 1) * FB)
        hh = (_silu(_dot(xb, w1_ref[:, fs])) * _dot(xb, w3_ref[:, fs])).astype(BF16)
        part = _dot(hh, w2_ref[fs, :])
        acc = part if acc is None else acc + part
    return acc


def _ffn_kernel(*refs, n_in, nblk_first):
    x_refs = refs[:n_in]
    nw_ref, w1_ref, w3_ref, w2_ref, o_ref = refs[n_in:]
    x = x_refs[0][...]
    if n_in == 2:
        x = jnp.where(pl.program_id(0) < nblk_first, x, x_refs[1][...])
    xb = _rms(x, nw_ref[...]).astype(BF16)
    o_ref[...] = x + _swiglu(xb, w1_ref, w3_ref, w2_ref)


def _two_group_specs(tm, nblk_first, nblk_total):
    last_first = nblk_first - 1
    return [
        pl.BlockSpec((tm, D), lambda i, *_: (jnp.minimum(i, last_first), 0)),
        pl.BlockSpec((tm, D), lambda i, *_: (jnp.maximum(i - nblk_first, 0), 0)),
    ]


def _ffn_call(xs, nw, w1, w3, w2, *, tm):
    nblks = [x.shape[0] // tm for x in xs]
    assert all(x.shape[0] % tm == 0 for x in xs)
    total = sum(nblks)
    if len(xs) == 2:
        x_specs = _two_group_specs(tm, nblks[0], total)
    else:
        x_specs = [pl.BlockSpec((tm, D), lambda i: (i, 0))]
    return pl.pallas_call(
        functools.partial(_ffn_kernel, n_in=len(xs), nblk_first=nblks[0]),
        grid=(total,),
        in_specs=x_specs + [_const_spec((1, D)), _const_spec((D, DFF)), _const_spec((D, DFF)), _const_spec((DFF, D))],
        out_specs=pl.BlockSpec((tm, D), lambda i: (i, 0)),
        out_shape=jax.ShapeDtypeStruct((total * tm, D), F32),
        compiler_params=pltpu.CompilerParams(dimension_semantics=("arbitrary",), vmem_limit_bytes=V7X_VMEM_LIMIT),
        name=f"ffn_dense_{total * tm}",
    )(*xs, nw, w1, w3, w2)


def _router_kernel(xa_ref, xb_ref, nw_ref, wr_ref, gate_ref, lpos_ref, lpost_ref, tab_ref, tot_ref, seg_scr, *,
                   tm, nblk_first):
    i = pl.program_id(0)

    @pl.when(i == 0)
    def _():
        seg_scr[...] = jnp.zeros_like(seg_scr)

    x = jnp.where(i < nblk_first, xa_ref[...], xb_ref[...])
    xn = _rms(x, nw_ref[...])
    logits = _dot(xn, wr_ref[...], HI)
    lane = _iota2((tm, LANES), 1).astype(F32)
    neg = jnp.float32(-jnp.inf)
    l1 = jnp.where(lane < NE, logits, neg)
    m1 = jnp.max(l1, axis=-1, keepdims=True)
    i1 = jnp.min(jnp.where(l1 == m1, lane, float(LANES)), axis=-1, keepdims=True)
    l2 = jnp.where(lane == i1, neg, l1)
    m2 = jnp.max(l2, axis=-1, keepdims=True)
    i2 = jnp.min(jnp.where(l2 == m2, lane, float(LANES)), axis=-1, keepdims=True)
    e2 = jnp.exp(m2 - m1)
    g1 = 1.0 / (1.0 + e2)
    g2 = e2 / (1.0 + e2)

    oh1 = (lane == i1).astype(F32)
    oh2 = (lane == i2).astype(F32)
    below = (_iota2((tm, tm), 0) > _iota2((tm, tm), 1)).astype(BF16)
    c1 = _dot(below, oh1.astype(BF16))
    c2 = _dot(below, oh2.astype(BF16))
    n1 = jnp.sum(oh1, axis=0, keepdims=True)
    n2 = jnp.sum(oh2, axis=0, keepdims=True)
    cnt = jnp.floor((n1 + n2 + (SEG - 1)) * (1.0 / SEG)) * SEG
    before = (_iota2((LANES, LANES), 0) < _iota2((LANES, LANES), 1)).astype(BF16)
    start = _dot(jnp.broadcast_to(cnt, (16, LANES)).astype(BF16), before)[0:1]
    p1 = jnp.sum(oh1 * (start + c1), axis=-1, keepdims=True)
    p2 = jnp.sum(oh2 * (start + n1 + c2), axis=-1, keepdims=True)
    used = seg_scr[...]
    seg_scr[...] = used + cnt
    tot_ref[...] = used + cnt

    gate_ref[:, 0:1] = g1
    gate_ref[:, 1:2] = g2
    lpos_ref[:, 0:1] = p1
    lpos_ref[:, 1:2] = p2
    sub = _iota2((8, LANES), 0)
    tab_ref[...] = jnp.where(sub == 0, cnt, jnp.where(sub == 1, start, jnp.where(sub == 2, used, 0.0)))
    slab = jnp.where(lane == 0.0, p1, jnp.where(lane == 1.0, p2, 0.0))
    eye16 = (_iota2((16, LANES), 0) == _iota2((16, LANES), 1)).astype(BF16)
    lpost_ref[...] = _sel_dot_nt(eye16, slab)[0:8]


def _router_call(xa, xb, nw, wr, *, tm):
    na, nb = xa.shape[0] // tm, xb.shape[0] // tm
    nblk = na + nb
    n = nblk * tm
    return pl.pallas_call(
        functools.partial(_router_kernel, tm=tm, nblk_first=na),
        grid=(nblk,),
        in_specs=_two_group_specs(tm, na, nblk) + [_const_spec((1, D)), _const_spec((D, LANES))],
        out_specs=[
            pl.BlockSpec((tm, 2), lambda i: (i, 0)),
            pl.BlockSpec((tm, 2), lambda i: (i, 0)),
            pl.BlockSpec((8, tm), lambda i: (i, 0)),
            pl.BlockSpec((8, LANES), lambda i: (i, 0)),
            pl.BlockSpec((1, LANES), lambda i: (0, 0)),
        ],
        out_shape=[
            jax.ShapeDtypeStruct((n, 2), F32),
            jax.ShapeDtypeStruct((n, 2), F32),
            jax.ShapeDtypeStruct((nblk * 8, tm), F32),
            jax.ShapeDtypeStruct((nblk * 8, LANES), F32),
            jax.ShapeDtypeStruct((1, LANES), F32),
        ],
        scratch_shapes=[pltpu.VMEM((1, LANES), F32)],
        compiler_params=pltpu.CompilerParams(dimension_semantics=("arbitrary",), vmem_limit_bytes=V7X_VMEM_LIMIT),
        name="moe_router",
    )(xa, xb, nw, wr)


def _local_rows(tm):
    return -(-(2 * tm + NE * (SEG - 1)) // LANES) * LANES


def _segment_copies(tab_ref, tile, local_ref, global_ref, sem, *, to_global, wait):
    for e in range(NE):
        base = (tile * NE + e) * 3
        npiece, lstart, gstart = tab_ref[base], tab_ref[base + 1], tab_ref[base + 2]

        def piece(k, carry, lstart=lstart, gstart=gstart):
            loc = local_ref.at[pl.ds(pl.multiple_of(lstart + k * SEG, SEG), SEG)]
            glo = global_ref.at[pl.ds(pl.multiple_of(gstart + k * SEG, SEG), SEG)]
            cp = pltpu.make_async_copy(loc, glo, sem) if to_global else pltpu.make_async_copy(glo, loc, sem)
            if wait:
                cp.wait()
            else:
                cp.start()
            return carry

        lax.fori_loop(0, npiece, piece, 0)


def _dispatch_kernel(tab_ref, zst_ref, xa_ref, xb_ref, lpost_ref, xs_ref, buf, zero_scr, sem, zsem, *,
                     tm, nblk_first, lr):
    i = pl.program_id(0)
    slot = lax.rem(i, 2)

    @pl.when(i == 0)
    def _():
        zero_scr[...] = jnp.zeros_like(zero_scr)

        def zero_copy(e):
            start = pl.multiple_of(zst_ref[e], TM)
            return pltpu.make_async_copy(zero_scr, xs_ref.at[pl.ds(start, TM)], zsem)

        for e in range(zst_ref.shape[0]):
            pl.when(zst_ref[e] >= 0)(lambda e=e: zero_copy(e).start())
        for e in range(zst_ref.shape[0]):
            pl.when(zst_ref[e] >= 0)(lambda e=e: zero_copy(e).wait())

    x = jnp.where(i < nblk_first, xa_ref[...], xb_ref[...])
    lpt = lpost_ref[...]
    rio = _iota2((lr, tm), 0).astype(F32)
    sel = jnp.where(rio == lpt[0:1, :], 1.0, jnp.where(rio == lpt[1:2, :], 1.0, 0.0)).astype(BF16)
    buf[slot] = _sel_dot2(sel, x)

    copies = functools.partial(_segment_copies, tab_ref, global_ref=xs_ref, to_global=True)
    copies(i, local_ref=buf.at[slot], sem=sem.at[slot], wait=False)
    pl.when(i > 0)(lambda: copies(i - 1, local_ref=buf.at[1 - slot], sem=sem.at[1 - slot], wait=True))
    pl.when(i == pl.num_programs(0) - 1)(
        lambda: copies(i, local_ref=buf.at[slot], sem=sem.at[slot], wait=True))


def _dispatch_call(tab, zst, xa, xb, lpost, *, tm, rmax):
    na, nb = xa.shape[0] // tm, xb.shape[0] // tm
    lr = _local_rows(tm)
    return pl.pallas_call(
        functools.partial(_dispatch_kernel, tm=tm, nblk_first=na, lr=lr),
        grid_spec=pltpu.PrefetchScalarGridSpec(
            num_scalar_prefetch=2,
            grid=(na + nb,),
            in_specs=_two_group_specs(tm, na, na + nb) + [pl.BlockSpec((8, tm), lambda i, *_: (i, 0))],
            out_specs=pl.BlockSpec(memory_space=pl.ANY),
            scratch_shapes=[pltpu.VMEM((2, lr, D), F32), pltpu.VMEM((TM, D), F32),
                            pltpu.SemaphoreType.DMA((2,)), pltpu.SemaphoreType.DMA],
        ),
        out_shape=jax.ShapeDtypeStruct((rmax, D), F32),
        compiler_params=pltpu.CompilerParams(dimension_semantics=("arbitrary",), vmem_limit_bytes=V7X_VMEM_LIMIT),
        name="moe_dispatch",
    )(tab, zst, xa, xb, lpost)


def _expert_kernel(texp_ref, tsrc_ref, nact_ref, xs_ref, nw_ref, w1_ref, w3_ref, w2_ref, ys_ref):
    i = pl.program_id(0)

    @pl.when(i < nact_ref[0])
    def _():
        xb = _rms(xs_ref[...], nw_ref[...]).astype(BF16)
        ys_ref[...] = _swiglu(xb, w1_ref, w3_ref, w2_ref)

    @pl.when(i >= nact_ref[0])
    def _():
        ys_ref[...] = jnp.zeros_like(ys_ref)


def _expert_call(texp, tsrc, nact, xs, nw, w1, w3, w2):
    ntile = xs.shape[0] // TM
    return pl.pallas_call(
        _expert_kernel,
        grid_spec=pltpu.PrefetchScalarGridSpec(
            num_scalar_prefetch=3,
            grid=(ntile,),
            in_specs=[
                pl.BlockSpec((TM, D), lambda i, te, ts, na: (ts[i], 0)),
                pl.BlockSpec((1, D), lambda i, te, ts, na: (0, 0)),
                pl.BlockSpec((None, D, DFF), lambda i, te, ts, na: (te[i], 0, 0)),
                pl.BlockSpec((None, D, DFF), lambda i, te, ts, na: (te[i], 0, 0)),
                pl.BlockSpec((None, DFF, D), lambda i, te, ts, na: (te[i], 0, 0)),
            ],
            out_specs=pl.BlockSpec((TM, D), lambda i, te, ts, na: (i, 0)),
        ),
        out_shape=jax.ShapeDtypeStruct((xs.shape[0], D), F32),
        compiler_params=pltpu.CompilerParams(dimension_semantics=("arbitrary",), vmem_limit_bytes=V7X_VMEM_LIMIT),
        name="moe_experts",
    )(texp, tsrc, nact, xs, nw, w1, w3, w2)


def _combine_kernel(tab_ref, x_ref, gate_ref, lpos_ref, nf_ref, ys_ref, o_ref, buf, sem, *, tm, blk0, lr):
    i = pl.program_id(0)
    slot = lax.rem(i, 2)

    def fetch(step, slot_):
        tile = blk0 + step
        last = (tile * NE + NE - 1) * 3
        used = pl.multiple_of(tab_ref[last + 1] + tab_ref[last] * SEG, SEG)

        def clear(k, carry):
            buf[slot_, pl.ds(pl.multiple_of(used + k * SEG, SEG), SEG), :] = jnp.zeros((SEG, D), F32)
            return carry

        lax.fori_loop(0, (lr - used) // SEG, clear, 0)
        _segment_copies(tab_ref, tile, buf.at[slot_], ys_ref, sem.at[slot_], to_global=False, wait=False)

    pl.when(i == 0)(lambda: fetch(i, slot))
    pl.when(i + 1 < pl.num_programs(0))(lambda: fetch(i + 1, 1 - slot))
    _segment_copies(tab_ref, blk0 + i, buf.at[slot], ys_ref, sem.at[slot], to_global=False, wait=True)

    y = buf[slot]
    y_hi = y.astype(BF16)
    y_lo = (y - y_hi.astype(F32)).astype(BF16)
    lp = lpos_ref[...]
    cio = _iota2((tm, lr), 1).astype(F32)
    g = gate_ref[...]
    moe = None
    for s in range(2):
        sel = jnp.where(cio == lp[:, s:s + 1], 1.0, 0.0).astype(BF16)
        picked = _dot(sel, y_hi) + _dot(sel, y_lo)
        moe = g[:, s:s + 1] * picked if moe is None else moe + g[:, s:s + 1] * picked
    o_ref[...] = _rms(x_ref[...] + moe, nf_ref[...])


def _combine_call(tab, x, gate, lpos, nf, ys, *, tm, tok0):
    n = x.shape[0]
    blk0 = tok0 // tm
    lr = _local_rows(tm)
    return pl.pallas_call(
        functools.partial(_combine_kernel, tm=tm, blk0=blk0, lr=lr),
        grid_spec=pltpu.PrefetchScalarGridSpec(
            num_scalar_prefetch=1,
            grid=(n // tm,),
            in_specs=[
                pl.BlockSpec((tm, D), lambda i, *_: (i, 0)),
                pl.BlockSpec((tm, 2), lambda i, *_: (blk0 + i, 0)),
                pl.BlockSpec((tm, 2), lambda i, *_: (blk0 + i, 0)),
                pl.BlockSpec((1, D), lambda i, *_: (0, 0)),
                pl.BlockSpec(memory_space=pl.ANY),
            ],
            out_specs=pl.BlockSpec((tm, D), lambda i, *_: (i, 0)),
            scratch_shapes=[pltpu.VMEM((2, lr, D), F32), pltpu.SemaphoreType.DMA((2,))],
        ),
        out_shape=jax.ShapeDtypeStruct((n, D), F32),
        compiler_params=pltpu.CompilerParams(dimension_semantics=("arbitrary",), vmem_limit_bytes=V7X_VMEM_LIMIT),
        name=f"moe_combine_{n}",
    )(tab, x, gate, lpos, nf, ys)


def kernel(x_prompt, x_sample, state_hgrn, state_gdn, state_gdn_conv, meta_tokens, norm_mix, norm_ffn, norm_final,
           hg_w_in, hg_lower_bounds, hg_g_norm, hg_w_o, gdn_w_in, gdn_conv_w, gdn_a_log, gdn_dt_bias, gdn_g_norm,
           gdn_w_o, ffn_w1, ffn_w3, ffn_w2, moe_w_router, moe_w1, moe_w3, moe_w2):
    bp, tp, _ = x_prompt.shape
    bs, ts, _ = x_sample.shape
    np_, ns_ = bp * tp, bs * ts
    n_tok = np_ + ns_
    row = lambda v: v.reshape(1, -1).astype(F32)

    hg_win = hg_w_in.astype(BF16)
    hg_wo = hg_w_o.astype(BF16)
    gdn_win = jnp.concatenate(
        [gdn_w_in[:, :QKV + D], gdn_w_in[:, QKV + D:], jnp.zeros((D, LANES - 2 * H), gdn_w_in.dtype)],
        axis=1).astype(BF16)
    gdn_wo = gdn_w_o.astype(BF16)
    par = jnp.zeros((2, LANES), F32)
    par = par.at[0, H:2 * H].set(gdn_dt_bias.astype(F32)).at[1, H:2 * H].set(gdn_a_log.astype(F32))
    w1d, w3d, w2d = ffn_w1.astype(BF16), ffn_w3.astype(BF16), ffn_w2.astype(BF16)
    w1e, w3e, w2e = moe_w1.astype(BF16), moe_w3.astype(BF16), moe_w2.astype(BF16)
    wr = jnp.concatenate([moe_w_router.astype(F32), jnp.zeros((D, LANES - NE), F32)], axis=1)
    nm0, nm1 = row(norm_mix[0]), row(norm_mix[1])
    nf0, nf1 = row(norm_ffn[0]), row(norm_ffn[1])
    nfin = row(norm_final)
    hg_gn, gdn_gn = row(hg_g_norm), row(gdn_g_norm)
    lbnd = hg_lower_bounds.astype(F32)
    cw = gdn_conv_w.astype(F32)

    hg_args = (nm0, hg_win, lbnd, hg_gn, hg_wo)
    gdn_args = (nm1, gdn_win, cw, par, gdn_gn, gdn_wo)
    xm = meta_tokens.astype(F32)
    xp2 = x_prompt.reshape(np_, D)
    xs2 = x_sample.reshape(ns_, D)

    zero_state = jnp.zeros((1, H, HD, HD), F32)
    h1m, hg_m = _hgrn_call(xm, zero_state, *hg_args, nseq=1, seqlen=N_META, bb=1, tt=N_META, c=N_META)
    h1p, hg_p = _hgrn_call(xp2, hg_m, *hg_args, nseq=bp, seqlen=tp, bb=1, tt=256, c=64, nu=4, bcast_state=True)
    h1s, hg_s = _hgrn_call(xs2, state_hgrn.astype(F32), *hg_args, nseq=bs, seqlen=ts, bb=8, tt=ts, c=64, nu=1)
    h2m = _ffn_call([h1m], nf0, w1d, w3d, w2d, tm=N_META)
    h2 = _ffn_call([h1p, h1s], nf0, w1d, w3d, w2d, tm=256)

    zero_conv = jnp.zeros((1, CONV - 1, QKV), F32)
    _, gdn_m, conv_m = _gdn_call(h2m, zero_state, zero_conv, *gdn_args, nseq=1, seqlen=N_META, bb=1, tt=N_META, c=16)
    h3p, gdn_p, conv_p = _gdn_call(h2, gdn_m, conv_m, *gdn_args, nseq=bp, seqlen=tp, bb=1, tt=256, c=128, nu=2,
                                   bcast_state=True)
    h3s, gdn_s, conv_s = _gdn_call(h2, state_gdn.astype(F32), state_gdn_conv.astype(F32), *gdn_args,
                                   nseq=bs, seqlen=ts, bb=8, tt=ts, c=64, nu=1, row0=np_)

    yp, ysm = _moe_final(h3p, h3s, nf1, wr, w1e, w3e, w2e, nfin, tm=256)

    sd, gd, cd = state_hgrn.dtype, state_gdn.dtype, state_gdn_conv.dtype
    return (yp.reshape(bp, tp, D), ysm.reshape(bs, ts, D),
            hg_p.astype(sd), hg_s.astype(sd),
            gdn_p.astype(gd), gdn_s.astype(gd), conv_p.astype(cd), conv_s.astype(cd))


def _moe_final(h3p, h3s, nf1, wr, w1e, w3e, w2e, nfin, *, tm):
    np_ = h3p.shape[0]
    n_tok = np_ + h3s.shape[0]
    nblk = n_tok // tm
    gate, lpos, lpost, tab, tot = _router_call(h3p, h3s, nf1, wr, tm=tm)
    totals = tot[0, :NE].astype(jnp.int32)
    padded = ((totals + TM - 1) // TM) * TM
    ends = jnp.cumsum(padded)
    offs = ends - padded
    tab3 = tab.reshape(nblk, 8, LANES)[:, :3, :NE].astype(jnp.int32)
    seg_tab = jnp.stack([tab3[:, 0] // SEG, tab3[:, 1], tab3[:, 2] + offs[None, :]], axis=-1).reshape(-1)
    ntile = -(-(2 * n_tok + nblk * NE * (SEG - 1) + NE * (TM - 1)) // TM)
    nact = ends[-1] // TM
    tiles = jnp.arange(ntile, dtype=jnp.int32)
    texp_all = jnp.minimum(jnp.sum(tiles[:, None] * TM >= ends[None, :], axis=1), NE - 1).astype(jnp.int32)
    last = jnp.maximum(nact - 1, 0)
    active = tiles < nact
    texp = jnp.where(active, texp_all, texp_all[last]).astype(jnp.int32)
    tsrc = jnp.where(active, tiles, last).astype(jnp.int32)
    tail = nact + jnp.arange(ntile - 2 * n_tok // TM)
    zst = jnp.concatenate([jnp.where(padded > 0, ends - TM, -1),
                           jnp.where(tail < ntile, tail * TM, -1)]).astype(jnp.int32)
    xs_sorted = _dispatch_call(seg_tab, zst, h3p, h3s, lpost, tm=tm, rmax=ntile * TM)
    ys = _expert_call(texp, tsrc, nact.reshape(1).astype(jnp.int32), xs_sorted, nf1, w1e, w3e, w2e)
    yp = _combine_call(seg_tab, h3p, gate, lpos, nfin, ys, tm=tm, tok0=0)
    ysm = _combine_call(seg_tab, h3s, gate, lpos, nfin, ys, tm=tm, tok0=np_)
    return yp, ysm
```

```python
import functools

import jax
import jax.numpy as jnp
from jax import lax
from jax.experimental import pallas as pl
from jax.experimental.pallas import tpu as pltpu

D = 1024
H = 8
HD = 128
QKV = 3 * H * HD
CONV = 4
DFF = 3584
NE = 8
N_META = 16
EPS = 1e-6
LANES = 128
FB = 512
TM = 256
HG_SUB = 16
SEG = 8
V7X_VMEM_LIMIT = 56 * 1024 * 1024

F32 = jnp.float32
BF16 = jnp.bfloat16
HI = lax.Precision.HIGHEST


def _dot(a, b, precision=None):
    return jnp.dot(a, b, preferred_element_type=F32, precision=precision)


def _dot_nt(a, b, precision=None):
    return lax.dot_general(a, b, (((1,), (1,)), ((), ())), preferred_element_type=F32, precision=precision)


def _dot_tn(a, b, precision=None):
    return lax.dot_general(a, b, (((0,), (0,)), ((), ())), preferred_element_type=F32, precision=precision)


def _split3(x):
    x1 = x.astype(BF16)
    r1 = x - x1.astype(F32)
    x2 = r1.astype(BF16)
    x3 = (r1 - x2.astype(F32)).astype(BF16)
    return x1, x2, x3


def _sel_dot(sel, x):
    x1, x2, x3 = _split3(x)
    return _dot(sel, x1) + _dot(sel, x2) + _dot(sel, x3)


def _sel_dot2(sel, x):
    hi = x.astype(BF16)
    lo = (x - hi.astype(F32)).astype(BF16)
    return _dot(sel, hi) + _dot(sel, lo)


def _dot_3x(a, b):
    a_hi = a.astype(BF16)
    a_lo = (a - a_hi.astype(F32)).astype(BF16)
    b_hi = b.astype(BF16)
    b_lo = (b - b_hi.astype(F32)).astype(BF16)
    return _dot(a_hi, b_hi) + (_dot(a_hi, b_lo) + _dot(a_lo, b_hi))


def _sel_dot_nt(sel, x):
    x1, x2, x3 = _split3(x)
    return _dot_nt(sel, x1) + _dot_nt(sel, x2) + _dot_nt(sel, x3)


def _rms(x, g):
    return x * lax.rsqrt(jnp.mean(x * x, axis=-1, keepdims=True) + EPS) * g


def _silu(x):
    return x * jax.nn.sigmoid(x)


def _iota2(shape, dim):
    return lax.broadcasted_iota(jnp.int32, shape, dim)


def _head_rms_gate(o, g, gate):
    parts = []
    for h in range(H):
        oh = o[:, h * HD:(h + 1) * HD]
        parts.append(_rms(oh, g))
    return jnp.concatenate(parts, axis=1) * gate


def _chunk_row(idx, j, nu, c):
    if isinstance(idx, int):
        return (idx * nu + j) * c
    return pl.multiple_of((idx * nu + j) * c, c)


def _chunk_seq(idx, j, nu, ncs):
    return (idx * nu + j) // ncs


def _chunk_ops(tri, c):
    if c % 16 == 0:
        tri_b = tri.astype(BF16)
        return (lambda x: _sel_dot(tri_b, x)), (lambda x: x.astype(BF16))
    tri_f = tri.astype(F32)
    return (lambda x: _dot(tri_f, x, HI)), (lambda x: x)


def _run_groups(group, n):
    if n == 1:
        group(0, 0)
    else:
        lax.fori_loop(0, n, group, 0)


def _const_spec(shape):
    nd = len(shape)
    return pl.BlockSpec(shape, lambda *_: (0,) * nd, pipeline_mode=pl.Buffered(1))


def _hgrn_kernel(x_ref, s0_ref, nw_ref, win_ref, lb_ref, gn_ref, wo_ref,
                 o_ref, sout_ref,
                 st_scr, q_scr, k_scr, lf_scr, v_scr, g_scr, *, bb, tt, c, nu):
    t = pl.program_id(1)
    rows = bb * tt
    sl = min(c, tt)
    spc = c // sl
    ncs = max(tt // c, 1)
    assert c % sl == 0 and (ncs == 1 or ncs % nu == 0)

    @pl.when(t == 0)
    def _():
        for b in range(bb):
            for h in range(H):
                st_scr[b, h] = s0_ref[b, h].T

    x = x_ref[...]
    xn = _rms(x, nw_ref[...]).astype(BF16)
    lbn = lb_ref[...]
    e = jnp.exp(lbn - jnp.max(lbn, axis=0, keepdims=True))
    lb = e[0:1] / jnp.sum(e, axis=0, keepdims=True)
    q_scr[...] = _silu(_dot(xn, win_ref[:, 0:D])) * HD ** -0.5
    f = lb + (1.0 - lb) * jax.nn.sigmoid(_dot(xn, win_ref[:, D:2 * D]))
    k_scr[...] = 1.0 - f
    lf_scr[...] = jnp.log(f)
    v_scr[...] = _dot(xn, win_ref[:, 2 * D:3 * D])
    g_scr[...] = _silu(_dot(xn, win_ref[:, 3 * D:4 * D]))

    row = _iota2((c, c), 0)
    col = _iota2((c, c), 1)
    tri = (row >= col) & ((row // sl) == (col // sl))
    cumsum, mxu = _chunk_ops(tri, c)
    heads = [slice(h * HD, (h + 1) * HD) for h in range(H)]
    nsub = c // HG_SUB if sl > HG_SUB else 1
    if nsub > 1:
        local_tri = (tri & ((row // HG_SUB) == (col // HG_SUB))).astype(BF16)
        slab_row = _iota2((c, D), 0)
    if spc > 1:
        seq_last = (col == (row // sl) * sl + (sl - 1)).astype(BF16)

    def chunk_terms(r0):
        lf = lf_scr[pl.ds(r0, c), :]
        bc = cumsum(lf)
        tot = bc[c - 1:c, :] if spc == 1 else _sel_dot(seq_last, bc)
        qq = q_scr[pl.ds(r0, c), :]
        kk = k_scr[pl.ds(r0, c), :]
        v32 = v_scr[pl.ds(r0, c), :]
        vv = mxu(v32)
        qd32 = qq * jnp.exp(bc)
        kt32 = kk * jnp.exp(tot - bc)
        qdj = mxu(qd32)
        ktail = mxu(kt32)
        if nsub == 1:
            kinv = mxu(kk * jnp.exp(-bc))
            scj = [_dot_nt(qdj[:, s], kinv[:, s]) for s in heads]
        else:
            bl = _sel_dot(local_tri, lf)
            qloc = mxu(qq * jnp.exp(bl))
            parts = [[] for _ in heads]
            for i in range(nsub):
                lo, hi = i * HG_SUB, (i + 1) * HG_SUB
                ref_pt = bc[lo - 1:lo, :] if i else 0.0
                kci = mxu(jnp.where(slab_row < hi, kk * jnp.exp(ref_pt - bc), 0.0))
                for h, s in enumerate(heads):
                    parts[h].append(_dot_nt(qloc[lo:hi, s], kci[:, s]))
            scj = [jnp.concatenate(p, axis=0) for p in parts]
        scj = [mxu(jnp.where(tri, sc, 0.0)) for sc in scj]
        oij = [_dot(scj[h], vv[:, heads[h]]) for h in range(H)]
        if spc > 1:
            return qd32, jnp.exp(tot), (v32, kt32), oij
        kvj = [_dot_tn(vv[:, s], ktail[:, s]) for s in heads]
        return qdj, jnp.exp(tot), kvj, oij

    def group(idx, carry):
        qd, dec, kv, oi, r0s, sqs = [], [], [], [], [], []
        for j in range(nu):
            r0 = _chunk_row(idx, j, nu, c)
            qdj, decj, kvj, oij = chunk_terms(r0)
            kv.append(kvj)
            oi.append(oij)
            qd.append(qdj)
            dec.append(decj)
            r0s.append(r0)
            sqs.append(_chunk_seq(idx, j, nu, ncs) if spc == 1 else (idx * nu + j) * spc)

        if spc > 1:
            outs, finals = [], []
            for j in range(nu):
                v32, kt32 = kv[j]
                o_j = []
                for h, s in enumerate(heads):
                    parts = []
                    for q in range(spc):
                        rs = slice(q * sl, (q + 1) * sl)
                        st = st_scr[sqs[j] + q, h]
                        parts.append(oi[j][h][rs] + _dot_nt(qd[j][rs, s], st))
                        finals.append((sqs[j] + q, h,
                                       st * dec[j][q * sl:q * sl + 1, s] + _dot_tn(v32[rs, s], kt32[rs, s])))
                    o_j.append(jnp.concatenate(parts, axis=0))
                outs.append(o_j)
            for j in range(nu):
                q_scr[pl.ds(r0s[j], c), :] = jnp.concatenate(outs[j], axis=1)
            for sq, h, st_new in finals:
                st_scr[sq, h] = st_new
            return carry

        own_state = ncs == 1
        loaded = [[st_scr[sqs[j], h] for h in range(H)] for j in range(nu if own_state else 1)]
        outs, finals = [], []
        st = loaded[0]
        for j in range(nu):
            if own_state:
                st = loaded[j]
            outs.append([oi[j][h] + _dot_nt(qd[j][:, heads[h]], mxu(st[h])) for h in range(H)])
            st = [st[h] * dec[j][:, heads[h]] + kv[j][h] for h in range(H)]
            if own_state or j == nu - 1:
                finals.append((sqs[j], st))
        for j in range(nu):
            q_scr[pl.ds(r0s[j], c), :] = jnp.concatenate(outs[j], axis=1)
        for sq, st in finals:
            for h in range(H):
                st_scr[sq, h] = st[h]
        return carry

    _run_groups(group, rows // (nu * c))

    on = _head_rms_gate(q_scr[...], gn_ref[...], g_scr[...])
    o_ref[...] = x + _dot(on.astype(BF16), wo_ref[...])

    @pl.when(t == pl.num_programs(1) - 1)
    def _():
        for b in range(bb):
            for h in range(H):
                sout_ref[b, h] = st_scr[b, h].T


def _hgrn_call(x2d, s0, nw, win, lbnd, gn, wo, *, nseq, seqlen, bb, tt, c, nu=4, row0=0, bcast_state=False):
    assert bb == 1 or tt == seqlen
    assert nseq % bb == 0 and seqlen % tt == 0 and row0 % (bb * tt) == 0
    assert tt % c == 0 or (c % tt == 0 and (bb * tt) % c == 0)
    rows = bb * tt
    nu = min(nu, rows // c)
    nt = seqlen // tt
    blk0 = row0 // rows
    s_idx = (lambda b, t: (0, 0, 0, 0)) if bcast_state else (lambda b, t: (b, 0, 0, 0))
    slab = pltpu.VMEM((rows, D), F32)
    return pl.pallas_call(
        functools.partial(_hgrn_kernel, bb=bb, tt=tt, c=c, nu=nu),
        grid=(nseq // bb, nt),
        in_specs=[
            pl.BlockSpec((rows, D), lambda b, t: (blk0 + b * nt + t, 0)),
            pl.BlockSpec((bb, H, HD, HD), s_idx),
            _const_spec((1, D)),
            _const_spec((D, 4 * D)),
            _const_spec(lbnd.shape),
            _const_spec((1, HD)),
            _const_spec((D, D)),
        ],
        out_specs=[
            pl.BlockSpec((rows, D), lambda b, t: (b * nt + t, 0)),
            pl.BlockSpec((bb, H, HD, HD), lambda b, t: (b, 0, 0, 0)),
        ],
        out_shape=[
            jax.ShapeDtypeStruct((nseq * seqlen, D), F32),
            jax.ShapeDtypeStruct((nseq, H, HD, HD), F32),
        ],
        scratch_shapes=[pltpu.VMEM((bb, H, HD, HD), F32), slab, slab, slab, slab, slab],
        compiler_params=pltpu.CompilerParams(
            dimension_semantics=("arbitrary", "arbitrary"), vmem_limit_bytes=V7X_VMEM_LIMIT),
        name=f"hgrn_mixer_{nseq}x{seqlen}",
    )(x2d, s0, nw, win, lbnd, gn, wo)


def _neumann_inv(ls, eye, levels, mxu):
    xs = [eye - l for l in ls]
    ps = [mxu(l) for l in ls]
    for _ in range(levels - 1):
        ps = [mxu(_dot(p, p)) for p in ps]
        xs = [x + _dot(mxu(x), p) for x, p in zip(xs, ps)]
    return xs


def _unit_lower_inv(ls, c, mxu, span=None):
    span = c if span is None else span
    row = _iota2((c, c), 0)
    col = _iota2((c, c), 1)
    eye = (row == col).astype(F32)
    if span <= 16:
        return _neumann_inv(ls, eye, span.bit_length() - 1, mxu)
    sub = 16
    nblk = c // sub
    assert c == nblk * sub and nblk & (nblk - 1) == 0
    diag = (row // sub) == (col // sub)
    lds = [jnp.where(diag, l, 0.0) for l in ls]
    dinvs = [mxu(d) for d in _neumann_inv(lds, eye, 4, mxu)]
    ms = [_dot(d, mxu(l - ld)) for d, l, ld in zip(dinvs, ls, lds)]
    ys = _neumann_inv(ms, eye, nblk.bit_length() - 1, mxu)
    return [_dot(mxu(y), d) for y, d in zip(ys, dinvs)]


def _gdn_kernel(x_ref, s0_ref, c0_ref, nw_ref, win_ref, cw_ref, par_ref, gn_ref, wo_ref,
                o_ref, sout_ref, cout_ref,
                s_scr, xp_scr, act_scr, z_scr, la_scr, be_scr, o_scr, *, bb, tt, c, nu):
    t = pl.program_id(1)
    rows = bb * tt
    sl = min(c, tt)
    spc = c // sl
    ncs = max(tt // c, 1)
    assert c % sl == 0 and (ncs == 1 or ncs % nu == 0)

    @pl.when(t == 0)
    def _():
        s_scr[...] = s0_ref[...]
        for b in range(bb):
            xp_scr[b, 8 - (CONV - 1):8, :] = c0_ref[b]

    x = x_ref[...]
    xn = _rms(x, nw_ref[...]).astype(BF16)
    pre = _dot(xn, win_ref[:, 0:QKV])
    for b in range(bb):
        xp_scr[b, 8:8 + tt, :] = pre[b * tt:(b + 1) * tt]
    z_scr[...] = _silu(_dot(xn, win_ref[:, QKV:QKV + D]))
    ba = _dot(xn, win_ref[:, QKV + D:QKV + D + LANES])
    par = par_ref[...]
    sp = ba + par[0:1]
    softplus = jnp.maximum(sp, 0.0) + jnp.log(1.0 + jnp.exp(-jnp.abs(sp)))
    la_scr[...] = -jnp.exp(par[1:2]) * softplus
    be_scr[...] = jax.nn.sigmoid(ba)

    cw = cw_ref[...]
    for b in range(bb):
        conv = xp_scr[b, 8 - (CONV - 1):8 - (CONV - 1) + tt, :] * cw[0:1]
        for w in range(1, CONV):
            conv = conv + xp_scr[b, 8 - (CONV - 1) + w:8 - (CONV - 1) + w + tt, :] * cw[w:w + 1]
        act = _silu(conv)
        for j in range(2 * H):
            a = act[:, j * HD:(j + 1) * HD]
            scale = HD ** -0.5 if j < H else 1.0
            act_scr[b * tt:(b + 1) * tt, j * HD:(j + 1) * HD] = (
                a * lax.rsqrt(jnp.sum(a * a, axis=-1, keepdims=True) + EPS) * scale)
        act_scr[b * tt:(b + 1) * tt, 2 * H * HD:] = act[:, 2 * H * HD:]
        tail = xp_scr[b, tt:tt + 8, :]
        xp_scr[b, 0:8, :] = tail

    row = _iota2((c, c), 0)
    col = _iota2((c, c), 1)
    same_seq = (row // sl) == (col // sl)
    incl = (row >= col) & same_seq
    strict = (row > col) & same_seq
    cumsum, mxu = _chunk_ops(incl, c)
    if c % 16 == 0:
        eye_l = (_iota2((LANES, LANES), 0) == _iota2((LANES, LANES), 1)).astype(BF16)
        transpose = lambda m: _sel_dot_nt(eye_l, m)
    else:
        eye_l = (_iota2((LANES, LANES), 0) == _iota2((LANES, LANES), 1)).astype(F32)
        transpose = lambda m: _dot_nt(eye_l, m, HI)
    if spc > 1:
        seq_last = (col == (row // sl) * sl + (sl - 1)).astype(BF16)
    narrow = mxu if spc == 1 else (lambda a: a)

    heads = [slice(h * HD, (h + 1) * HD) for h in range(H)]

    def group(idx, carry):
        qd, ktail, dtot, lmat, amat, rwu, r0s, sqs = [], [], [], [], [], [], [], []
        for j in range(nu):
            r0 = _chunk_row(idx, j, nu, c)
            gc = cumsum(la_scr[pl.ds(r0, c), :])
            gt = transpose(gc)
            be = be_scr[pl.ds(r0, c), :]
            eg = jnp.exp(gc)
            gl = gc[c - 1:c, :] if spc == 1 else _sel_dot(seq_last, gc)
            et = jnp.exp(gl - gc)
            dt = jnp.exp(gl)
            qq = act_scr[pl.ds(r0, c), 0:H * HD]
            kk = act_scr[pl.ds(r0, c), H * HD:2 * H * HD]
            vv = act_scr[pl.ds(r0, c), 2 * H * HD:]
            qd_j, ktail_j, dtot_j = [], [], []
            for h, s in enumerate(heads):
                g_col = gc[:, 8 + h:9 + h]
                eg_col = eg[:, 8 + h:9 + h]
                be_col = be[:, h:h + 1]
                kh = mxu(kk[:, s])
                qk = _dot_nt(jnp.concatenate([mxu(qq[:, s]), kh], axis=0), kh)
                ed = jnp.exp(g_col - gt[8 + h:9 + h, :])
                lmat.append(be_col * qk[c:] * jnp.where(strict, ed, 0.0))
                amat.append(mxu(qk[:c] * jnp.where(incl, ed, 0.0)))
                rwu.append(mxu(jnp.concatenate([kk[:, s] * (be_col * eg_col), vv[:, s] * be_col], axis=1)))
                qd_j.append(narrow(qq[:, s] * eg_col))
                ktail_j.append(narrow(kk[:, s] * et[:, 8 + h:9 + h]))
                dtot_j.append(dt[:, 8 + h:9 + h])
            qd.append(qd_j)
            ktail.append(ktail_j)
            dtot.append(dtot_j)
            r0s.append(r0)
            sqs.append(_chunk_seq(idx, j, nu, ncs) if spc == 1 else (idx * nu + j) * spc)
        tinv = _unit_lower_inv(lmat, c, mxu, span=sl)
        wu = [_dot(mxu(t_), r_) for t_, r_ in zip(tinv, rwu)]

        if spc > 1:
            outs, finals = [], []
            for j in range(nu):
                o_j = []
                for h in range(H):
                    wuh = wu[j * H + h]
                    us, oins = [], []
                    for q in range(spc):
                        rs = slice(q * sl, (q + 1) * sl)
                        s_old = s_scr[sqs[j] + q, h]
                        qw = _dot(mxu(jnp.concatenate([qd[j][h][rs], wuh[rs, :HD]], axis=0)), mxu(s_old))
                        u = wuh[rs, HD:] - qw[sl:]
                        us.append(u)
                        oins.append(qw[:sl])
                        finals.append((sqs[j] + q, h,
                                       s_old * dtot[j][h][q * sl:q * sl + 1] + _dot_tn(ktail[j][h][rs], u)))
                    o_j.append(jnp.concatenate(oins, axis=0)
                               + _dot(amat[j * H + h], mxu(jnp.concatenate(us, axis=0))))
                outs.append(o_j)
            for j in range(nu):
                o_scr[pl.ds(r0s[j], c), :] = jnp.concatenate(outs[j], axis=1)
            for sq, h, s_new in finals:
                s_scr[sq, h] = s_new
            return carry

        own_state = ncs == 1
        loaded = [[s_scr[sqs[j], h] for h in range(H)] for j in range(nu if own_state else 1)]
        outs, finals = [], []
        st = loaded[0]
        for j in range(nu):
            if own_state:
                st = loaded[j]
            o_j, st_next = [], []
            for h in range(H):
                wuh = wu[j * H + h]
                qw = _dot(jnp.concatenate([qd[j][h], mxu(wuh[:, :HD])], axis=0), mxu(st[h]))
                u = wuh[:, HD:] - qw[c:]
                ub = mxu(u)
                o_j.append(qw[:c] + _dot(amat[j * H + h], ub))
                st_next.append(st[h] * dtot[j][h] + _dot_tn(ktail[j][h], ub))
            st = st_next
            outs.append(o_j)
            if own_state or j == nu - 1:
                finals.append((sqs[j], st))
        for j in range(nu):
            o_scr[pl.ds(r0s[j], c), :] = jnp.concatenate(outs[j], axis=1)
        for sq, st in finals:
            for h in range(H):
                s_scr[sq, h] = st[h]
        return carry

    _run_groups(group, rows // (nu * c))

    on = _head_rms_gate(o_scr[...], gn_ref[...], z_scr[...])
    o_ref[...] = x + _dot(on.astype(BF16), wo_ref[...])

    @pl.when(t == pl.num_programs(1) - 1)
    def _():
        sout_ref[...] = s_scr[...]
        for b in range(bb):
            cout_ref[b] = xp_scr[b, 8 - (CONV - 1):8, :]


def _gdn_call(x2d, s0, c0, nw, win, cw, par, gn, wo, *, nseq, seqlen, bb, tt, c, nu=2, row0=0,
              bcast_state=False):
    assert bb == 1 or tt == seqlen
    assert nseq % bb == 0 and seqlen % tt == 0 and tt % 8 == 0 and row0 % (bb * tt) == 0
    assert tt % c == 0 or (c % tt == 0 and (bb * tt) % c == 0)
    rows = bb * tt
    nt = seqlen // tt
    blk0 = row0 // rows
    s_idx = (lambda b, t: (0, 0, 0, 0)) if bcast_state else (lambda b, t: (b, 0, 0, 0))
    c_idx = (lambda b, t: (0, 0, 0)) if bcast_state else (lambda b, t: (b, 0, 0))
    slab = pltpu.VMEM((rows, D), F32)
    return pl.pallas_call(
        functools.partial(_gdn_kernel, bb=bb, tt=tt, c=c, nu=min(nu, bb * tt // c)),
        grid=(nseq // bb, nt),
        in_specs=[
            pl.BlockSpec((rows, D), lambda b, t: (blk0 + b * nt + t, 0)),
            pl.BlockSpec((bb, H, HD, HD), s_idx),
            pl.BlockSpec((bb, CONV - 1, QKV), c_idx),
            _const_spec((1, D)),
            _const_spec(win.shape),
            _const_spec((CONV, QKV)),
            _const_spec((2, LANES)),
            _const_spec((1, HD)),
            _const_spec((D, D)),
        ],
        out_specs=[
            pl.BlockSpec((rows, D), lambda b, t: (b * nt + t, 0)),
            pl.BlockSpec((bb, H, HD, HD), lambda b, t: (b, 0, 0, 0)),
            pl.BlockSpec((bb, CONV - 1, QKV), lambda b, t: (b, 0, 0)),
        ],
        out_shape=[
            jax.ShapeDtypeStruct((nseq * seqlen, D), F32),
            jax.ShapeDtypeStruct((nseq, H, HD, HD), F32),
            jax.ShapeDtypeStruct((nseq, CONV - 1, QKV), F32),
        ],
        scratch_shapes=[
            pltpu.VMEM((bb, H, HD, HD), F32),
            pltpu.VMEM((bb, tt + 8, QKV), F32),
            pltpu.VMEM((rows, QKV), F32),
            slab,
            pltpu.VMEM((rows, LANES), F32),
            pltpu.VMEM((rows, LANES), F32),
            slab,
        ],
        compiler_params=pltpu.CompilerParams(
            dimension_semantics=("arbitrary", "arbitrary"), vmem_limit_bytes=V7X_VMEM_LIMIT),
        name=f"gdn_mixer_{nseq}x{seqlen}",
    )(x2d, s0, c0, nw, win, cw, par, gn, wo)


def _swiglu(xb, w1_ref, w3_ref, w2_ref):
    acc = None
    for j in range(DFF // FB):
        fs = slice(j * FB, (j + 1) * FB)
        hh = (_silu(_dot(xb, w1_ref[:, fs])) * _dot(xb, w3_ref[:, fs])).astype(BF16)
        part = _dot(hh, w2_ref[fs, :])
        acc = part if acc is None else acc + part
    return acc


def _ffn_kernel(*refs, n_in, nblk_first):
    x_refs = refs[:n_in]
    nw_ref, w1_ref, w3_ref, w2_ref, o_ref = refs[n_in:]
    x = x_refs[0][...]
    if n_in == 2:
        x = jnp.where(pl.program_id(0) < nblk_first, x, x_refs[1][...])
    xb = _rms(x, nw_ref[...]).astype(BF16)
    o_ref[...] = x + _swiglu(xb, w1_ref, w3_ref, w2_ref)


def _two_group_specs(tm, nblk_first, nblk_total):
    last_first = nblk_first - 1
    return [
        pl.BlockSpec((tm, D), lambda i, *_: (jnp.minimum(i, last_first), 0)),
        pl.BlockSpec((tm, D), lambda i, *_: (jnp.maximum(i - nblk_first, 0), 0)),
    ]


def _ffn_call(xs, nw, w1, w3, w2, *, tm):
    nblks = [x.shape[0] // tm for x in xs]
    assert all(x.shape[0] % tm == 0 for x in xs)
    total = sum(nblks)
    if len(xs) == 2:
        x_specs = _two_group_specs(tm, nblks[0], total)
    else:
        x_specs = [pl.BlockSpec((tm, D), lambda i: (i, 0))]
    return pl.pallas_call(
        functools.partial(_ffn_kernel, n_in=len(xs), nblk_first=nblks[0]),
        grid=(total,),
        in_specs=x_specs + [_const_spec((1, D)), _const_spec((D, DFF)), _const_spec((D, DFF)), _const_spec((DFF, D))],
        out_specs=pl.BlockSpec((tm, D), lambda i: (i, 0)),
        out_shape=jax.ShapeDtypeStruct((total * tm, D), F32),
        compiler_params=pltpu.CompilerParams(dimension_semantics=("arbitrary",), vmem_limit_bytes=V7X_VMEM_LIMIT),
        name=f"ffn_dense_{total * tm}",
    )(*xs, nw, w1, w3, w2)


def _router_kernel(xa_ref, xb_ref, nw_ref, wr_ref, gate_ref, lpos_ref, lpost_ref, tab_ref, tot_ref, seg_scr, *,
                   tm, nblk_first):
    i = pl.program_id(0)

    @pl.when(i == 0)
    def _():
        seg_scr[...] = jnp.zeros_like(seg_scr)

    x = jnp.where(i < nblk_first, xa_ref[...], xb_ref[...])
    xn = _rms(x, nw_ref[...])
    logits = _dot_3x(xn, wr_ref[...])
    lane = _iota2((tm, LANES), 1).astype(F32)
    neg = jnp.float32(-jnp.inf)
    l1 = jnp.where(lane < NE, logits, neg)
    m1 = jnp.max(l1, axis=-1, keepdims=True)
    i1 = jnp.min(jnp.where(l1 == m1, lane, float(LANES)), axis=-1, keepdims=True)
    l2 = jnp.where(lane == i1, neg, l1)
    m2 = jnp.max(l2, axis=-1, keepdims=True)
    i2 = jnp.min(jnp.where(l2 == m2, lane, float(LANES)), axis=-1, keepdims=True)
    e2 = jnp.exp(m2 - m1)
    g1 = 1.0 / (1.0 + e2)
    g2 = e2 / (1.0 + e2)

    oh1 = (lane == i1).astype(F32)
    oh2 = (lane == i2).astype(F32)
    below = (_iota2((tm, tm), 0) > _iota2((tm, tm), 1)).astype(BF16)
    c1 = _dot(below, oh1.astype(BF16))
    c2 = _dot(below, oh2.astype(BF16))
    n1 = jnp.sum(oh1, axis=0, keepdims=True)
    n2 = jnp.sum(oh2, axis=0, keepdims=True)
    cnt = jnp.floor((n1 + n2 + (SEG - 1)) * (1.0 / SEG)) * SEG
    before = (_iota2((LANES, LANES), 0) < _iota2((LANES, LANES), 1)).astype(BF16)
    start = _dot(jnp.broadcast_to(cnt, (16, LANES)).astype(BF16), before)[0:1]
    p1 = jnp.sum(oh1 * (start + c1), axis=-1, keepdims=True)
    p2 = jnp.sum(oh2 * (start + n1 + c2), axis=-1, keepdims=True)
    used = seg_scr[...]
    seg_scr[...] = used + cnt
    tot_ref[...] = used + cnt

    gate_ref[:, 0:1] = g1
    gate_ref[:, 1:2] = g2
    lpos_ref[:, 0:1] = p1
    lpos_ref[:, 1:2] = p2
    sub = _iota2((8, LANES), 0)
    tab_ref[...] = jnp.where(sub == 0, cnt, jnp.where(sub == 1, start, jnp.where(sub == 2, used, 0.0)))
    slab = jnp.where(lane == 0.0, p1, jnp.where(lane == 1.0, p2, 0.0))
    eye16 = (_iota2((16, LANES), 0) == _iota2((16, LANES), 1)).astype(BF16)
    lpost_ref[...] = _sel_dot_nt(eye16, slab)[0:8]


def _router_call(xa, xb, nw, wr, *, tm):
    na, nb = xa.shape[0] // tm, xb.shape[0] // tm
    nblk = na + nb
    n = nblk * tm
    return pl.pallas_call(
        functools.partial(_router_kernel, tm=tm, nblk_first=na),
        grid=(nblk,),
        in_specs=_two_group_specs(tm, na, nblk) + [_const_spec((1, D)), _const_spec((D, LANES))],
        out_specs=[
            pl.BlockSpec((tm, 2), lambda i: (i, 0)),
            pl.BlockSpec((tm, 2), lambda i: (i, 0)),
            pl.BlockSpec((8, tm), lambda i: (i, 0)),
            pl.BlockSpec((8, LANES), lambda i: (i, 0)),
            pl.BlockSpec((1, LANES), lambda i: (0, 0)),
        ],
        out_shape=[
            jax.ShapeDtypeStruct((n, 2), F32),
            jax.ShapeDtypeStruct((n, 2), F32),
            jax.ShapeDtypeStruct((nblk * 8, tm), F32),
            jax.ShapeDtypeStruct((nblk * 8, LANES), F32),
            jax.ShapeDtypeStruct((1, LANES), F32),
        ],
        scratch_shapes=[pltpu.VMEM((1, LANES), F32)],
        compiler_params=pltpu.CompilerParams(dimension_semantics=("arbitrary",), vmem_limit_bytes=V7X_VMEM_LIMIT),
        name="moe_router",
    )(xa, xb, nw, wr)


def _local_rows(tm):
    return -(-(2 * tm + NE * (SEG - 1)) // LANES) * LANES


def _segment_copies(tab_ref, tile, local_ref, global_ref, sem, *, to_global, wait):
    for e in range(NE):
        base = (tile * NE + e) * 3
        npiece, lstart, gstart = tab_ref[base], tab_ref[base + 1], tab_ref[base + 2]

        def piece(k, carry, lstart=lstart, gstart=gstart):
            loc = local_ref.at[pl.ds(pl.multiple_of(lstart + k * SEG, SEG), SEG)]
            glo = global_ref.at[pl.ds(pl.multiple_of(gstart + k * SEG, SEG), SEG)]
            cp = pltpu.make_async_copy(loc, glo, sem) if to_global else pltpu.make_async_copy(glo, loc, sem)
            if wait:
                cp.wait()
            else:
                cp.start()
            return carry

        lax.fori_loop(0, npiece, piece, 0)


def _dispatch_kernel(tab_ref, zst_ref, xa_ref, xb_ref, lpost_ref, xs_ref, buf, zero_scr, sem, zsem, *,
                     tm, nblk_first, lr):
    i = pl.program_id(0)
    slot = lax.rem(i, 2)

    @pl.when(i == 0)
    def _():
        zero_scr[...] = jnp.zeros_like(zero_scr)

        def zero_copy(e):
            start = pl.multiple_of(zst_ref[e], TM)
            return pltpu.make_async_copy(zero_scr, xs_ref.at[pl.ds(start, TM)], zsem)

        for e in range(zst_ref.shape[0]):
            pl.when(zst_ref[e] >= 0)(lambda e=e: zero_copy(e).start())
        for e in range(zst_ref.shape[0]):
            pl.when(zst_ref[e] >= 0)(lambda e=e: zero_copy(e).wait())

    x = jnp.where(i < nblk_first, xa_ref[...], xb_ref[...])
    lpt = lpost_ref[...]
    rio = _iota2((lr, tm), 0).astype(F32)
    sel = jnp.where(rio == lpt[0:1, :], 1.0, jnp.where(rio == lpt[1:2, :], 1.0, 0.0)).astype(BF16)
    buf[slot] = _sel_dot2(sel, x)

    copies = functools.partial(_segment_copies, tab_ref, global_ref=xs_ref, to_global=True)
    copies(i, local_ref=buf.at[slot], sem=sem.at[slot], wait=False)
    pl.when(i > 0)(lambda: copies(i - 1, local_ref=buf.at[1 - slot], sem=sem.at[1 - slot], wait=True))
    pl.when(i == pl.num_programs(0) - 1)(
        lambda: copies(i, local_ref=buf.at[slot], sem=sem.at[slot], wait=True))


def _dispatch_call(tab, zst, xa, xb, lpost, *, tm, rmax):
    na, nb = xa.shape[0] // tm, xb.shape[0] // tm
    lr = _local_rows(tm)
    return pl.pallas_call(
        functools.partial(_dispatch_kernel, tm=tm, nblk_first=na, lr=lr),
        grid_spec=pltpu.PrefetchScalarGridSpec(
            num_scalar_prefetch=2,
            grid=(na + nb,),
            in_specs=_two_group_specs(tm, na, na + nb) + [pl.BlockSpec((8, tm), lambda i, *_: (i, 0))],
            out_specs=pl.BlockSpec(memory_space=pl.ANY),
            scratch_shapes=[pltpu.VMEM((2, lr, D), F32), pltpu.VMEM((TM, D), F32),
                            pltpu.SemaphoreType.DMA((2,)), pltpu.SemaphoreType.DMA],
        ),
        out_shape=jax.ShapeDtypeStruct((rmax, D), F32),
        compiler_params=pltpu.CompilerParams(dimension_semantics=("arbitrary",), vmem_limit_bytes=V7X_VMEM_LIMIT),
        name="moe_dispatch",
    )(tab, zst, xa, xb, lpost)


def _expert_kernel(texp_ref, tsrc_ref, nact_ref, xs_ref, nw_ref, w1_ref, w3_ref, w2_ref, ys_ref):
    i = pl.program_id(0)

    @pl.when(i < nact_ref[0])
    def _():
        xb = _rms(xs_ref[...], nw_ref[...]).astype(BF16)
        ys_ref[...] = _swiglu(xb, w1_ref, w3_ref, w2_ref)

    @pl.when(i >= nact_ref[0])
    def _():
        ys_ref[...] = jnp.zeros_like(ys_ref)


def _expert_call(texp, tsrc, nact, xs, nw, w1, w3, w2):
    ntile = xs.shape[0] // TM
    return pl.pallas_call(
        _expert_kernel,
        grid_spec=pltpu.PrefetchScalarGridSpec(
            num_scalar_prefetch=3,
            grid=(ntile,),
            in_specs=[
                pl.BlockSpec((TM, D), lambda i, te, ts, na: (ts[i], 0)),
                pl.BlockSpec((1, D), lambda i, te, ts, na: (0, 0)),
                pl.BlockSpec((None, D, DFF), lambda i, te, ts, na: (te[i], 0, 0)),
                pl.BlockSpec((None, D, DFF), lambda i, te, ts, na: (te[i], 0, 0)),
                pl.BlockSpec((None, DFF, D), lambda i, te, ts, na: (te[i], 0, 0)),
            ],
            out_specs=pl.BlockSpec((TM, D), lambda i, te, ts, na: (i, 0)),
        ),
        out_shape=jax.ShapeDtypeStruct((xs.shape[0], D), F32),
        compiler_params=pltpu.CompilerParams(dimension_semantics=("arbitrary",), vmem_limit_bytes=V7X_VMEM_LIMIT),
        name="moe_experts",
    )(texp, tsrc, nact, xs, nw, w1, w3, w2)


def _combine_kernel(tab_ref, x_ref, gate_ref, lpos_ref, nf_ref, ys_ref, o_ref, buf, sem, *, tm, blk0, lr):
    i = pl.program_id(0)
    slot = lax.rem(i, 2)

    def fetch(step, slot_):
        tile = blk0 + step
        last = (tile * NE + NE - 1) * 3
        used = pl.multiple_of(tab_ref[last + 1] + tab_ref[last] * SEG, SEG)

        def clear(k, carry):
            buf[slot_, pl.ds(pl.multiple_of(used + k * SEG, SEG), SEG), :] = jnp.zeros((SEG, D), F32)
            return carry

        lax.fori_loop(0, (lr - used) // SEG, clear, 0)
        _segment_copies(tab_ref, tile, buf.at[slot_], ys_ref, sem.at[slot_], to_global=False, wait=False)

    pl.when(i == 0)(lambda: fetch(i, slot))
    pl.when(i + 1 < pl.num_programs(0))(lambda: fetch(i + 1, 1 - slot))
    _segment_copies(tab_ref, blk0 + i, buf.at[slot], ys_ref, sem.at[slot], to_global=False, wait=True)

    y = buf[slot]
    y_hi = y.astype(BF16)
    y_lo = (y - y_hi.astype(F32)).astype(BF16)
    lp = lpos_ref[...]
    cio = _iota2((tm, lr), 1).astype(F32)
    g = gate_ref[...]
    moe = None
    for s in range(2):
        sel = jnp.where(cio == lp[:, s:s + 1], 1.0, 0.0).astype(BF16)
        picked = _dot(sel, y_hi) + _dot(sel, y_lo)
        moe = g[:, s:s + 1] * picked if moe is None else moe + g[:, s:s + 1] * picked
    o_ref[...] = _rms(x_ref[...] + moe, nf_ref[...])


def _combine_call(tab, x, gate, lpos, nf, ys, *, tm, tok0):
    n = x.shape[0]
    blk0 = tok0 // tm
    lr = _local_rows(tm)
    return pl.pallas_call(
        functools.partial(_combine_kernel, tm=tm, blk0=blk0, lr=lr),
        grid_spec=pltpu.PrefetchScalarGridSpec(
            num_scalar_prefetch=1,
            grid=(n // tm,),
            in_specs=[
                pl.BlockSpec((tm, D), lambda i, *_: (i, 0)),
                pl.BlockSpec((tm, 2), lambda i, *_: (blk0 + i, 0)),
                pl.BlockSpec((tm, 2), lambda i, *_: (blk0 + i, 0)),
                pl.BlockSpec((1, D), lambda i, *_: (0, 0)),
                pl.BlockSpec(memory_space=pl.ANY),
            ],
            out_specs=pl.BlockSpec((tm, D), lambda i, *_: (i, 0)),
            scratch_shapes=[pltpu.VMEM((2, lr, D), F32), pltpu.SemaphoreType.DMA((2,))],
        ),
        out_shape=jax.ShapeDtypeStruct((n, D), F32),
        compiler_params=pltpu.CompilerParams(dimension_semantics=("arbitrary",), vmem_limit_bytes=V7X_VMEM_LIMIT),
        name=f"moe_combine_{n}",
    )(tab, x, gate, lpos, nf, ys)


def kernel(x_prompt, x_sample, state_hgrn, state_gdn, state_gdn_conv, meta_tokens, norm_mix, norm_ffn, norm_final,
           hg_w_in, hg_lower_bounds, hg_g_norm, hg_w_o, gdn_w_in, gdn_conv_w, gdn_a_log, gdn_dt_bias, gdn_g_norm,
           gdn_w_o, ffn_w1, ffn_w3, ffn_w2, moe_w_router, moe_w1, moe_w3, moe_w2):
    bp, tp, _ = x_prompt.shape
    bs, ts, _ = x_sample.shape
    np_, ns_ = bp * tp, bs * ts
    n_tok = np_ + ns_
    row = lambda v: v.reshape(1, -1).astype(F32)

    hg_win = hg_w_in.astype(BF16)
    hg_wo = hg_w_o.astype(BF16)
    gdn_win = jnp.concatenate(
        [gdn_w_in[:, :QKV + D], gdn_w_in[:, QKV + D:], jnp.zeros((D, LANES - 2 * H), gdn_w_in.dtype)],
        axis=1).astype(BF16)
    gdn_wo = gdn_w_o.astype(BF16)
    par = jnp.zeros((2, LANES), F32)
    par = par.at[0, H:2 * H].set(gdn_dt_bias.astype(F32)).at[1, H:2 * H].set(gdn_a_log.astype(F32))
    w1d, w3d, w2d = ffn_w1.astype(BF16), ffn_w3.astype(BF16), ffn_w2.astype(BF16)
    w1e, w3e, w2e = moe_w1.astype(BF16), moe_w3.astype(BF16), moe_w2.astype(BF16)
    wr = jnp.concatenate([moe_w_router.astype(F32), jnp.zeros((D, LANES - NE), F32)], axis=1)
    nm0, nm1 = row(norm_mix[0]), row(norm_mix[1])
    nf0, nf1 = row(norm_ffn[0]), row(norm_ffn[1])
    nfin = row(norm_final)
    hg_gn, gdn_gn = row(hg_g_norm), row(gdn_g_norm)
    lbnd = hg_lower_bounds.astype(F32)
    cw = gdn_conv_w.astype(F32)

    hg_args = (nm0, hg_win, lbnd, hg_gn, hg_wo)
    gdn_args = (nm1, gdn_win, cw, par, gdn_gn, gdn_wo)
    xm = meta_tokens.astype(F32)
    xp2 = x_prompt.reshape(np_, D)
    xs2 = x_sample.reshape(ns_, D)

    zero_state = jnp.zeros((1, H, HD, HD), F32)
    h1m, hg_m = _hgrn_call(xm, zero_state, *hg_args, nseq=1, seqlen=N_META, bb=1, tt=N_META, c=N_META)
    h1p, hg_p = _hgrn_call(xp2, hg_m, *hg_args, nseq=bp, seqlen=tp, bb=1, tt=256, c=64, nu=4, bcast_state=True)
    h1s, hg_s = _hgrn_call(xs2, state_hgrn.astype(F32), *hg_args, nseq=bs, seqlen=ts, bb=8, tt=ts, c=64, nu=1)
    h2m = _ffn_call([h1m], nf0, w1d, w3d, w2d, tm=N_META)
    h2 = _ffn_call([h1p, h1s], nf0, w1d, w3d, w2d, tm=512)

    zero_conv = jnp.zeros((1, CONV - 1, QKV), F32)
    _, gdn_m, conv_m = _gdn_call(h2m, zero_state, zero_conv, *gdn_args, nseq=1, seqlen=N_META, bb=1, tt=N_META, c=16)
    h3p, gdn_p, conv_p = _gdn_call(h2, gdn_m, conv_m, *gdn_args, nseq=bp, seqlen=tp, bb=1, tt=256, c=128, nu=2,
                                   bcast_state=True)
    h3s, gdn_s, conv_s = _gdn_call(h2, state_gdn.astype(F32), state_gdn_conv.astype(F32), *gdn_args,
                                   nseq=bs, seqlen=ts, bb=8, tt=ts, c=64, nu=1, row0=np_)

    yp, ysm = _moe_final(h3p, h3s, nf1, wr, w1e, w3e, w2e, nfin, tm=256)

    sd, gd, cd = state_hgrn.dtype, state_gdn.dtype, state_gdn_conv.dtype
    return (yp.reshape(bp, tp, D), ysm.reshape(bs, ts, D),
            hg_p.astype(sd), hg_s.astype(sd),
            gdn_p.astype(gd), gdn_s.astype(gd), conv_p.astype(cd), conv_s.astype(cd))


def _moe_final(h3p, h3s, nf1, wr, w1e, w3e, w2e, nfin, *, tm):
    np_ = h3p.shape[0]
    n_tok = np_ + h3s.shape[0]
    nblk = n_tok // tm
    gate, lpos, lpost, tab, tot = _router_call(h3p, h3s, nf1, wr, tm=tm)
    totals = tot[0, :NE].astype(jnp.int32)
    padded = ((totals + TM - 1) // TM) * TM
    ends = jnp.cumsum(padded)
    offs = ends - padded
    tab3 = tab.reshape(nblk, 8, LANES)[:, :3, :NE].astype(jnp.int32)
    seg_tab = jnp.stack([tab3[:, 0] // SEG, tab3[:, 1], tab3[:, 2] + offs[None, :]], axis=-1).reshape(-1)
    ntile = -(-(2 * n_tok + nblk * NE * (SEG - 1) + NE * (TM - 1)) // TM)
    nact = ends[-1] // TM
    tiles = jnp.arange(ntile, dtype=jnp.int32)
    texp_all = jnp.minimum(jnp.sum(tiles[:, None] * TM >= ends[None, :], axis=1), NE - 1).astype(jnp.int32)
    last = jnp.maximum(nact - 1, 0)
    active = tiles < nact
    texp = jnp.where(active, texp_all, texp_all[last]).astype(jnp.int32)
    tsrc = jnp.where(active, tiles, last).astype(jnp.int32)
    tail = nact + jnp.arange(ntile - 2 * n_tok // TM)
    zst = jnp.concatenate([jnp.where(padded > 0, ends - TM, -1),
                           jnp.where(tail < ntile, tail * TM, -1)]).astype(jnp.int32)
    xs_sorted = _dispatch_call(seg_tab, zst, h3p, h3s, lpost, tm=tm, rmax=ntile * TM)
    ys = _expert_call(texp, tsrc, nact.reshape(1).astype(jnp.int32), xs_sorted, nf1, w1e, w3e, w2e)
    yp = _combine_call(seg_tab, h3p, gate, lpos, nfin, ys, tm=tm, tok0=0)
    ysm = _combine_call(seg_tab, h3s, gate, lpos, nfin, ys, tm=tm, tok0=np_)
    return yp, ysm
```

```python
import functools

import jax
import jax.numpy as jnp
from jax import lax
from jax.experimental import pallas as pl
from jax.experimental.pallas import tpu as pltpu

D = 1024
H = 8
HD = 128
QKV = 3 * H * HD
CONV = 4
DFF = 3584
NE = 8
N_META = 16
EPS = 1e-6
LANES = 128
FB = 512
TM = 512
HG_SUB = 16
SEG = 8
V7X_VMEM_LIMIT = 56 * 1024 * 1024

F32 = jnp.float32
BF16 = jnp.bfloat16
HI = lax.Precision.HIGHEST


def _dot(a, b, precision=None):
    return jnp.dot(a, b, preferred_element_type=F32, precision=precision)


def _dot_nt(a, b, precision=None):
    return lax.dot_general(a, b, (((1,), (1,)), ((), ())), preferred_element_type=F32, precision=precision)


def _dot_tn(a, b, precision=None):
    return lax.dot_general(a, b, (((0,), (0,)), ((), ())), preferred_element_type=F32, precision=precision)


def _split3(x):
    x1 = x.astype(BF16)
    r1 = x - x1.astype(F32)
    x2 = r1.astype(BF16)
    x3 = (r1 - x2.astype(F32)).astype(BF16)
    return x1, x2, x3


def _sel_dot(sel, x):
    x1, x2, x3 = _split3(x)
    return _dot(sel, x1) + _dot(sel, x2) + _dot(sel, x3)


def _sel_dot2(sel, x):
    hi = x.astype(BF16)
    lo = (x - hi.astype(F32)).astype(BF16)
    return _dot(sel, hi) + _dot(sel, lo)


def _dot_3x(a, b):
    a_hi = a.astype(BF16)
    a_lo = (a - a_hi.astype(F32)).astype(BF16)
    b_hi = b.astype(BF16)
    b_lo = (b - b_hi.astype(F32)).astype(BF16)
    return _dot(a_hi, b_hi) + (_dot(a_hi, b_lo) + _dot(a_lo, b_hi))


def _sel_dot_nt(sel, x):
    x1, x2, x3 = _split3(x)
    return _dot_nt(sel, x1) + _dot_nt(sel, x2) + _dot_nt(sel, x3)


def _rms(x, g):
    return x * lax.rsqrt(jnp.mean(x * x, axis=-1, keepdims=True) + EPS) * g


def _silu(x):
    return x * jax.nn.sigmoid(x)


def _iota2(shape, dim):
    return lax.broadcasted_iota(jnp.int32, shape, dim)


def _head_rms_gate(o, g, gate):
    parts = []
    for h in range(H):
        oh = o[:, h * HD:(h + 1) * HD]
        parts.append(_rms(oh, g))
    return jnp.concatenate(parts, axis=1) * gate


def _chunk_row(idx, j, nu, c):
    if isinstance(idx, int):
        return (idx * nu + j) * c
    return pl.multiple_of((idx * nu + j) * c, c)


def _chunk_seq(idx, j, nu, ncs):
    return (idx * nu + j) // ncs


def _chunk_ops(tri, c):
    if c % 16 == 0:
        tri_b = tri.astype(BF16)
        return (lambda x: _sel_dot(tri_b, x)), (lambda x: x.astype(BF16))
    tri_f = tri.astype(F32)
    return (lambda x: _dot(tri_f, x, HI)), (lambda x: x)


def _run_groups(group, n):
    if n == 1:
        group(0, 0)
    else:
        lax.fori_loop(0, n, group, 0)


def _const_spec(shape):
    nd = len(shape)
    return pl.BlockSpec(shape, lambda *_: (0,) * nd, pipeline_mode=pl.Buffered(1))


def _hgrn_kernel(x_ref, s0_ref, nw_ref, win_ref, lb_ref, gn_ref, wo_ref,
                 o_ref, sout_ref,
                 st_scr, q_scr, k_scr, lf_scr, v_scr, g_scr, *, bb, tt, c, nu):
    t = pl.program_id(1)
    rows = bb * tt
    sl = min(c, tt)
    spc = c // sl
    ncs = max(tt // c, 1)
    assert c % sl == 0 and (ncs == 1 or ncs % nu == 0)

    @pl.when(t == 0)
    def _():
        for b in range(bb):
            for h in range(H):
                st_scr[b, h] = s0_ref[b, h].T

    x = x_ref[...]
    xn = _rms(x, nw_ref[...]).astype(BF16)
    lbn = lb_ref[...]
    e = jnp.exp(lbn - jnp.max(lbn, axis=0, keepdims=True))
    lb = e[0:1] / jnp.sum(e, axis=0, keepdims=True)
    q_scr[...] = _silu(_dot(xn, win_ref[:, 0:D])) * HD ** -0.5
    f = lb + (1.0 - lb) * jax.nn.sigmoid(_dot(xn, win_ref[:, D:2 * D]))
    k_scr[...] = 1.0 - f
    lf_scr[...] = jnp.log(f)
    v_scr[...] = _dot(xn, win_ref[:, 2 * D:3 * D])
    g_scr[...] = _silu(_dot(xn, win_ref[:, 3 * D:4 * D]))

    row = _iota2((c, c), 0)
    col = _iota2((c, c), 1)
    tri = (row >= col) & ((row // sl) == (col // sl))
    cumsum, mxu = _chunk_ops(tri, c)
    heads = [slice(h * HD, (h + 1) * HD) for h in range(H)]
    nsub = c // HG_SUB if sl > HG_SUB else 1
    if nsub > 1:
        local_tri = (tri & ((row // HG_SUB) == (col // HG_SUB))).astype(BF16)
        slab_row = _iota2((c, D), 0)
    if spc > 1:
        seq_last = (col == (row // sl) * sl + (sl - 1)).astype(BF16)

    def chunk_terms(r0):
        lf = lf_scr[pl.ds(r0, c), :]
        bc = cumsum(lf)
        tot = bc[c - 1:c, :] if spc == 1 else _sel_dot(seq_last, bc)
        qq = q_scr[pl.ds(r0, c), :]
        kk = k_scr[pl.ds(r0, c), :]
        v32 = v_scr[pl.ds(r0, c), :]
        vv = mxu(v32)
        qd32 = qq * jnp.exp(bc)
        kt32 = kk * jnp.exp(tot - bc)
        qdj = mxu(qd32)
        ktail = mxu(kt32)
        if nsub == 1:
            kinv = mxu(kk * jnp.exp(-bc))
            scj = [_dot_nt(qdj[:, s], kinv[:, s]) for s in heads]
        else:
            bl = _sel_dot(local_tri, lf)
            qloc = mxu(qq * jnp.exp(bl))
            parts = [[] for _ in heads]
            for i in range(nsub):
                lo, hi = i * HG_SUB, (i + 1) * HG_SUB
                ref_pt = bc[lo - 1:lo, :] if i else 0.0
                kci = mxu(jnp.where(slab_row < hi, kk * jnp.exp(ref_pt - bc), 0.0))
                for h, s in enumerate(heads):
                    parts[h].append(_dot_nt(qloc[lo:hi, s], kci[:, s]))
            scj = [jnp.concatenate(p, axis=0) for p in parts]
        scj = [mxu(jnp.where(tri, sc, 0.0)) for sc in scj]
        oij = [_dot(scj[h], vv[:, heads[h]]) for h in range(H)]
        if spc > 1:
            return qd32, jnp.exp(tot), (v32, kt32), oij
        kvj = [_dot_tn(vv[:, s], ktail[:, s]) for s in heads]
        return qdj, jnp.exp(tot), kvj, oij

    def group(idx, carry):
        qd, dec, kv, oi, r0s, sqs = [], [], [], [], [], []
        for j in range(nu):
            r0 = _chunk_row(idx, j, nu, c)
            qdj, decj, kvj, oij = chunk_terms(r0)
            kv.append(kvj)
            oi.append(oij)
            qd.append(qdj)
            dec.append(decj)
            r0s.append(r0)
            sqs.append(_chunk_seq(idx, j, nu, ncs) if spc == 1 else (idx * nu + j) * spc)

        if spc > 1:
            outs, finals = [], []
            for j in range(nu):
                v32, kt32 = kv[j]
                o_j = []
                for h, s in enumerate(heads):
                    parts = []
                    for q in range(spc):
                        rs = slice(q * sl, (q + 1) * sl)
                        st = st_scr[sqs[j] + q, h]
                        parts.append(oi[j][h][rs] + _dot_nt(qd[j][rs, s], st))
                        finals.append((sqs[j] + q, h,
                                       st * dec[j][q * sl:q * sl + 1, s] + _dot_tn(v32[rs, s], kt32[rs, s])))
                    o_j.append(jnp.concatenate(parts, axis=0))
                outs.append(o_j)
            for j in range(nu):
                q_scr[pl.ds(r0s[j], c), :] = jnp.concatenate(outs[j], axis=1)
            for sq, h, st_new in finals:
                st_scr[sq, h] = st_new
            return carry

        own_state = ncs == 1
        loaded = [[st_scr[sqs[j], h] for h in range(H)] for j in range(nu if own_state else 1)]
        outs, finals = [], []
        st = loaded[0]
        for j in range(nu):
            if own_state:
                st = loaded[j]
            outs.append([oi[j][h] + _dot_nt(qd[j][:, heads[h]], mxu(st[h])) for h in range(H)])
            st = [st[h] * dec[j][:, heads[h]] + kv[j][h] for h in range(H)]
            if own_state or j == nu - 1:
                finals.append((sqs[j], st))
        for j in range(nu):
            q_scr[pl.ds(r0s[j], c), :] = jnp.concatenate(outs[j], axis=1)
        for sq, st in finals:
            for h in range(H):
                st_scr[sq, h] = st[h]
        return carry

    _run_groups(group, rows // (nu * c))

    on = _head_rms_gate(q_scr[...], gn_ref[...], g_scr[...])
    o_ref[...] = x + _dot(on.astype(BF16), wo_ref[...])

    @pl.when(t == pl.num_programs(1) - 1)
    def _():
        for b in range(bb):
            for h in range(H):
                sout_ref[b, h] = st_scr[b, h].T


def _hgrn_call(x2d, s0, nw, win, lbnd, gn, wo, *, nseq, seqlen, bb, tt, c, nu=4, row0=0, bcast_state=False):
    assert bb == 1 or tt == seqlen
    assert nseq % bb == 0 and seqlen % tt == 0 and row0 % (bb * tt) == 0
    assert tt % c == 0 or (c % tt == 0 and (bb * tt) % c == 0)
    rows = bb * tt
    nu = min(nu, rows // c)
    nt = seqlen // tt
    blk0 = row0 // rows
    s_idx = (lambda b, t: (0, 0, 0, 0)) if bcast_state else (lambda b, t: (b, 0, 0, 0))
    slab = pltpu.VMEM((rows, D), F32)
    return pl.pallas_call(
        functools.partial(_hgrn_kernel, bb=bb, tt=tt, c=c, nu=nu),
        grid=(nseq // bb, nt),
        in_specs=[
            pl.BlockSpec((rows, D), lambda b, t: (blk0 + b * nt + t, 0)),
            pl.BlockSpec((bb, H, HD, HD), s_idx),
            _const_spec((1, D)),
            _const_spec((D, 4 * D)),
            _const_spec(lbnd.shape),
            _const_spec((1, HD)),
            _const_spec((D, D)),
        ],
        out_specs=[
            pl.BlockSpec((rows, D), lambda b, t: (b * nt + t, 0)),
            pl.BlockSpec((bb, H, HD, HD), lambda b, t: (b, 0, 0, 0)),
        ],
        out_shape=[
            jax.ShapeDtypeStruct((nseq * seqlen, D), F32),
            jax.ShapeDtypeStruct((nseq, H, HD, HD), F32),
        ],
        scratch_shapes=[pltpu.VMEM((bb, H, HD, HD), F32), slab, slab, slab, slab, slab],
        compiler_params=pltpu.CompilerParams(
            dimension_semantics=("arbitrary", "arbitrary"), vmem_limit_bytes=V7X_VMEM_LIMIT),
        name=f"hgrn_mixer_{nseq}x{seqlen}",
    )(x2d, s0, nw, win, lbnd, gn, wo)


def _neumann_inv(ls, eye, levels, mxu):
    xs = [eye - l for l in ls]
    ps = [mxu(l) for l in ls]
    for _ in range(levels - 1):
        ps = [mxu(_dot(p, p)) for p in ps]
        xs = [x + _dot(mxu(x), p) for x, p in zip(xs, ps)]
    return xs


def _unit_lower_inv(ls, c, mxu, span=None):
    span = c if span is None else span
    row = _iota2((c, c), 0)
    col = _iota2((c, c), 1)
    eye = (row == col).astype(F32)
    if span <= 16:
        return _neumann_inv(ls, eye, span.bit_length() - 1, mxu)
    sub = 16
    nblk = c // sub
    assert c == nblk * sub and nblk & (nblk - 1) == 0
    diag = (row // sub) == (col // sub)
    lds = [jnp.where(diag, l, 0.0) for l in ls]
    dinvs = [mxu(d) for d in _neumann_inv(lds, eye, 4, mxu)]
    ms = [_dot(d, mxu(l - ld)) for d, l, ld in zip(dinvs, ls, lds)]
    ys = _neumann_inv(ms, eye, nblk.bit_length() - 1, mxu)
    return [_dot(mxu(y), d) for y, d in zip(ys, dinvs)]


def _gdn_kernel(x_ref, s0_ref, c0_ref, nw_ref, win_ref, cw_ref, par_ref, gn_ref, wo_ref,
                o_ref, sout_ref, cout_ref,
                s_scr, xp_scr, act_scr, z_scr, la_scr, be_scr, o_scr, *, bb, tt, c, nu):
    t = pl.program_id(1)
    rows = bb * tt
    sl = min(c, tt)
    spc = c // sl
    ncs = max(tt // c, 1)
    assert c % sl == 0 and (ncs == 1 or ncs % nu == 0)

    @pl.when(t == 0)
    def _():
        s_scr[...] = s0_ref[...]
        for b in range(bb):
            xp_scr[b, 8 - (CONV - 1):8, :] = c0_ref[b]

    x = x_ref[...]
    xn = _rms(x, nw_ref[...]).astype(BF16)
    pre = _dot(xn, win_ref[:, 0:QKV])
    for b in range(bb):
        xp_scr[b, 8:8 + tt, :] = pre[b * tt:(b + 1) * tt]
    z_scr[...] = _silu(_dot(xn, win_ref[:, QKV:QKV + D]))
    ba = _dot(xn, win_ref[:, QKV + D:QKV + D + LANES])
    par = par_ref[...]
    sp = ba + par[0:1]
    softplus = jnp.maximum(sp, 0.0) + jnp.log(1.0 + jnp.exp(-jnp.abs(sp)))
    la_scr[...] = -jnp.exp(par[1:2]) * softplus
    be_scr[...] = jax.nn.sigmoid(ba)

    cw = cw_ref[...]
    for b in range(bb):
        conv = xp_scr[b, 8 - (CONV - 1):8 - (CONV - 1) + tt, :] * cw[0:1]
        for w in range(1, CONV):
            conv = conv + xp_scr[b, 8 - (CONV - 1) + w:8 - (CONV - 1) + w + tt, :] * cw[w:w + 1]
        act = _silu(conv)
        for j in range(2 * H):
            a = act[:, j * HD:(j + 1) * HD]
            scale = HD ** -0.5 if j < H else 1.0
            act_scr[b * tt:(b + 1) * tt, j * HD:(j + 1) * HD] = (
                a * lax.rsqrt(jnp.sum(a * a, axis=-1, keepdims=True) + EPS) * scale)
        act_scr[b * tt:(b + 1) * tt, 2 * H * HD:] = act[:, 2 * H * HD:]
        tail = xp_scr[b, tt:tt + 8, :]
        xp_scr[b, 0:8, :] = tail

    row = _iota2((c, c), 0)
    col = _iota2((c, c), 1)
    same_seq = (row // sl) == (col // sl)
    incl = (row >= col) & same_seq
    strict = (row > col) & same_seq
    cumsum, mxu = _chunk_ops(incl, c)
    if c % 16 == 0:
        eye_l = (_iota2((LANES, LANES), 0) == _iota2((LANES, LANES), 1)).astype(BF16)
        transpose = lambda m: _sel_dot_nt(eye_l, m)
    else:
        eye_l = (_iota2((LANES, LANES), 0) == _iota2((LANES, LANES), 1)).astype(F32)
        transpose = lambda m: _dot_nt(eye_l, m, HI)
    if spc > 1:
        seq_last = (col == (row // sl) * sl + (sl - 1)).astype(BF16)
    narrow = mxu if spc == 1 else (lambda a: a)

    heads = [slice(h * HD, (h + 1) * HD) for h in range(H)]

    def group(idx, carry):
        qd, ktail, dtot, lmat, amat, rwu, r0s, sqs = [], [], [], [], [], [], [], []
        for j in range(nu):
            r0 = _chunk_row(idx, j, nu, c)
            gc = cumsum(la_scr[pl.ds(r0, c), :])
            gt = transpose(gc)
            be = be_scr[pl.ds(r0, c), :]
            eg = jnp.exp(gc)
            gl = gc[c - 1:c, :] if spc == 1 else _sel_dot(seq_last, gc)
            et = jnp.exp(gl - gc)
            dt = jnp.exp(gl)
            qq = act_scr[pl.ds(r0, c), 0:H * HD]
            kk = act_scr[pl.ds(r0, c), H * HD:2 * H * HD]
            vv = act_scr[pl.ds(r0, c), 2 * H * HD:]
            qd_j, ktail_j, dtot_j = [], [], []
            for h, s in enumerate(heads):
                g_col = gc[:, 8 + h:9 + h]
                eg_col = eg[:, 8 + h:9 + h]
                be_col = be[:, h:h + 1]
                kh = mxu(kk[:, s])
                qk = _dot_nt(jnp.concatenate([mxu(qq[:, s]), kh], axis=0), kh)
                ed = jnp.exp(g_col - gt[8 + h:9 + h, :])
                lmat.append(be_col * qk[c:] * jnp.where(strict, ed, 0.0))
                amat.append(mxu(qk[:c] * jnp.where(incl, ed, 0.0)))
                rwu.append(mxu(jnp.concatenate([kk[:, s] * (be_col * eg_col), vv[:, s] * be_col], axis=1)))
                qd_j.append(narrow(qq[:, s] * eg_col))
                ktail_j.append(narrow(kk[:, s] * et[:, 8 + h:9 + h]))
                dtot_j.append(dt[:, 8 + h:9 + h])
            qd.append(qd_j)
            ktail.append(ktail_j)
            dtot.append(dtot_j)
            r0s.append(r0)
            sqs.append(_chunk_seq(idx, j, nu, ncs) if spc == 1 else (idx * nu + j) * spc)
        tinv = _unit_lower_inv(lmat, c, mxu, span=sl)
        wu = [_dot(mxu(t_), r_) for t_, r_ in zip(tinv, rwu)]

        if spc > 1:
            outs, finals = [], []
            for j in range(nu):
                o_j = []
                for h in range(H):
                    wuh = wu[j * H + h]
                    us, oins = [], []
                    for q in range(spc):
                        rs = slice(q * sl, (q + 1) * sl)
                        s_old = s_scr[sqs[j] + q, h]
                        qw = _dot(mxu(jnp.concatenate([qd[j][h][rs], wuh[rs, :HD]], axis=0)), mxu(s_old))
                        u = wuh[rs, HD:] - qw[sl:]
                        us.append(u)
                        oins.append(qw[:sl])
                        finals.append((sqs[j] + q, h,
                                       s_old * dtot[j][h][q * sl:q * sl + 1] + _dot_tn(ktail[j][h][rs], u)))
                    o_j.append(jnp.concatenate(oins, axis=0)
                               + _dot(amat[j * H + h], mxu(jnp.concatenate(us, axis=0))))
                outs.append(o_j)
            for j in range(nu):
                o_scr[pl.ds(r0s[j], c), :] = jnp.concatenate(outs[j], axis=1)
            for sq, h, s_new in finals:
                s_scr[sq, h] = s_new
            return carry

        own_state = ncs == 1
        loaded = [[s_scr[sqs[j], h] for h in range(H)] for j in range(nu if own_state else 1)]
        outs, finals = [], []
        st = loaded[0]
        for j in range(nu):
            if own_state:
                st = loaded[j]
            o_j, st_next = [], []
            for h in range(H):
                wuh = wu[j * H + h]
                qw = _dot(jnp.concatenate([qd[j][h], mxu(wuh[:, :HD])], axis=0), mxu(st[h]))
                u = wuh[:, HD:] - qw[c:]
                ub = mxu(u)
                o_j.append(qw[:c] + _dot(amat[j * H + h], ub))
                st_next.append(st[h] * dtot[j][h] + _dot_tn(ktail[j][h], ub))
            st = st_next
            outs.append(o_j)
            if own_state or j == nu - 1:
                finals.append((sqs[j], st))
        for j in range(nu):
            o_scr[pl.ds(r0s[j], c), :] = jnp.concatenate(outs[j], axis=1)
        for sq, st in finals:
            for h in range(H):
                s_scr[sq, h] = st[h]
        return carry

    _run_groups(group, rows // (nu * c))

    on = _head_rms_gate(o_scr[...], gn_ref[...], z_scr[...])
    o_ref[...] = x + _dot(on.astype(BF16), wo_ref[...])

    @pl.when(t == pl.num_programs(1) - 1)
    def _():
        sout_ref[...] = s_scr[...]
        for b in range(bb):
            cout_ref[b] = xp_scr[b, 8 - (CONV - 1):8, :]


def _gdn_call(x2d, s0, c0, nw, win, cw, par, gn, wo, *, nseq, seqlen, bb, tt, c, nu=2, row0=0,
              bcast_state=False):
    assert bb == 1 or tt == seqlen
    assert nseq % bb == 0 and seqlen % tt == 0 and tt % 8 == 0 and row0 % (bb * tt) == 0
    assert tt % c == 0 or (c % tt == 0 and (bb * tt) % c == 0)
    rows = bb * tt
    nt = seqlen // tt
    blk0 = row0 // rows
    s_idx = (lambda b, t: (0, 0, 0, 0)) if bcast_state else (lambda b, t: (b, 0, 0, 0))
    c_idx = (lambda b, t: (0, 0, 0)) if bcast_state else (lambda b, t: (b, 0, 0))
    slab = pltpu.VMEM((rows, D), F32)
    return pl.pallas_call(
        functools.partial(_gdn_kernel, bb=bb, tt=tt, c=c, nu=min(nu, bb * tt // c)),
        grid=(nseq // bb, nt),
        in_specs=[
            pl.BlockSpec((rows, D), lambda b, t: (blk0 + b * nt + t, 0)),
            pl.BlockSpec((bb, H, HD, HD), s_idx),
            pl.BlockSpec((bb, CONV - 1, QKV), c_idx),
            _const_spec((1, D)),
            _const_spec(win.shape),
            _const_spec((CONV, QKV)),
            _const_spec((2, LANES)),
            _const_spec((1, HD)),
            _const_spec((D, D)),
        ],
        out_specs=[
            pl.BlockSpec((rows, D), lambda b, t: (b * nt + t, 0)),
            pl.BlockSpec((bb, H, HD, HD), lambda b, t: (b, 0, 0, 0)),
            pl.BlockSpec((bb, CONV - 1, QKV), lambda b, t: (b, 0, 0)),
        ],
        out_shape=[
            jax.ShapeDtypeStruct((nseq * seqlen, D), F32),
            jax.ShapeDtypeStruct((nseq, H, HD, HD), F32),
            jax.ShapeDtypeStruct((nseq, CONV - 1, QKV), F32),
        ],
        scratch_shapes=[
            pltpu.VMEM((bb, H, HD, HD), F32),
            pltpu.VMEM((bb, tt + 8, QKV), F32),
            pltpu.VMEM((rows, QKV), F32),
            slab,
            pltpu.VMEM((rows, LANES), F32),
            pltpu.VMEM((rows, LANES), F32),
            slab,
        ],
        compiler_params=pltpu.CompilerParams(
            dimension_semantics=("arbitrary", "arbitrary"), vmem_limit_bytes=V7X_VMEM_LIMIT),
        name=f"gdn_mixer_{nseq}x{seqlen}",
    )(x2d, s0, c0, nw, win, cw, par, gn, wo)


def _swiglu(xb, w1_ref, w3_ref, w2_ref):
    acc = None
    for j in range(DFF // FB):
        fs = slice(j * FB, (j + 1) * FB)
        hh = (_silu(_dot(xb, w1_ref[:, fs])) * _dot(xb, w3_ref[:, fs])).astype(BF16)
        part = _dot(hh, w2_ref[fs, :])
        acc = part if acc is None else acc + part
    return acc


def _ffn_kernel(*refs, n_in, nblk_first):
    x_refs = refs[:n_in]
    nw_ref, w1_ref, w3_ref, w2_ref, o_ref = refs[n_in:]
    x = x_refs[0][...]
    if n_in == 2:
        x = jnp.where(pl.program_id(0) < nblk_first, x, x_refs[1][...])
    xb = _rms(x, nw_ref[...]).astype(BF16)
    o_ref[...] = x + _swiglu(xb, w1_ref, w3_ref, w2_ref)


def _two_group_specs(tm, nblk_first, nblk_total):
    last_first = nblk_first - 1
    return [
        pl.BlockSpec((tm, D), lambda i, *_: (jnp.minimum(i, last_first), 0)),
        pl.BlockSpec((tm, D), lambda i, *_: (jnp.maximum(i - nblk_first, 0), 0)),
    ]


def _ffn_call(xs, nw, w1, w3, w2, *, tm):
    nblks = [x.shape[0] // tm for x in xs]
    assert all(x.shape[0] % tm == 0 for x in xs)
    total = sum(nblks)
    if len(xs) == 2:
        x_specs = _two_group_specs(tm, nblks[0], total)
    else:
        x_specs = [pl.BlockSpec((tm, D), lambda i: (i, 0))]
    return pl.pallas_call(
        functools.partial(_ffn_kernel, n_in=len(xs), nblk_first=nblks[0]),
        grid=(total,),
        in_specs=x_specs + [_const_spec((1, D)), _const_spec((D, DFF)), _const_spec((D, DFF)), _const_spec((DFF, D))],
        out_specs=pl.BlockSpec((tm, D), lambda i: (i, 0)),
        out_shape=jax.ShapeDtypeStruct((total * tm, D), F32),
        compiler_params=pltpu.CompilerParams(dimension_semantics=("arbitrary",), vmem_limit_bytes=V7X_VMEM_LIMIT),
        name=f"ffn_dense_{total * tm}",
    )(*xs, nw, w1, w3, w2)


def _router_kernel(xa_ref, xb_ref, nw_ref, wr_ref, gate_ref, lpos_ref, lpost_ref, tab_ref, tot_ref, seg_scr, *,
                   tm, nblk_first):
    i = pl.program_id(0)

    @pl.when(i == 0)
    def _():
        seg_scr[...] = jnp.zeros_like(seg_scr)

    x = jnp.where(i < nblk_first, xa_ref[...], xb_ref[...])
    xn = _rms(x, nw_ref[...])
    logits = _dot_3x(xn, wr_ref[...])
    lane = _iota2((tm, LANES), 1).astype(F32)
    neg = jnp.float32(-jnp.inf)
    l1 = jnp.where(lane < NE, logits, neg)
    m1 = jnp.max(l1, axis=-1, keepdims=True)
    i1 = jnp.min(jnp.where(l1 == m1, lane, float(LANES)), axis=-1, keepdims=True)
    l2 = jnp.where(lane == i1, neg, l1)
    m2 = jnp.max(l2, axis=-1, keepdims=True)
    i2 = jnp.min(jnp.where(l2 == m2, lane, float(LANES)), axis=-1, keepdims=True)
    e2 = jnp.exp(m2 - m1)
    g1 = 1.0 / (1.0 + e2)
    g2 = e2 / (1.0 + e2)

    oh1 = (lane == i1).astype(F32)
    oh2 = (lane == i2).astype(F32)
    below = (_iota2((tm, tm), 0) > _iota2((tm, tm), 1)).astype(BF16)
    c1 = _dot(below, oh1.astype(BF16))
    c2 = _dot(below, oh2.astype(BF16))
    n1 = jnp.sum(oh1, axis=0, keepdims=True)
    n2 = jnp.sum(oh2, axis=0, keepdims=True)
    cnt = jnp.floor((n1 + n2 + (SEG - 1)) * (1.0 / SEG)) * SEG
    before = (_iota2((LANES, LANES), 0) < _iota2((LANES, LANES), 1)).astype(BF16)
    start = _dot(jnp.broadcast_to(cnt, (16, LANES)).astype(BF16), before)[0:1]
    p1 = jnp.sum(oh1 * (start + c1), axis=-1, keepdims=True)
    p2 = jnp.sum(oh2 * (start + n1 + c2), axis=-1, keepdims=True)
    used = seg_scr[...]
    seg_scr[...] = used + cnt
    tot_ref[...] = used + cnt

    gate_ref[:, 0:1] = g1
    gate_ref[:, 1:2] = g2
    lpos_ref[:, 0:1] = p1
    lpos_ref[:, 1:2] = p2
    sub = _iota2((8, LANES), 0)
    tab_ref[...] = jnp.where(sub == 0, cnt, jnp.where(sub == 1, start, jnp.where(sub == 2, used, 0.0)))
    slab = jnp.where(lane == 0.0, p1, jnp.where(lane == 1.0, p2, 0.0))
    eye16 = (_iota2((16, LANES), 0) == _iota2((16, LANES), 1)).astype(BF16)
    lpost_ref[...] = _sel_dot_nt(eye16, slab)[0:8]


def _router_call(xa, xb, nw, wr, *, tm):
    na, nb = xa.shape[0] // tm, xb.shape[0] // tm
    nblk = na + nb
    n = nblk * tm
    return pl.pallas_call(
        functools.partial(_router_kernel, tm=tm, nblk_first=na),
        grid=(nblk,),
        in_specs=_two_group_specs(tm, na, nblk) + [_const_spec((1, D)), _const_spec((D, LANES))],
        out_specs=[
            pl.BlockSpec((tm, 2), lambda i: (i, 0)),
            pl.BlockSpec((tm, 2), lambda i: (i, 0)),
            pl.BlockSpec((8, tm), lambda i: (i, 0)),
            pl.BlockSpec((8, LANES), lambda i: (i, 0)),
            pl.BlockSpec((1, LANES), lambda i: (0, 0)),
        ],
        out_shape=[
            jax.ShapeDtypeStruct((n, 2), F32),
            jax.ShapeDtypeStruct((n, 2), F32),
            jax.ShapeDtypeStruct((nblk * 8, tm), F32),
            jax.ShapeDtypeStruct((nblk * 8, LANES), F32),
            jax.ShapeDtypeStruct((1, LANES), F32),
        ],
        scratch_shapes=[pltpu.VMEM((1, LANES), F32)],
        compiler_params=pltpu.CompilerParams(dimension_semantics=("arbitrary",), vmem_limit_bytes=V7X_VMEM_LIMIT),
        name="moe_router",
    )(xa, xb, nw, wr)


def _local_rows(tm):
    return -(-(2 * tm + NE * (SEG - 1)) // LANES) * LANES


def _segment_copies(tab_ref, tile, local_ref, global_ref, sem, *, to_global, wait):
    for e in range(NE):
        base = (tile * NE + e) * 3
        npiece, lstart, gstart = tab_ref[base], tab_ref[base + 1], tab_ref[base + 2]

        def piece(k, carry, lstart=lstart, gstart=gstart):
            loc = local_ref.at[pl.ds(pl.multiple_of(lstart + k * SEG, SEG), SEG)]
            glo = global_ref.at[pl.ds(pl.multiple_of(gstart + k * SEG, SEG), SEG)]
            cp = pltpu.make_async_copy(loc, glo, sem) if to_global else pltpu.make_async_copy(glo, loc, sem)
            if wait:
                cp.wait()
            else:
                cp.start()
            return carry

        lax.fori_loop(0, npiece, piece, 0)


def _dispatch_kernel(tab_ref, zst_ref, xa_ref, xb_ref, lpost_ref, xs_ref, buf, zero_scr, sem, zsem, *,
                     tm, nblk_first, lr):
    i = pl.program_id(0)
    slot = lax.rem(i, 2)

    @pl.when(i == 0)
    def _():
        zero_scr[...] = jnp.zeros_like(zero_scr)

        def zero_copy(e):
            start = pl.multiple_of(zst_ref[e], TM)
            return pltpu.make_async_copy(zero_scr, xs_ref.at[pl.ds(start, TM)], zsem)

        for e in range(zst_ref.shape[0]):
            pl.when(zst_ref[e] >= 0)(lambda e=e: zero_copy(e).start())
        for e in range(zst_ref.shape[0]):
            pl.when(zst_ref[e] >= 0)(lambda e=e: zero_copy(e).wait())

    x = jnp.where(i < nblk_first, xa_ref[...], xb_ref[...])
    lpt = lpost_ref[...]
    rio = _iota2((lr, tm), 0).astype(F32)
    sel = jnp.where(rio == lpt[0:1, :], 1.0, jnp.where(rio == lpt[1:2, :], 1.0, 0.0)).astype(BF16)
    buf[slot] = _sel_dot2(sel, x)

    copies = functools.partial(_segment_copies, tab_ref, global_ref=xs_ref, to_global=True)
    copies(i, local_ref=buf.at[slot], sem=sem.at[slot], wait=False)
    pl.when(i > 0)(lambda: copies(i - 1, local_ref=buf.at[1 - slot], sem=sem.at[1 - slot], wait=True))
    pl.when(i == pl.num_programs(0) - 1)(
        lambda: copies(i, local_ref=buf.at[slot], sem=sem.at[slot], wait=True))


def _dispatch_call(tab, zst, xa, xb, lpost, *, tm, rmax):
    na, nb = xa.shape[0] // tm, xb.shape[0] // tm
    lr = _local_rows(tm)
    return pl.pallas_call(
        functools.partial(_dispatch_kernel, tm=tm, nblk_first=na, lr=lr),
        grid_spec=pltpu.PrefetchScalarGridSpec(
            num_scalar_prefetch=2,
            grid=(na + nb,),
            in_specs=_two_group_specs(tm, na, na + nb) + [pl.BlockSpec((8, tm), lambda i, *_: (i, 0))],
            out_specs=pl.BlockSpec(memory_space=pl.ANY),
            scratch_shapes=[pltpu.VMEM((2, lr, D), F32), pltpu.VMEM((TM, D), F32),
                            pltpu.SemaphoreType.DMA((2,)), pltpu.SemaphoreType.DMA],
        ),
        out_shape=jax.ShapeDtypeStruct((rmax, D), F32),
        compiler_params=pltpu.CompilerParams(dimension_semantics=("arbitrary",), vmem_limit_bytes=V7X_VMEM_LIMIT),
        name="moe_dispatch",
    )(tab, zst, xa, xb, lpost)


def _expert_kernel(texp_ref, tsrc_ref, nact_ref, xs_ref, nw_ref, w1_ref, w3_ref, w2_ref, ys_ref):
    i = pl.program_id(0)

    @pl.when(i < nact_ref[0])
    def _():
        xb = _rms(xs_ref[...], nw_ref[...]).astype(BF16)
        ys_ref[...] = _swiglu(xb, w1_ref, w3_ref, w2_ref)

    @pl.when(i >= nact_ref[0])
    def _():
        ys_ref[...] = jnp.zeros_like(ys_ref)


def _expert_call(texp, tsrc, nact, xs, nw, w1, w3, w2):
    ntile = xs.shape[0] // TM
    return pl.pallas_call(
        _expert_kernel,
        grid_spec=pltpu.PrefetchScalarGridSpec(
            num_scalar_prefetch=3,
            grid=(ntile,),
            in_specs=[
                pl.BlockSpec((TM, D), lambda i, te, ts, na: (ts[i], 0)),
                pl.BlockSpec((1, D), lambda i, te, ts, na: (0, 0)),
                pl.BlockSpec((None, D, DFF), lambda i, te, ts, na: (te[i], 0, 0)),
                pl.BlockSpec((None, D, DFF), lambda i, te, ts, na: (te[i], 0, 0)),
                pl.BlockSpec((None, DFF, D), lambda i, te, ts, na: (te[i], 0, 0)),
            ],
            out_specs=pl.BlockSpec((TM, D), lambda i, te, ts, na: (i, 0)),
        ),
        out_shape=jax.ShapeDtypeStruct((xs.shape[0], D), F32),
        compiler_params=pltpu.CompilerParams(dimension_semantics=("arbitrary",), vmem_limit_bytes=V7X_VMEM_LIMIT),
        name="moe_experts",
    )(texp, tsrc, nact, xs, nw, w1, w3, w2)


def _combine_kernel(tab_ref, x_ref, gate_ref, lpos_ref, nf_ref, ys_ref, o_ref, buf, sem, *, tm, blk0, lr):
    i = pl.program_id(0)
    slot = lax.rem(i, 2)

    def fetch(step, slot_):
        tile = blk0 + step
        last = (tile * NE + NE - 1) * 3
        used = pl.multiple_of(tab_ref[last + 1] + tab_ref[last] * SEG, SEG)

        def clear(k, carry):
            buf[slot_, pl.ds(pl.multiple_of(used + k * SEG, SEG), SEG), :] = jnp.zeros((SEG, D), F32)
            return carry

        lax.fori_loop(0, (lr - used) // SEG, clear, 0)
        _segment_copies(tab_ref, tile, buf.at[slot_], ys_ref, sem.at[slot_], to_global=False, wait=False)

    pl.when(i == 0)(lambda: fetch(i, slot))
    pl.when(i + 1 < pl.num_programs(0))(lambda: fetch(i + 1, 1 - slot))
    _segment_copies(tab_ref, blk0 + i, buf.at[slot], ys_ref, sem.at[slot], to_global=False, wait=True)

    y = buf[slot]
    y_hi = y.astype(BF16)
    y_lo = (y - y_hi.astype(F32)).astype(BF16)
    lp = lpos_ref[...]
    cio = _iota2((tm, lr), 1).astype(F32)
    g = gate_ref[...]
    moe = None
    for s in range(2):
        sel = jnp.where(cio == lp[:, s:s + 1], 1.0, 0.0).astype(BF16)
        picked = _dot(sel, y_hi) + _dot(sel, y_lo)
        moe = g[:, s:s + 1] * picked if moe is None else moe + g[:, s:s + 1] * picked
    o_ref[...] = _rms(x_ref[...] + moe, nf_ref[...])


def _combine_call(tab, x, gate, lpos, nf, ys, *, tm, tok0):
    n = x.shape[0]
    blk0 = tok0 // tm
    lr = _local_rows(tm)
    return pl.pallas_call(
        functools.partial(_combine_kernel, tm=tm, blk0=blk0, lr=lr),
        grid_spec=pltpu.PrefetchScalarGridSpec(
            num_scalar_prefetch=1,
            grid=(n // tm,),
            in_specs=[
                pl.BlockSpec((tm, D), lambda i, *_: (i, 0)),
                pl.BlockSpec((tm, 2), lambda i, *_: (blk0 + i, 0)),
                pl.BlockSpec((tm, 2), lambda i, *_: (blk0 + i, 0)),
                pl.BlockSpec((1, D), lambda i, *_: (0, 0)),
                pl.BlockSpec(memory_space=pl.ANY),
            ],
            out_specs=pl.BlockSpec((tm, D), lambda i, *_: (i, 0)),
            scratch_shapes=[pltpu.VMEM((2, lr, D), F32), pltpu.SemaphoreType.DMA((2,))],
        ),
        out_shape=jax.ShapeDtypeStruct((n, D), F32),
        compiler_params=pltpu.CompilerParams(dimension_semantics=("arbitrary",), vmem_limit_bytes=V7X_VMEM_LIMIT),
        name=f"moe_combine_{n}",
    )(tab, x, gate, lpos, nf, ys)


def kernel(x_prompt, x_sample, state_hgrn, state_gdn, state_gdn_conv, meta_tokens, norm_mix, norm_ffn, norm_final,
           hg_w_in, hg_lower_bounds, hg_g_norm, hg_w_o, gdn_w_in, gdn_conv_w, gdn_a_log, gdn_dt_bias, gdn_g_norm,
           gdn_w_o, ffn_w1, ffn_w3, ffn_w2, moe_w_router, moe_w1, moe_w3, moe_w2):
    bp, tp, _ = x_prompt.shape
    bs, ts, _ = x_sample.shape
    np_, ns_ = bp * tp, bs * ts
    n_tok = np_ + ns_
    row = lambda v: v.reshape(1, -1).astype(F32)

    hg_win = hg_w_in.astype(BF16)
    hg_wo = hg_w_o.astype(BF16)
    gdn_win = jnp.concatenate(
        [gdn_w_in[:, :QKV + D], gdn_w_in[:, QKV + D:], jnp.zeros((D, LANES - 2 * H), gdn_w_in.dtype)],
        axis=1).astype(BF16)
    gdn_wo = gdn_w_o.astype(BF16)
    par = jnp.zeros((2, LANES), F32)
    par = par.at[0, H:2 * H].set(gdn_dt_bias.astype(F32)).at[1, H:2 * H].set(gdn_a_log.astype(F32))
    w1d, w3d, w2d = ffn_w1.astype(BF16), ffn_w3.astype(BF16), ffn_w2.astype(BF16)
    w1e, w3e, w2e = moe_w1.astype(BF16), moe_w3.astype(BF16), moe_w2.astype(BF16)
    wr = jnp.concatenate([moe_w_router.astype(F32), jnp.zeros((D, LANES - NE), F32)], axis=1)
    nm0, nm1 = row(norm_mix[0]), row(norm_mix[1])
    nf0, nf1 = row(norm_ffn[0]), row(norm_ffn[1])
    nfin = row(norm_final)
    hg_gn, gdn_gn = row(hg_g_norm), row(gdn_g_norm)
    lbnd = hg_lower_bounds.astype(F32)
    cw = gdn_conv_w.astype(F32)

    hg_args = (nm0, hg_win, lbnd, hg_gn, hg_wo)
    gdn_args = (nm1, gdn_win, cw, par, gdn_gn, gdn_wo)
    xm = meta_tokens.astype(F32)
    xp2 = x_prompt.reshape(np_, D)
    xs2 = x_sample.reshape(ns_, D)

    zero_state = jnp.zeros((1, H, HD, HD), F32)
    h1m, hg_m = _hgrn_call(xm, zero_state, *hg_args, nseq=1, seqlen=N_META, bb=1, tt=N_META, c=N_META)
    h1p, hg_p = _hgrn_call(xp2, hg_m, *hg_args, nseq=bp, seqlen=tp, bb=1, tt=256, c=64, nu=4, bcast_state=True)
    h1s, hg_s = _hgrn_call(xs2, state_hgrn.astype(F32), *hg_args, nseq=bs, seqlen=ts, bb=8, tt=ts, c=64, nu=1)
    h2m = _ffn_call([h1m], nf0, w1d, w3d, w2d, tm=N_META)
    h2 = _ffn_call([h1p, h1s], nf0, w1d, w3d, w2d, tm=512)

    zero_conv = jnp.zeros((1, CONV - 1, QKV), F32)
    _, gdn_m, conv_m = _gdn_call(h2m, zero_state, zero_conv, *gdn_args, nseq=1, seqlen=N_META, bb=1, tt=N_META, c=16)
    h3p, gdn_p, conv_p = _gdn_call(h2, gdn_m, conv_m, *gdn_args, nseq=bp, seqlen=tp, bb=1, tt=256, c=128, nu=2,
                                   bcast_state=True)
    h3s, gdn_s, conv_s = _gdn_call(h2, state_gdn.astype(F32), state_gdn_conv.astype(F32), *gdn_args,
                                   nseq=bs, seqlen=ts, bb=8, tt=ts, c=64, nu=1, row0=np_)

    yp, ysm = _moe_final(h3p, h3s, nf1, wr, w1e, w3e, w2e, nfin, tm=256)

    sd, gd, cd = state_hgrn.dtype, state_gdn.dtype, state_gdn_conv.dtype
    return (yp.reshape(bp, tp, D), ysm.reshape(bs, ts, D),
            hg_p.astype(sd), hg_s.astype(sd),
            gdn_p.astype(gd), gdn_s.astype(gd), conv_p.astype(cd), conv_s.astype(cd))


def _moe_final(h3p, h3s, nf1, wr, w1e, w3e, w2e, nfin, *, tm):
    np_ = h3p.shape[0]
    n_tok = np_ + h3s.shape[0]
    nblk = n_tok // tm
    gate, lpos, lpost, tab, tot = _router_call(h3p, h3s, nf1, wr, tm=tm)
    totals = tot[0, :NE].astype(jnp.int32)
    padded = ((totals + TM - 1) // TM) * TM
    ends = jnp.cumsum(padded)
    offs = ends - padded
    tab3 = tab.reshape(nblk, 8, LANES)[:, :3, :NE].astype(jnp.int32)
    seg_tab = jnp.stack([tab3[:, 0] // SEG, tab3[:, 1], tab3[:, 2] + offs[None, :]], axis=-1).reshape(-1)
    ntile = -(-(2 * n_tok + nblk * NE * (SEG - 1) + NE * (TM - 1)) // TM)
    nact = ends[-1] // TM
    tiles = jnp.arange(ntile, dtype=jnp.int32)
    texp_all = jnp.minimum(jnp.sum(tiles[:, None] * TM >= ends[None, :], axis=1), NE - 1).astype(jnp.int32)
    last = jnp.maximum(nact - 1, 0)
    active = tiles < nact
    texp = jnp.where(active, texp_all, texp_all[last]).astype(jnp.int32)
    tsrc = jnp.where(active, tiles, last).astype(jnp.int32)
    tail = nact + jnp.arange(ntile - 2 * n_tok // TM)
    zst = jnp.concatenate([jnp.where(padded > 0, ends - TM, -1),
                           jnp.where(tail < ntile, tail * TM, -1)]).astype(jnp.int32)
    xs_sorted = _dispatch_call(seg_tab, zst, h3p, h3s, lpost, tm=tm, rmax=ntile * TM)
    ys = _expert_call(texp, tsrc, nact.reshape(1).astype(jnp.int32), xs_sorted, nf1, w1e, w3e, w2e)
    yp = _combine_call(seg_tab, h3p, gate, lpos, nfin, ys, tm=tm, tok0=0)
    ysm = _combine_call(seg_tab, h3s, gate, lpos, nfin, ys, tm=tm, tok0=np_)
    return yp, ysm
```

```python
import functools

import jax
import jax.numpy as jnp
from jax import lax
from jax.experimental import pallas as pl
from jax.experimental.pallas import tpu as pltpu

D = 1024
H = 8
HD = 128
QKV = 3 * H * HD
CONV = 4
DFF = 3584
NE = 8
N_META = 16
EPS = 1e-6
LANES = 128
FB = 512
TM = 512
HG_SUB = 16
SEG = 8
V7X_VMEM_LIMIT = 56 * 1024 * 1024

F32 = jnp.float32
BF16 = jnp.bfloat16


def _dot(a, b):
    return jnp.dot(a, b, preferred_element_type=F32)


def _dot_nt(a, b):
    return lax.dot_general(a, b, (((1,), (1,)), ((), ())), preferred_element_type=F32)


def _dot_tn(a, b):
    return lax.dot_general(a, b, (((0,), (0,)), ((), ())), preferred_element_type=F32)


def _split3(x):
    x1 = x.astype(BF16)
    r1 = x - x1.astype(F32)
    x2 = r1.astype(BF16)
    x3 = (r1 - x2.astype(F32)).astype(BF16)
    return x1, x2, x3


def _sel_dot(sel, x):
    x1, x2, x3 = _split3(x)
    return _dot(sel, x1) + _dot(sel, x2) + _dot(sel, x3)


def _sel_dot2(sel, x):
    hi = x.astype(BF16)
    lo = (x - hi.astype(F32)).astype(BF16)
    return _dot(sel, hi) + _dot(sel, lo)


def _dot_3x(a, b):
    a_hi = a.astype(BF16)
    a_lo = (a - a_hi.astype(F32)).astype(BF16)
    b_hi = b.astype(BF16)
    b_lo = (b - b_hi.astype(F32)).astype(BF16)
    return _dot(a_hi, b_hi) + (_dot(a_hi, b_lo) + _dot(a_lo, b_hi))


def _sel_dot_nt(sel, x):
    x1, x2, x3 = _split3(x)
    return _dot_nt(sel, x1) + _dot_nt(sel, x2) + _dot_nt(sel, x3)


def _rms(x, g):
    return x * lax.rsqrt(jnp.mean(x * x, axis=-1, keepdims=True) + EPS) * g


def _silu(x):
    return x * jax.nn.sigmoid(x)


def _iota2(shape, dim):
    return lax.broadcasted_iota(jnp.int32, shape, dim)


def _head_rms_gate(o, g, gate):
    parts = []
    for h in range(H):
        oh = o[:, h * HD:(h + 1) * HD]
        parts.append(_rms(oh, g))
    return jnp.concatenate(parts, axis=1) * gate


def _chunk_row(idx, j, nu, c):
    if isinstance(idx, int):
        return (idx * nu + j) * c
    return pl.multiple_of((idx * nu + j) * c, c)


def _chunk_seq(idx, j, nu, ncs):
    return (idx * nu + j) // ncs


def _chunk_ops(tri, c):
    assert c % 16 == 0
    tri_b = tri.astype(BF16)
    return (lambda x: _sel_dot(tri_b, x)), (lambda x: x.astype(BF16))


def _run_groups(group, n):
    if n == 1:
        group(0, 0)
    else:
        lax.fori_loop(0, n, group, 0)


def _const_spec(shape):
    nd = len(shape)
    return pl.BlockSpec(shape, lambda *_: (0,) * nd, pipeline_mode=pl.Buffered(1))


def _hgrn_kernel(x_ref, s0_ref, nw_ref, win_ref, lb_ref, gn_ref, wo_ref,
                 o_ref, sout_ref,
                 st_scr, q_scr, k_scr, lf_scr, v_scr, g_scr, *, bb, tt, c, nu):
    t = pl.program_id(1)
    rows = bb * tt
    sl = min(c, tt)
    spc = c // sl
    ncs = max(tt // c, 1)
    assert c % sl == 0 and (ncs == 1 or ncs % nu == 0)

    @pl.when(t == 0)
    def _():
        for b in range(bb):
            for h in range(H):
                st_scr[b, h] = s0_ref[b, h].T

    x = x_ref[...]
    xn = _rms(x, nw_ref[...]).astype(BF16)
    lbn = lb_ref[...]
    e = jnp.exp(lbn - jnp.max(lbn, axis=0, keepdims=True))
    lb = e[0:1] / jnp.sum(e, axis=0, keepdims=True)
    q_scr[...] = _silu(_dot(xn, win_ref[:, 0:D])) * HD ** -0.5
    f = lb + (1.0 - lb) * jax.nn.sigmoid(_dot(xn, win_ref[:, D:2 * D]))
    k_scr[...] = 1.0 - f
    lf_scr[...] = jnp.log(f)
    v_scr[...] = _dot(xn, win_ref[:, 2 * D:3 * D])
    g_scr[...] = _silu(_dot(xn, win_ref[:, 3 * D:4 * D]))

    row = _iota2((c, c), 0)
    col = _iota2((c, c), 1)
    tri = (row >= col) & ((row // sl) == (col // sl))
    cumsum, mxu = _chunk_ops(tri, c)
    heads = [slice(h * HD, (h + 1) * HD) for h in range(H)]
    nsub = c // HG_SUB if sl > HG_SUB else 1
    if nsub > 1:
        local_tri = (tri & ((row // HG_SUB) == (col // HG_SUB))).astype(BF16)
        slab_row = _iota2((c, D), 0)
    if spc > 1:
        seq_last = (col == (row // sl) * sl + (sl - 1)).astype(BF16)

    def chunk_terms(r0):
        lf = lf_scr[pl.ds(r0, c), :]
        bc = cumsum(lf)
        tot = bc[c - 1:c, :] if spc == 1 else _sel_dot(seq_last, bc)
        qq = q_scr[pl.ds(r0, c), :]
        kk = k_scr[pl.ds(r0, c), :]
        v32 = v_scr[pl.ds(r0, c), :]
        vv = mxu(v32)
        qd32 = qq * jnp.exp(bc)
        kt32 = kk * jnp.exp(tot - bc)
        qdj = mxu(qd32)
        ktail = mxu(kt32)
        if nsub == 1:
            kinv = mxu(kk * jnp.exp(-bc))
            scj = [_dot_nt(qdj[:, s], kinv[:, s]) for s in heads]
        else:
            bl = _sel_dot(local_tri, lf)
            qloc = mxu(qq * jnp.exp(bl))
            parts = [[] for _ in heads]
            for i in range(nsub):
                lo, hi = i * HG_SUB, (i + 1) * HG_SUB
                ref_pt = bc[lo - 1:lo, :] if i else 0.0
                kci = mxu(jnp.where(slab_row < hi, kk * jnp.exp(ref_pt - bc), 0.0))
                for h, s in enumerate(heads):
                    parts[h].append(_dot_nt(qloc[lo:hi, s], kci[:, s]))
            scj = [jnp.concatenate(p, axis=0) for p in parts]
        scj = [mxu(jnp.where(tri, sc, 0.0)) for sc in scj]
        oij = [_dot(scj[h], vv[:, heads[h]]) for h in range(H)]
        if spc > 1:
            return qd32, jnp.exp(tot), (v32, kt32), oij
        kvj = [_dot_tn(vv[:, s], ktail[:, s]) for s in heads]
        return qdj, jnp.exp(tot), kvj, oij

    def group(idx, carry):
        qd, dec, kv, oi, r0s, sqs = [], [], [], [], [], []
        for j in range(nu):
            r0 = _chunk_row(idx, j, nu, c)
            qdj, decj, kvj, oij = chunk_terms(r0)
            kv.append(kvj)
            oi.append(oij)
            qd.append(qdj)
            dec.append(decj)
            r0s.append(r0)
            sqs.append(_chunk_seq(idx, j, nu, ncs) if spc == 1 else (idx * nu + j) * spc)

        if spc > 1:
            outs, finals = [], []
            for j in range(nu):
                v32, kt32 = kv[j]
                o_j = []
                for h, s in enumerate(heads):
                    parts = []
                    for q in range(spc):
                        rs = slice(q * sl, (q + 1) * sl)
                        st = st_scr[sqs[j] + q, h]
                        parts.append(oi[j][h][rs] + _dot_nt(qd[j][rs, s], st))
                        finals.append((sqs[j] + q, h,
                                       st * dec[j][q * sl:q * sl + 1, s] + _dot_tn(v32[rs, s], kt32[rs, s])))
                    o_j.append(jnp.concatenate(parts, axis=0))
                outs.append(o_j)
            for j in range(nu):
                q_scr[pl.ds(r0s[j], c), :] = jnp.concatenate(outs[j], axis=1)
            for sq, h, st_new in finals:
                st_scr[sq, h] = st_new
            return carry

        own_state = ncs == 1
        loaded = [[st_scr[sqs[j], h] for h in range(H)] for j in range(nu if own_state else 1)]
        outs, finals = [], []
        st = loaded[0]
        for j in range(nu):
            if own_state:
                st = loaded[j]
            outs.append([oi[j][h] + _dot_nt(qd[j][:, heads[h]], mxu(st[h])) for h in range(H)])
            st = [st[h] * dec[j][:, heads[h]] + kv[j][h] for h in range(H)]
            if own_state or j == nu - 1:
                finals.append((sqs[j], st))
        for j in range(nu):
            q_scr[pl.ds(r0s[j], c), :] = jnp.concatenate(outs[j], axis=1)
        for sq, st in finals:
            for h in range(H):
                st_scr[sq, h] = st[h]
        return carry

    _run_groups(group, rows // (nu * c))

    on = _head_rms_gate(q_scr[...], gn_ref[...], g_scr[...])
    o_ref[...] = x + _dot(on.astype(BF16), wo_ref[...])

    @pl.when(t == pl.num_programs(1) - 1)
    def _():
        for b in range(bb):
            for h in range(H):
                sout_ref[b, h] = st_scr[b, h].T


def _hgrn_call(x2d, s0, nw, win, lbnd, gn, wo, *, nseq, seqlen, bb, tt, c, nu=4, row0=0, bcast_state=False):
    assert bb == 1 or tt == seqlen
    assert nseq % bb == 0 and seqlen % tt == 0 and row0 % (bb * tt) == 0
    assert tt % c == 0 or (c % tt == 0 and (bb * tt) % c == 0)
    rows = bb * tt
    nu = min(nu, rows // c)
    nt = seqlen // tt
    blk0 = row0 // rows
    s_idx = (lambda b, t: (0, 0, 0, 0)) if bcast_state else (lambda b, t: (b, 0, 0, 0))
    slab = pltpu.VMEM((rows, D), F32)
    return pl.pallas_call(
        functools.partial(_hgrn_kernel, bb=bb, tt=tt, c=c, nu=nu),
        grid=(nseq // bb, nt),
        in_specs=[
            pl.BlockSpec((rows, D), lambda b, t: (blk0 + b * nt + t, 0)),
            pl.BlockSpec((bb, H, HD, HD), s_idx),
            _const_spec((1, D)),
            _const_spec((D, 4 * D)),
            _const_spec(lbnd.shape),
            _const_spec((1, HD)),
            _const_spec((D, D)),
        ],
        out_specs=[
            pl.BlockSpec((rows, D), lambda b, t: (b * nt + t, 0)),
            pl.BlockSpec((bb, H, HD, HD), lambda b, t: (b, 0, 0, 0)),
        ],
        out_shape=[
            jax.ShapeDtypeStruct((nseq * seqlen, D), F32),
            jax.ShapeDtypeStruct((nseq, H, HD, HD), F32),
        ],
        scratch_shapes=[pltpu.VMEM((bb, H, HD, HD), F32), slab, slab, slab, slab, slab],
        compiler_params=pltpu.CompilerParams(
            dimension_semantics=("arbitrary", "arbitrary"), vmem_limit_bytes=V7X_VMEM_LIMIT),
        name=f"hgrn_mixer_{nseq}x{seqlen}",
    )(x2d, s0, nw, win, lbnd, gn, wo)


def _neumann_inv(ls, eye, levels, mxu):
    xs = [eye - l for l in ls]
    ps = [mxu(l) for l in ls]
    for _ in range(levels - 1):
        ps = [mxu(_dot(p, p)) for p in ps]
        xs = [x + _dot(mxu(x), p) for x, p in zip(xs, ps)]
    return xs


def _unit_lower_inv(ls, c, mxu, span=None):
    span = c if span is None else span
    row = _iota2((c, c), 0)
    col = _iota2((c, c), 1)
    eye = (row == col).astype(F32)
    if span <= 16:
        return _neumann_inv(ls, eye, span.bit_length() - 1, mxu)
    sub = 16
    nblk = c // sub
    assert c == nblk * sub and nblk & (nblk - 1) == 0
    diag = (row // sub) == (col // sub)
    lds = [jnp.where(diag, l, 0.0) for l in ls]
    dinvs = [mxu(d) for d in _neumann_inv(lds, eye, 4, mxu)]
    ms = [_dot(d, mxu(l - ld)) for d, l, ld in zip(dinvs, ls, lds)]
    ys = _neumann_inv(ms, eye, nblk.bit_length() - 1, mxu)
    return [_dot(mxu(y), d) for y, d in zip(ys, dinvs)]


def _gdn_kernel(x_ref, s0_ref, c0_ref, nw_ref, win_ref, cw_ref, par_ref, gn_ref, wo_ref,
                o_ref, sout_ref, cout_ref,
                s_scr, xp_scr, act_scr, z_scr, la_scr, be_scr, o_scr, *, bb, tt, c, nu):
    t = pl.program_id(1)
    rows = bb * tt
    sl = min(c, tt)
    spc = c // sl
    ncs = max(tt // c, 1)
    assert c % sl == 0 and (ncs == 1 or ncs % nu == 0)

    @pl.when(t == 0)
    def _():
        s_scr[...] = s0_ref[...]
        for b in range(bb):
            xp_scr[b, 8 - (CONV - 1):8, :] = c0_ref[b]

    x = x_ref[...]
    xn = _rms(x, nw_ref[...]).astype(BF16)
    pre = _dot(xn, win_ref[:, 0:QKV])
    for b in range(bb):
        xp_scr[b, 8:8 + tt, :] = pre[b * tt:(b + 1) * tt]
    z_scr[...] = _silu(_dot(xn, win_ref[:, QKV:QKV + D]))
    ba = _dot(xn, win_ref[:, QKV + D:QKV + D + LANES])
    par = par_ref[...]
    sp = ba + par[0:1]
    softplus = jnp.maximum(sp, 0.0) + jnp.log(1.0 + jnp.exp(-jnp.abs(sp)))
    la_scr[...] = -jnp.exp(par[1:2]) * softplus
    be_scr[...] = jax.nn.sigmoid(ba)

    cw = cw_ref[...]
    for b in range(bb):
        conv = xp_scr[b, 8 - (CONV - 1):8 - (CONV - 1) + tt, :] * cw[0:1]
        for w in range(1, CONV):
            conv = conv + xp_scr[b, 8 - (CONV - 1) + w:8 - (CONV - 1) + w + tt, :] * cw[w:w + 1]
        act = _silu(conv)
        for j in range(2 * H):
            a = act[:, j * HD:(j + 1) * HD]
            scale = HD ** -0.5 if j < H else 1.0
            act_scr[b * tt:(b + 1) * tt, j * HD:(j + 1) * HD] = (
                a * lax.rsqrt(jnp.sum(a * a, axis=-1, keepdims=True) + EPS) * scale)
        act_scr[b * tt:(b + 1) * tt, 2 * H * HD:] = act[:, 2 * H * HD:]
        tail = xp_scr[b, tt:tt + 8, :]
        xp_scr[b, 0:8, :] = tail

    row = _iota2((c, c), 0)
    col = _iota2((c, c), 1)
    same_seq = (row // sl) == (col // sl)
    incl = (row >= col) & same_seq
    strict = (row > col) & same_seq
    cumsum, mxu = _chunk_ops(incl, c)
    eye_l = (_iota2((LANES, LANES), 0) == _iota2((LANES, LANES), 1)).astype(BF16)
    transpose = lambda m: _sel_dot_nt(eye_l, m)
    if spc > 1:
        seq_last = (col == (row // sl) * sl + (sl - 1)).astype(BF16)
    narrow = mxu if spc == 1 else (lambda a: a)

    heads = [slice(h * HD, (h + 1) * HD) for h in range(H)]

    def group(idx, carry):
        qd, ktail, dtot, lmat, amat, rwu, r0s, sqs = [], [], [], [], [], [], [], []
        for j in range(nu):
            r0 = _chunk_row(idx, j, nu, c)
            gc = cumsum(la_scr[pl.ds(r0, c), :])
            gt = transpose(gc)
            be = be_scr[pl.ds(r0, c), :]
            eg = jnp.exp(gc)
            gl = gc[c - 1:c, :] if spc == 1 else _sel_dot(seq_last, gc)
            et = jnp.exp(gl - gc)
            dt = jnp.exp(gl)
            qq = act_scr[pl.ds(r0, c), 0:H * HD]
            kk = act_scr[pl.ds(r0, c), H * HD:2 * H * HD]
            vv = act_scr[pl.ds(r0, c), 2 * H * HD:]
            qd_j, ktail_j, dtot_j = [], [], []
            for h, s in enumerate(heads):
                g_col = gc[:, 8 + h:9 + h]
                eg_col = eg[:, 8 + h:9 + h]
                be_col = be[:, h:h + 1]
                kh = mxu(kk[:, s])
                qk = _dot_nt(jnp.concatenate([mxu(qq[:, s]), kh], axis=0), kh)
                ed = jnp.exp(g_col - gt[8 + h:9 + h, :])
                lmat.append(be_col * qk[c:] * jnp.where(strict, ed, 0.0))
                amat.append(mxu(qk[:c] * jnp.where(incl, ed, 0.0)))
                rwu.append(mxu(jnp.concatenate([kk[:, s] * (be_col * eg_col), vv[:, s] * be_col], axis=1)))
                qd_j.append(narrow(qq[:, s] * eg_col))
                ktail_j.append(narrow(kk[:, s] * et[:, 8 + h:9 + h]))
                dtot_j.append(dt[:, 8 + h:9 + h])
            qd.append(qd_j)
            ktail.append(ktail_j)
            dtot.append(dtot_j)
            r0s.append(r0)
            sqs.append(_chunk_seq(idx, j, nu, ncs) if spc == 1 else (idx * nu + j) * spc)
        tinv = _unit_lower_inv(lmat, c, mxu, span=sl)
        wu = [_dot(mxu(t_), r_) for t_, r_ in zip(tinv, rwu)]

        if spc > 1:
            outs, finals = [], []
            for j in range(nu):
                o_j = []
                for h in range(H):
                    wuh = wu[j * H + h]
                    us, oins = [], []
                    for q in range(spc):
                        rs = slice(q * sl, (q + 1) * sl)
                        s_old = s_scr[sqs[j] + q, h]
                        qw = _dot(mxu(jnp.concatenate([qd[j][h][rs], wuh[rs, :HD]], axis=0)), mxu(s_old))
                        u = wuh[rs, HD:] - qw[sl:]
                        us.append(u)
                        oins.append(qw[:sl])
                        finals.append((sqs[j] + q, h,
                                       s_old * dtot[j][h][q * sl:q * sl + 1] + _dot_tn(ktail[j][h][rs], u)))
                    o_j.append(jnp.concatenate(oins, axis=0)
                               + _dot(amat[j * H + h], mxu(jnp.concatenate(us, axis=0))))
                outs.append(o_j)
            for j in range(nu):
                o_scr[pl.ds(r0s[j], c), :] = jnp.concatenate(outs[j], axis=1)
            for sq, h, s_new in finals:
                s_scr[sq, h] = s_new
            return carry

        own_state = ncs == 1
        loaded = [[s_scr[sqs[j], h] for h in range(H)] for j in range(nu if own_state else 1)]
        outs, finals = [], []
        st = loaded[0]
        for j in range(nu):
            if own_state:
                st = loaded[j]
            o_j, st_next = [], []
            for h in range(H):
                wuh = wu[j * H + h]
                qw = _dot(jnp.concatenate([qd[j][h], mxu(wuh[:, :HD])], axis=0), mxu(st[h]))
                u = wuh[:, HD:] - qw[c:]
                ub = mxu(u)
                o_j.append(qw[:c] + _dot(amat[j * H + h], ub))
                st_next.append(st[h] * dtot[j][h] + _dot_tn(ktail[j][h], ub))
            st = st_next
            outs.append(o_j)
            if own_state or j == nu - 1:
                finals.append((sqs[j], st))
        for j in range(nu):
            o_scr[pl.ds(r0s[j], c), :] = jnp.concatenate(outs[j], axis=1)
        for sq, st in finals:
            for h in range(H):
                s_scr[sq, h] = st[h]
        return carry

    _run_groups(group, rows // (nu * c))

    on = _head_rms_gate(o_scr[...], gn_ref[...], z_scr[...])
    o_ref[...] = x + _dot(on.astype(BF16), wo_ref[...])

    @pl.when(t == pl.num_programs(1) - 1)
    def _():
        sout_ref[...] = s_scr[...]
        for b in range(bb):
            cout_ref[b] = xp_scr[b, 8 - (CONV - 1):8, :]


def _gdn_call(x2d, s0, c0, nw, win, cw, par, gn, wo, *, nseq, seqlen, bb, tt, c, nu=2, row0=0,
              bcast_state=False):
    assert bb == 1 or tt == seqlen
    assert nseq % bb == 0 and seqlen % tt == 0 and tt % 8 == 0 and row0 % (bb * tt) == 0
    assert tt % c == 0 or (c % tt == 0 and (bb * tt) % c == 0)
    rows = bb * tt
    nt = seqlen // tt
    blk0 = row0 // rows
    s_idx = (lambda b, t: (0, 0, 0, 0)) if bcast_state else (lambda b, t: (b, 0, 0, 0))
    c_idx = (lambda b, t: (0, 0, 0)) if bcast_state else (lambda b, t: (b, 0, 0))
    slab = pltpu.VMEM((rows, D), F32)
    return pl.pallas_call(
        functools.partial(_gdn_kernel, bb=bb, tt=tt, c=c, nu=min(nu, bb * tt // c)),
        grid=(nseq // bb, nt),
        in_specs=[
            pl.BlockSpec((rows, D), lambda b, t: (blk0 + b * nt + t, 0)),
            pl.BlockSpec((bb, H, HD, HD), s_idx),
            pl.BlockSpec((bb, CONV - 1, QKV), c_idx),
            _const_spec((1, D)),
            _const_spec(win.shape),
            _const_spec((CONV, QKV)),
            _const_spec((2, LANES)),
            _const_spec((1, HD)),
            _const_spec((D, D)),
        ],
        out_specs=[
            pl.BlockSpec((rows, D), lambda b, t: (b * nt + t, 0)),
            pl.BlockSpec((bb, H, HD, HD), lambda b, t: (b, 0, 0, 0)),
            pl.BlockSpec((bb, CONV - 1, QKV), lambda b, t: (b, 0, 0)),
        ],
        out_shape=[
            jax.ShapeDtypeStruct((nseq * seqlen, D), F32),
            jax.ShapeDtypeStruct((nseq, H, HD, HD), F32),
            jax.ShapeDtypeStruct((nseq, CONV - 1, QKV), F32),
        ],
        scratch_shapes=[
            pltpu.VMEM((bb, H, HD, HD), F32),
            pltpu.VMEM((bb, tt + 8, QKV), F32),
            pltpu.VMEM((rows, QKV), F32),
            slab,
            pltpu.VMEM((rows, LANES), F32),
            pltpu.VMEM((rows, LANES), F32),
            slab,
        ],
        compiler_params=pltpu.CompilerParams(
            dimension_semantics=("arbitrary", "arbitrary"), vmem_limit_bytes=V7X_VMEM_LIMIT),
        name=f"gdn_mixer_{nseq}x{seqlen}",
    )(x2d, s0, c0, nw, win, cw, par, gn, wo)


def _swiglu(xb, w1_ref, w3_ref, w2_ref):
    acc = None
    for j in range(DFF // FB):
        fs = slice(j * FB, (j + 1) * FB)
        hh = (_silu(_dot(xb, w1_ref[:, fs])) * _dot(xb, w3_ref[:, fs])).astype(BF16)
        part = _dot(hh, w2_ref[fs, :])
        acc = part if acc is None else acc + part
    return acc


def _ffn_kernel(*refs, n_in, nblk_first):
    x_refs = refs[:n_in]
    nw_ref, w1_ref, w3_ref, w2_ref, o_ref = refs[n_in:]
    x = x_refs[0][...]
    if n_in == 2:
        x = jnp.where(pl.program_id(0) < nblk_first, x, x_refs[1][...])
    xb = _rms(x, nw_ref[...]).astype(BF16)
    o_ref[...] = x + _swiglu(xb, w1_ref, w3_ref, w2_ref)


def _two_group_specs(tm, nblk_first, nblk_total):
    last_first = nblk_first - 1
    return [
        pl.BlockSpec((tm, D), lambda i, *_: (jnp.minimum(i, last_first), 0)),
        pl.BlockSpec((tm, D), lambda i, *_: (jnp.maximum(i - nblk_first, 0), 0)),
    ]


def _ffn_call(xs, nw, w1, w3, w2, *, tm):
    nblks = [x.shape[0] // tm for x in xs]
    assert all(x.shape[0] % tm == 0 for x in xs)
    total = sum(nblks)
    if len(xs) == 2:
        x_specs = _two_group_specs(tm, nblks[0], total)
    else:
        x_specs = [pl.BlockSpec((tm, D), lambda i: (i, 0))]
    return pl.pallas_call(
        functools.partial(_ffn_kernel, n_in=len(xs), nblk_first=nblks[0]),
        grid=(total,),
        in_specs=x_specs + [_const_spec((1, D)), _const_spec((D, DFF)), _const_spec((D, DFF)), _const_spec((DFF, D))],
        out_specs=pl.BlockSpec((tm, D), lambda i: (i, 0)),
        out_shape=jax.ShapeDtypeStruct((total * tm, D), F32),
        compiler_params=pltpu.CompilerParams(dimension_semantics=("arbitrary",), vmem_limit_bytes=V7X_VMEM_LIMIT),
        name=f"ffn_dense_{total * tm}",
    )(*xs, nw, w1, w3, w2)


def _router_kernel(xa_ref, xb_ref, nw_ref, wr_ref, gate_ref, lpos_ref, lpost_ref, tab_ref, tot_ref, seg_scr, *,
                   tm, nblk_first):
    i = pl.program_id(0)

    @pl.when(i == 0)
    def _():
        seg_scr[...] = jnp.zeros_like(seg_scr)

    x = jnp.where(i < nblk_first, xa_ref[...], xb_ref[...])
    xn = _rms(x, nw_ref[...])
    logits = _dot_3x(xn, wr_ref[...])
    lane = _iota2((tm, LANES), 1).astype(F32)
    neg = jnp.float32(-jnp.inf)
    l1 = jnp.where(lane < NE, logits, neg)
    m1 = jnp.max(l1, axis=-1, keepdims=True)
    i1 = jnp.min(jnp.where(l1 == m1, lane, float(LANES)), axis=-1, keepdims=True)
    l2 = jnp.where(lane == i1, neg, l1)
    m2 = jnp.max(l2, axis=-1, keepdims=True)
    i2 = jnp.min(jnp.where(l2 == m2, lane, float(LANES)), axis=-1, keepdims=True)
    e2 = jnp.exp(m2 - m1)
    g1 = 1.0 / (1.0 + e2)
    g2 = e2 / (1.0 + e2)

    oh1 = (lane == i1).astype(F32)
    oh2 = (lane == i2).astype(F32)
    below = (_iota2((tm, tm), 0) > _iota2((tm, tm), 1)).astype(BF16)
    c1 = _dot(below, oh1.astype(BF16))
    c2 = _dot(below, oh2.astype(BF16))
    n1 = jnp.sum(oh1, axis=0, keepdims=True)
    n2 = jnp.sum(oh2, axis=0, keepdims=True)
    cnt = jnp.floor((n1 + n2 + (SEG - 1)) * (1.0 / SEG)) * SEG
    before = (_iota2((LANES, LANES), 0) < _iota2((LANES, LANES), 1)).astype(BF16)
    start = _dot(jnp.broadcast_to(cnt, (16, LANES)).astype(BF16), before)[0:1]
    p1 = jnp.sum(oh1 * (start + c1), axis=-1, keepdims=True)
    p2 = jnp.sum(oh2 * (start + n1 + c2), axis=-1, keepdims=True)
    used = seg_scr[...]
    seg_scr[...] = used + cnt
    tot_ref[...] = used + cnt

    gate_ref[:, 0:1] = g1
    gate_ref[:, 1:2] = g2
    lpos_ref[:, 0:1] = p1
    lpos_ref[:, 1:2] = p2
    sub = _iota2((8, LANES), 0)
    tab_ref[...] = jnp.where(sub == 0, cnt, jnp.where(sub == 1, start, jnp.where(sub == 2, used, 0.0)))
    slab = jnp.where(lane == 0.0, p1, jnp.where(lane == 1.0, p2, 0.0))
    eye16 = (_iota2((16, LANES), 0) == _iota2((16, LANES), 1)).astype(BF16)
    lpost_ref[...] = _sel_dot_nt(eye16, slab)[0:8]


def _router_call(xa, xb, nw, wr, *, tm):
    na, nb = xa.shape[0] // tm, xb.shape[0] // tm
    nblk = na + nb
    n = nblk * tm
    return pl.pallas_call(
        functools.partial(_router_kernel, tm=tm, nblk_first=na),
        grid=(nblk,),
        in_specs=_two_group_specs(tm, na, nblk) + [_const_spec((1, D)), _const_spec((D, LANES))],
        out_specs=[
            pl.BlockSpec((tm, 2), lambda i: (i, 0)),
            pl.BlockSpec((tm, 2), lambda i: (i, 0)),
            pl.BlockSpec((8, tm), lambda i: (i, 0)),
            pl.BlockSpec((8, LANES), lambda i: (i, 0)),
            pl.BlockSpec((1, LANES), lambda i: (0, 0)),
        ],
        out_shape=[
            jax.ShapeDtypeStruct((n, 2), F32),
            jax.ShapeDtypeStruct((n, 2), F32),
            jax.ShapeDtypeStruct((nblk * 8, tm), F32),
            jax.ShapeDtypeStruct((nblk * 8, LANES), F32),
            jax.ShapeDtypeStruct((1, LANES), F32),
        ],
        scratch_shapes=[pltpu.VMEM((1, LANES), F32)],
        compiler_params=pltpu.CompilerParams(dimension_semantics=("arbitrary",), vmem_limit_bytes=V7X_VMEM_LIMIT),
        name="moe_router",
    )(xa, xb, nw, wr)


def _local_rows(tm):
    return -(-(2 * tm + NE * (SEG - 1)) // LANES) * LANES


def _segment_copies(tab_ref, tile, local_ref, global_ref, sem, *, to_global, wait):
    for e in range(NE):
        base = (tile * NE + e) * 3
        npiece, lstart, gstart = tab_ref[base], tab_ref[base + 1], tab_ref[base + 2]

        def piece(k, carry, lstart=lstart, gstart=gstart):
            loc = local_ref.at[pl.ds(pl.multiple_of(lstart + k * SEG, SEG), SEG)]
            glo = global_ref.at[pl.ds(pl.multiple_of(gstart + k * SEG, SEG), SEG)]
            cp = pltpu.make_async_copy(loc, glo, sem) if to_global else pltpu.make_async_copy(glo, loc, sem)
            if wait:
                cp.wait()
            else:
                cp.start()
            return carry

        lax.fori_loop(0, npiece, piece, 0)


def _dispatch_kernel(tab_ref, zst_ref, xa_ref, xb_ref, lpost_ref, xs_ref, buf, zero_scr, sem, zsem, *,
                     tm, nblk_first, lr):
    i = pl.program_id(0)
    slot = lax.rem(i, 2)

    @pl.when(i == 0)
    def _():
        zero_scr[...] = jnp.zeros_like(zero_scr)

        def zero_copy(e):
            start = pl.multiple_of(zst_ref[e], TM)
            return pltpu.make_async_copy(zero_scr, xs_ref.at[pl.ds(start, TM)], zsem)

        for e in range(zst_ref.shape[0]):
            pl.when(zst_ref[e] >= 0)(lambda e=e: zero_copy(e).start())
        for e in range(zst_ref.shape[0]):
            pl.when(zst_ref[e] >= 0)(lambda e=e: zero_copy(e).wait())

    x = jnp.where(i < nblk_first, xa_ref[...], xb_ref[...])
    lpt = lpost_ref[...]
    rio = _iota2((lr, tm), 0).astype(F32)
    sel = jnp.where(rio == lpt[0:1, :], 1.0, jnp.where(rio == lpt[1:2, :], 1.0, 0.0)).astype(BF16)
    buf[slot] = _sel_dot2(sel, x)

    copies = functools.partial(_segment_copies, tab_ref, global_ref=xs_ref, to_global=True)
    copies(i, local_ref=buf.at[slot], sem=sem.at[slot], wait=False)
    pl.when(i > 0)(lambda: copies(i - 1, local_ref=buf.at[1 - slot], sem=sem.at[1 - slot], wait=True))
    pl.when(i == pl.num_programs(0) - 1)(
        lambda: copies(i, local_ref=buf.at[slot], sem=sem.at[slot], wait=True))


def _dispatch_call(tab, zst, xa, xb, lpost, *, tm, rmax):
    na, nb = xa.shape[0] // tm, xb.shape[0] // tm
    lr = _local_rows(tm)
    return pl.pallas_call(
        functools.partial(_dispatch_kernel, tm=tm, nblk_first=na, lr=lr),
        grid_spec=pltpu.PrefetchScalarGridSpec(
            num_scalar_prefetch=2,
            grid=(na + nb,),
            in_specs=_two_group_specs(tm, na, na + nb) + [pl.BlockSpec((8, tm), lambda i, *_: (i, 0))],
            out_specs=pl.BlockSpec(memory_space=pl.ANY),
            scratch_shapes=[pltpu.VMEM((2, lr, D), F32), pltpu.VMEM((TM, D), F32),
                            pltpu.SemaphoreType.DMA((2,)), pltpu.SemaphoreType.DMA],
        ),
        out_shape=jax.ShapeDtypeStruct((rmax, D), F32),
        compiler_params=pltpu.CompilerParams(dimension_semantics=("arbitrary",), vmem_limit_bytes=V7X_VMEM_LIMIT),
        name="moe_dispatch",
    )(tab, zst, xa, xb, lpost)


def _expert_kernel(texp_ref, tsrc_ref, tvalid_ref, xs_ref, nw_ref, w1_ref, w3_ref, w2_ref, ys_ref):
    valid = tvalid_ref[pl.program_id(0)]
    half = TM // 2

    @pl.when(valid > half)
    def _():
        xb = _rms(xs_ref[...], nw_ref[...]).astype(BF16)
        ys_ref[...] = _swiglu(xb, w1_ref, w3_ref, w2_ref)

    @pl.when((valid > 0) & (valid <= half))
    def _():
        xb = _rms(xs_ref[0:half, :], nw_ref[...]).astype(BF16)
        ys_ref[0:half, :] = _swiglu(xb, w1_ref, w3_ref, w2_ref)
        ys_ref[half:, :] = jnp.zeros((TM - half, D), F32)

    @pl.when(valid == 0)
    def _():
        ys_ref[...] = jnp.zeros_like(ys_ref)


def _expert_call(texp, tsrc, tvalid, xs, nw, w1, w3, w2):
    ntile = xs.shape[0] // TM
    return pl.pallas_call(
        _expert_kernel,
        grid_spec=pltpu.PrefetchScalarGridSpec(
            num_scalar_prefetch=3,
            grid=(ntile,),
            in_specs=[
                pl.BlockSpec((TM, D), lambda i, te, ts, na: (ts[i], 0)),
                pl.BlockSpec((1, D), lambda i, te, ts, na: (0, 0)),
                pl.BlockSpec((None, D, DFF), lambda i, te, ts, na: (te[i], 0, 0)),
                pl.BlockSpec((None, D, DFF), lambda i, te, ts, na: (te[i], 0, 0)),
                pl.BlockSpec((None, DFF, D), lambda i, te, ts, na: (te[i], 0, 0)),
            ],
            out_specs=pl.BlockSpec((TM, D), lambda i, te, ts, na: (i, 0)),
        ),
        out_shape=jax.ShapeDtypeStruct((xs.shape[0], D), F32),
        compiler_params=pltpu.CompilerParams(dimension_semantics=("arbitrary",), vmem_limit_bytes=V7X_VMEM_LIMIT),
        name="moe_experts",
    )(texp, tsrc, tvalid, xs, nw, w1, w3, w2)


def _combine_kernel(tab_ref, x_ref, gate_ref, lpos_ref, nf_ref, ys_ref, o_ref, buf, sem, *, tm, blk0, lr):
    i = pl.program_id(0)
    slot = lax.rem(i, 2)

    def fetch(step, slot_):
        tile = blk0 + step
        last = (tile * NE + NE - 1) * 3
        used = pl.multiple_of(tab_ref[last + 1] + tab_ref[last] * SEG, SEG)

        def clear(k, carry):
            buf[slot_, pl.ds(pl.multiple_of(used + k * SEG, SEG), SEG), :] = jnp.zeros((SEG, D), F32)
            return carry

        lax.fori_loop(0, (lr - used) // SEG, clear, 0)
        _segment_copies(tab_ref, tile, buf.at[slot_], ys_ref, sem.at[slot_], to_global=False, wait=False)

    pl.when(i == 0)(lambda: fetch(i, slot))
    pl.when(i + 1 < pl.num_programs(0))(lambda: fetch(i + 1, 1 - slot))
    _segment_copies(tab_ref, blk0 + i, buf.at[slot], ys_ref, sem.at[slot], to_global=False, wait=True)

    y = buf[slot]
    y_hi = y.astype(BF16)
    y_lo = (y - y_hi.astype(F32)).astype(BF16)
    lp = lpos_ref[...]
    cio = _iota2((tm, lr), 1).astype(F32)
    g = gate_ref[...]
    moe = None
    for s in range(2):
        sel = jnp.where(cio == lp[:, s:s + 1], 1.0, 0.0).astype(BF16)
        picked = _dot(sel, y_hi) + _dot(sel, y_lo)
        moe = g[:, s:s + 1] * picked if moe is None else moe + g[:, s:s + 1] * picked
    o_ref[...] = _rms(x_ref[...] + moe, nf_ref[...])


def _combine_call(tab, x, gate, lpos, nf, ys, *, tm, tok0):
    n = x.shape[0]
    blk0 = tok0 // tm
    lr = _local_rows(tm)
    return pl.pallas_call(
        functools.partial(_combine_kernel, tm=tm, blk0=blk0, lr=lr),
        grid_spec=pltpu.PrefetchScalarGridSpec(
            num_scalar_prefetch=1,
            grid=(n // tm,),
            in_specs=[
                pl.BlockSpec((tm, D), lambda i, *_: (i, 0)),
                pl.BlockSpec((tm, 2), lambda i, *_: (blk0 + i, 0)),
                pl.BlockSpec((tm, 2), lambda i, *_: (blk0 + i, 0)),
                pl.BlockSpec((1, D), lambda i, *_: (0, 0)),
                pl.BlockSpec(memory_space=pl.ANY),
            ],
            out_specs=pl.BlockSpec((tm, D), lambda i, *_: (i, 0)),
            scratch_shapes=[pltpu.VMEM((2, lr, D), F32), pltpu.SemaphoreType.DMA((2,))],
        ),
        out_shape=jax.ShapeDtypeStruct((n, D), F32),
        compiler_params=pltpu.CompilerParams(dimension_semantics=("arbitrary",), vmem_limit_bytes=V7X_VMEM_LIMIT),
        name=f"moe_combine_{n}",
    )(tab, x, gate, lpos, nf, ys)


def kernel(x_prompt, x_sample, state_hgrn, state_gdn, state_gdn_conv, meta_tokens, norm_mix, norm_ffn, norm_final,
           hg_w_in, hg_lower_bounds, hg_g_norm, hg_w_o, gdn_w_in, gdn_conv_w, gdn_a_log, gdn_dt_bias, gdn_g_norm,
           gdn_w_o, ffn_w1, ffn_w3, ffn_w2, moe_w_router, moe_w1, moe_w3, moe_w2):
    bp, tp, _ = x_prompt.shape
    bs, ts, _ = x_sample.shape
    np_, ns_ = bp * tp, bs * ts
    n_tok = np_ + ns_
    row = lambda v: v.reshape(1, -1).astype(F32)

    hg_win = hg_w_in.astype(BF16)
    hg_wo = hg_w_o.astype(BF16)
    gdn_win = jnp.concatenate(
        [gdn_w_in[:, :QKV + D], gdn_w_in[:, QKV + D:], jnp.zeros((D, LANES - 2 * H), gdn_w_in.dtype)],
        axis=1).astype(BF16)
    gdn_wo = gdn_w_o.astype(BF16)
    par = jnp.zeros((2, LANES), F32)
    par = par.at[0, H:2 * H].set(gdn_dt_bias.astype(F32)).at[1, H:2 * H].set(gdn_a_log.astype(F32))
    w1d, w3d, w2d = ffn_w1.astype(BF16), ffn_w3.astype(BF16), ffn_w2.astype(BF16)
    w1e, w3e, w2e = moe_w1.astype(BF16), moe_w3.astype(BF16), moe_w2.astype(BF16)
    wr = jnp.concatenate([moe_w_router.astype(F32), jnp.zeros((D, LANES - NE), F32)], axis=1)
    nm0, nm1 = row(norm_mix[0]), row(norm_mix[1])
    nf0, nf1 = row(norm_ffn[0]), row(norm_ffn[1])
    nfin = row(norm_final)
    hg_gn, gdn_gn = row(hg_g_norm), row(gdn_g_norm)
    lbnd = hg_lower_bounds.astype(F32)
    cw = gdn_conv_w.astype(F32)

    hg_args = (nm0, hg_win, lbnd, hg_gn, hg_wo)
    gdn_args = (nm1, gdn_win, cw, par, gdn_gn, gdn_wo)
    xm = meta_tokens.astype(F32)
    xp2 = x_prompt.reshape(np_, D)
    xs2 = x_sample.reshape(ns_, D)

    zero_state = jnp.zeros((1, H, HD, HD), F32)
    h1m, hg_m = _hgrn_call(xm, zero_state, *hg_args, nseq=1, seqlen=N_META, bb=1, tt=N_META, c=N_META)
    h1p, hg_p = _hgrn_call(xp2, hg_m, *hg_args, nseq=bp, seqlen=tp, bb=1, tt=256, c=64, nu=4, bcast_state=True)
    h1s, hg_s = _hgrn_call(xs2, state_hgrn.astype(F32), *hg_args, nseq=bs, seqlen=ts, bb=8, tt=ts, c=64, nu=1)
    h2m = _ffn_call([h1m], nf0, w1d, w3d, w2d, tm=N_META)
    h2 = _ffn_call([h1p, h1s], nf0, w1d, w3d, w2d, tm=512)

    zero_conv = jnp.zeros((1, CONV - 1, QKV), F32)
    _, gdn_m, conv_m = _gdn_call(h2m, zero_state, zero_conv, *gdn_args, nseq=1, seqlen=N_META, bb=1, tt=N_META, c=16)
    h3p, gdn_p, conv_p = _gdn_call(h2, gdn_m, conv_m, *gdn_args, nseq=bp, seqlen=tp, bb=1, tt=256, c=128, nu=2,
                                   bcast_state=True)
    h3s, gdn_s, conv_s = _gdn_call(h2, state_gdn.astype(F32), state_gdn_conv.astype(F32), *gdn_args,
                                   nseq=bs, seqlen=ts, bb=8, tt=ts, c=64, nu=1, row0=np_)

    yp, ysm = _moe_final(h3p, h3s, nf1, wr, w1e, w3e, w2e, nfin, tm=256)

    sd, gd, cd = state_hgrn.dtype, state_gdn.dtype, state_gdn_conv.dtype
    return (yp.reshape(bp, tp, D), ysm.reshape(bs, ts, D),
            hg_p.astype(sd), hg_s.astype(sd),
            gdn_p.astype(gd), gdn_s.astype(gd), conv_p.astype(cd), conv_s.astype(cd))


def _moe_final(h3p, h3s, nf1, wr, w1e, w3e, w2e, nfin, *, tm):
    np_ = h3p.shape[0]
    n_tok = np_ + h3s.shape[0]
    nblk = n_tok // tm
    gate, lpos, lpost, tab, tot = _router_call(h3p, h3s, nf1, wr, tm=tm)
    totals = tot[0, :NE].astype(jnp.int32)
    padded = ((totals + TM - 1) // TM) * TM
    ends = jnp.cumsum(padded)
    offs = ends - padded
    tab3 = tab.reshape(nblk, 8, LANES)[:, :3, :NE].astype(jnp.int32)
    seg_tab = jnp.stack([tab3[:, 0] // SEG, tab3[:, 1], tab3[:, 2] + offs[None, :]], axis=-1).reshape(-1)
    ntile = -(-(2 * n_tok + nblk * NE * (SEG - 1) + NE * (TM - 1)) // TM)
    nact = ends[-1] // TM
    tiles = jnp.arange(ntile, dtype=jnp.int32)
    texp_all = jnp.minimum(jnp.sum(tiles[:, None] * TM >= ends[None, :], axis=1), NE - 1).astype(jnp.int32)
    last = jnp.maximum(nact - 1, 0)
    active = tiles < nact
    texp = jnp.where(active, texp_all, texp_all[last]).astype(jnp.int32)
    tsrc = jnp.where(active, tiles, last).astype(jnp.int32)
    tvalid = jnp.where(active, jnp.clip(totals[texp_all] - (tiles * TM - offs[texp_all]), 0, TM), 0).astype(jnp.int32)
    tail = nact + jnp.arange(ntile - 2 * n_tok // TM)
    zst = jnp.concatenate([jnp.where(padded > 0, ends - TM, -1),
                           jnp.where(tail < ntile, tail * TM, -1)]).astype(jnp.int32)
    xs_sorted = _dispatch_call(seg_tab, zst, h3p, h3s, lpost, tm=tm, rmax=ntile * TM)
    ys = _expert_call(texp, tsrc, tvalid, xs_sorted, nf1, w1e, w3e, w2e)
    yp = _combine_call(seg_tab, h3p, gate, lpos, nfin, ys, tm=tm, tok0=0)
    ysm = _combine_call(seg_tab, h3s, gate, lpos, nfin, ys, tm=tm, tok0=np_)
    return yp, ysm
```

```python
import functools

import jax
import jax.numpy as jnp
from jax import lax
from jax.experimental import pallas as pl
from jax.experimental.pallas import tpu as pltpu

D = 1024
H = 8
HD = 128
QKV = 3 * H * HD
CONV = 4
DFF = 3584
NE = 8
N_META = 16
EPS = 1e-6
LANES = 128
FB = 512
TM = 512
HG_SUB = 16
SEG = 8
V7X_VMEM_LIMIT = 56 * 1024 * 1024

F32 = jnp.float32
BF16 = jnp.bfloat16


def _dot(a, b):
    return jnp.dot(a, b, preferred_element_type=F32)


def _dot_nt(a, b):
    return lax.dot_general(a, b, (((1,), (1,)), ((), ())), preferred_element_type=F32)


def _dot_tn(a, b):
    return lax.dot_general(a, b, (((0,), (0,)), ((), ())), preferred_element_type=F32)


def _split3(x):
    x1 = x.astype(BF16)
    r1 = x - x1.astype(F32)
    x2 = r1.astype(BF16)
    x3 = (r1 - x2.astype(F32)).astype(BF16)
    return x1, x2, x3


def _sel_dot(sel, x):
    x1, x2, x3 = _split3(x)
    return _dot(sel, x1) + _dot(sel, x2) + _dot(sel, x3)


def _dot_3x(a, b):
    a_hi = a.astype(BF16)
    a_lo = (a - a_hi.astype(F32)).astype(BF16)
    b_hi = b.astype(BF16)
    b_lo = (b - b_hi.astype(F32)).astype(BF16)
    return _dot(a_hi, b_hi) + (_dot(a_hi, b_lo) + _dot(a_lo, b_hi))


def _sel_dot_nt(sel, x):
    x1, x2, x3 = _split3(x)
    return _dot_nt(sel, x1) + _dot_nt(sel, x2) + _dot_nt(sel, x3)


def _rms(x, g):
    return x * lax.rsqrt(jnp.mean(x * x, axis=-1, keepdims=True) + EPS) * g


def _silu(x):
    return x * jax.nn.sigmoid(x)


def _iota2(shape, dim):
    return lax.broadcasted_iota(jnp.int32, shape, dim)


def _head_rms_gate(o, g, gate):
    parts = []
    for h in range(H):
        oh = o[:, h * HD:(h + 1) * HD]
        parts.append(_rms(oh, g))
    return jnp.concatenate(parts, axis=1) * gate


def _chunk_row(idx, j, nu, c):
    if isinstance(idx, int):
        return (idx * nu + j) * c
    return pl.multiple_of((idx * nu + j) * c, c)


def _chunk_seq(idx, j, nu, ncs):
    return (idx * nu + j) // ncs


def _chunk_ops(tri, c):
    assert c % 16 == 0
    tri_b = tri.astype(BF16)
    return (lambda x: _sel_dot(tri_b, x)), (lambda x: x.astype(BF16))


def _run_groups(group, n):
    if n == 1:
        group(0, 0)
    else:
        lax.fori_loop(0, n, group, 0)


def _const_spec(shape):
    nd = len(shape)
    return pl.BlockSpec(shape, lambda *_: (0,) * nd, pipeline_mode=pl.Buffered(1))


def _hgrn_kernel(x_ref, s0_ref, nw_ref, win_ref, lb_ref, gn_ref, wo_ref,
                 o_ref, sout_ref,
                 st_scr, q_scr, k_scr, lf_scr, v_scr, g_scr, *, bb, tt, c, nu):
    t = pl.program_id(1)
    rows = bb * tt
    sl = min(c, tt)
    spc = c // sl
    ncs = max(tt // c, 1)
    assert c % sl == 0 and (ncs == 1 or ncs % nu == 0)

    @pl.when(t == 0)
    def _():
        for b in range(bb):
            for h in range(H):
                st_scr[b, h] = s0_ref[b, h].T

    x = x_ref[...]
    xn = _rms(x, nw_ref[...]).astype(BF16)
    lbn = lb_ref[...]
    e = jnp.exp(lbn - jnp.max(lbn, axis=0, keepdims=True))
    lb = e[0:1] / jnp.sum(e, axis=0, keepdims=True)
    q_scr[...] = _silu(_dot(xn, win_ref[:, 0:D])) * HD ** -0.5
    f = lb + (1.0 - lb) * jax.nn.sigmoid(_dot(xn, win_ref[:, D:2 * D]))
    k_scr[...] = 1.0 - f
    lf_scr[...] = jnp.log(f)
    v_scr[...] = _dot(xn, win_ref[:, 2 * D:3 * D])
    g_scr[...] = _silu(_dot(xn, win_ref[:, 3 * D:4 * D]))

    row = _iota2((c, c), 0)
    col = _iota2((c, c), 1)
    tri = (row >= col) & ((row // sl) == (col // sl))
    cumsum, mxu = _chunk_ops(tri, c)
    heads = [slice(h * HD, (h + 1) * HD) for h in range(H)]
    nsub = c // HG_SUB if sl > HG_SUB else 1
    if nsub > 1:
        local_tri = (tri & ((row // HG_SUB) == (col // HG_SUB))).astype(BF16)
        slab_row = _iota2((c, D), 0)
    if spc > 1:
        seq_last = (col == (row // sl) * sl + (sl - 1)).astype(BF16)

    def chunk_terms(r0):
        lf = lf_scr[pl.ds(r0, c), :]
        bc = cumsum(lf)
        tot = bc[c - 1:c, :] if spc == 1 else _sel_dot(seq_last, bc)
        qq = q_scr[pl.ds(r0, c), :]
        kk = k_scr[pl.ds(r0, c), :]
        v32 = v_scr[pl.ds(r0, c), :]
        vv = mxu(v32)
        qd32 = qq * jnp.exp(bc)
        kt32 = kk * jnp.exp(tot - bc)
        qdj = mxu(qd32)
        ktail = mxu(kt32)
        if nsub == 1:
            kinv = mxu(kk * jnp.exp(-bc))
            scj = [_dot_nt(qdj[:, s], kinv[:, s]) for s in heads]
        else:
            bl = _sel_dot(local_tri, lf)
            qloc = mxu(qq * jnp.exp(bl))
            parts = [[] for _ in heads]
            for i in range(nsub):
                lo, hi = i * HG_SUB, (i + 1) * HG_SUB
                ref_pt = bc[lo - 1:lo, :] if i else 0.0
                kci = mxu(jnp.where(slab_row < hi, kk * jnp.exp(ref_pt - bc), 0.0))
                for h, s in enumerate(heads):
                    parts[h].append(_dot_nt(qloc[lo:hi, s], kci[:, s]))
            scj = [jnp.concatenate(p, axis=0) for p in parts]
        scj = [mxu(jnp.where(tri, sc, 0.0)) for sc in scj]
        oij = [_dot(scj[h], vv[:, heads[h]]) for h in range(H)]
        if spc > 1:
            return qd32, jnp.exp(tot), (v32, kt32), oij
        kvj = [_dot_tn(vv[:, s], ktail[:, s]) for s in heads]
        return qdj, jnp.exp(tot), kvj, oij

    def group(idx, carry):
        qd, dec, kv, oi, r0s, sqs = [], [], [], [], [], []
        for j in range(nu):
            r0 = _chunk_row(idx, j, nu, c)
            qdj, decj, kvj, oij = chunk_terms(r0)
            kv.append(kvj)
            oi.append(oij)
            qd.append(qdj)
            dec.append(decj)
            r0s.append(r0)
            sqs.append(_chunk_seq(idx, j, nu, ncs) if spc == 1 else (idx * nu + j) * spc)

        if spc > 1:
            outs, finals = [], []
            for j in range(nu):
                v32, kt32 = kv[j]
                o_j = []
                for h, s in enumerate(heads):
                    parts = []
                    for q in range(spc):
                        rs = slice(q * sl, (q + 1) * sl)
                        st = st_scr[sqs[j] + q, h]
                        parts.append(oi[j][h][rs] + _dot_nt(qd[j][rs, s], st))
                        finals.append((sqs[j] + q, h,
                                       st * dec[j][q * sl:q * sl + 1, s] + _dot_tn(v32[rs, s], kt32[rs, s])))
                    o_j.append(jnp.concatenate(parts, axis=0))
                outs.append(o_j)
            for j in range(nu):
                q_scr[pl.ds(r0s[j], c), :] = jnp.concatenate(outs[j], axis=1)
            for sq, h, st_new in finals:
                st_scr[sq, h] = st_new
            return carry

        own_state = ncs == 1
        loaded = [[st_scr[sqs[j], h] for h in range(H)] for j in range(nu if own_state else 1)]
        outs, finals = [], []
        st = loaded[0]
        for j in range(nu):
            if own_state:
                st = loaded[j]
            outs.append([oi[j][h] + _dot_nt(qd[j][:, heads[h]], mxu(st[h])) for h in range(H)])
            st = [st[h] * dec[j][:, heads[h]] + kv[j][h] for h in range(H)]
            if own_state or j == nu - 1:
                finals.append((sqs[j], st))
        for j in range(nu):
            q_scr[pl.ds(r0s[j], c), :] = jnp.concatenate(outs[j], axis=1)
        for sq, st in finals:
            for h in range(H):
                st_scr[sq, h] = st[h]
        return carry

    _run_groups(group, rows // (nu * c))

    on = _head_rms_gate(q_scr[...], gn_ref[...], g_scr[...])
    o_ref[...] = x + _dot(on.astype(BF16), wo_ref[...])

    @pl.when(t == pl.num_programs(1) - 1)
    def _():
        for b in range(bb):
            for h in range(H):
                sout_ref[b, h] = st_scr[b, h].T


def _hgrn_call(x2d, s0, nw, win, lbnd, gn, wo, *, nseq, seqlen, bb, tt, c, nu=4, row0=0, bcast_state=False):
    assert bb == 1 or tt == seqlen
    assert nseq % bb == 0 and seqlen % tt == 0 and row0 % (bb * tt) == 0
    assert tt % c == 0 or (c % tt == 0 and (bb * tt) % c == 0)
    rows = bb * tt
    nu = min(nu, rows // c)
    nt = seqlen // tt
    blk0 = row0 // rows
    s_idx = (lambda b, t: (0, 0, 0, 0)) if bcast_state else (lambda b, t: (b, 0, 0, 0))
    slab = pltpu.VMEM((rows, D), F32)
    return pl.pallas_call(
        functools.partial(_hgrn_kernel, bb=bb, tt=tt, c=c, nu=nu),
        grid=(nseq // bb, nt),
        in_specs=[
            pl.BlockSpec((rows, D), lambda b, t: (blk0 + b * nt + t, 0)),
            pl.BlockSpec((bb, H, HD, HD), s_idx),
            _const_spec((1, D)),
            _const_spec((D, 4 * D)),
            _const_spec(lbnd.shape),
            _const_spec((1, HD)),
            _const_spec((D, D)),
        ],
        out_specs=[
            pl.BlockSpec((rows, D), lambda b, t: (b * nt + t, 0)),
            pl.BlockSpec((bb, H, HD, HD), lambda b, t: (b, 0, 0, 0)),
        ],
        out_shape=[
            jax.ShapeDtypeStruct((nseq * seqlen, D), F32),
            jax.ShapeDtypeStruct((nseq, H, HD, HD), F32),
        ],
        scratch_shapes=[pltpu.VMEM((bb, H, HD, HD), F32), slab, slab, slab, slab, slab],
        compiler_params=pltpu.CompilerParams(
            dimension_semantics=("arbitrary", "arbitrary"), vmem_limit_bytes=V7X_VMEM_LIMIT),
        name=f"hgrn_mixer_{nseq}x{seqlen}",
    )(x2d, s0, nw, win, lbnd, gn, wo)


def _neumann_inv(ls, eye, levels, mxu):
    xs = [eye - l for l in ls]
    ps = [mxu(l) for l in ls]
    for _ in range(levels - 1):
        ps = [mxu(_dot(p, p)) for p in ps]
        xs = [x + _dot(mxu(x), p) for x, p in zip(xs, ps)]
    return xs


def _unit_lower_inv(ls, c, mxu, span=None):
    span = c if span is None else span
    row = _iota2((c, c), 0)
    col = _iota2((c, c), 1)
    eye = (row == col).astype(F32)
    if span <= 16:
        return _neumann_inv(ls, eye, span.bit_length() - 1, mxu)
    sub = 16
    nblk = c // sub
    assert c == nblk * sub and nblk & (nblk - 1) == 0
    diag = (row // sub) == (col // sub)
    lds = [jnp.where(diag, l, 0.0) for l in ls]
    dinvs = [mxu(d) for d in _neumann_inv(lds, eye, 4, mxu)]
    ms = [_dot(d, mxu(l - ld)) for d, l, ld in zip(dinvs, ls, lds)]
    ys = _neumann_inv(ms, eye, nblk.bit_length() - 1, mxu)
    return [_dot(mxu(y), d) for y, d in zip(ys, dinvs)]


def _gdn_kernel(x_ref, s0_ref, c0_ref, nw_ref, win_ref, cw_ref, par_ref, gn_ref, wo_ref,
                o_ref, sout_ref, cout_ref,
                s_scr, xp_scr, act_scr, z_scr, la_scr, be_scr, o_scr, *, bb, tt, c, nu):
    t = pl.program_id(1)
    rows = bb * tt
    sl = min(c, tt)
    spc = c // sl
    ncs = max(tt // c, 1)
    assert c % sl == 0 and (ncs == 1 or ncs % nu == 0)

    @pl.when(t == 0)
    def _():
        s_scr[...] = s0_ref[...]
        for b in range(bb):
            xp_scr[b, 8 - (CONV - 1):8, :] = c0_ref[b]

    x = x_ref[...]
    xn = _rms(x, nw_ref[...]).astype(BF16)
    pre = _dot(xn, win_ref[:, 0:QKV])
    for b in range(bb):
        xp_scr[b, 8:8 + tt, :] = pre[b * tt:(b + 1) * tt]
    z_scr[...] = _silu(_dot(xn, win_ref[:, QKV:QKV + D]))
    ba = _dot(xn, win_ref[:, QKV + D:QKV + D + LANES])
    par = par_ref[...]
    sp = ba + par[0:1]
    softplus = jnp.maximum(sp, 0.0) + jnp.log(1.0 + jnp.exp(-jnp.abs(sp)))
    la_scr[...] = -jnp.exp(par[1:2]) * softplus
    be_scr[...] = jax.nn.sigmoid(ba)

    cw = cw_ref[...]
    for b in range(bb):
        conv = xp_scr[b, 8 - (CONV - 1):8 - (CONV - 1) + tt, :] * cw[0:1]
        for w in range(1, CONV):
            conv = conv + xp_scr[b, 8 - (CONV - 1) + w:8 - (CONV - 1) + w + tt, :] * cw[w:w + 1]
        act = _silu(conv)
        for j in range(2 * H):
            a = act[:, j * HD:(j + 1) * HD]
            scale = HD ** -0.5 if j < H else 1.0
            act_scr[b * tt:(b + 1) * tt, j * HD:(j + 1) * HD] = (
                a * lax.rsqrt(jnp.sum(a * a, axis=-1, keepdims=True) + EPS) * scale)
        act_scr[b * tt:(b + 1) * tt, 2 * H * HD:] = act[:, 2 * H * HD:]
        tail = xp_scr[b, tt:tt + 8, :]
        xp_scr[b, 0:8, :] = tail

    row = _iota2((c, c), 0)
    col = _iota2((c, c), 1)
    same_seq = (row // sl) == (col // sl)
    incl = (row >= col) & same_seq
    strict = (row > col) & same_seq
    cumsum, mxu = _chunk_ops(incl, c)
    eye_l = (_iota2((LANES, LANES), 0) == _iota2((LANES, LANES), 1)).astype(BF16)
    transpose = lambda m: _sel_dot_nt(eye_l, m)
    if spc > 1:
        seq_last = (col == (row // sl) * sl + (sl - 1)).astype(BF16)
    narrow = mxu if spc == 1 else (lambda a: a)

    heads = [slice(h * HD, (h + 1) * HD) for h in range(H)]

    def group(idx, carry):
        qd, ktail, dtot, lmat, amat, rwu, r0s, sqs = [], [], [], [], [], [], [], []
        for j in range(nu):
            r0 = _chunk_row(idx, j, nu, c)
            gc = cumsum(la_scr[pl.ds(r0, c), :])
            gt = transpose(gc)
            be = be_scr[pl.ds(r0, c), :]
            eg = jnp.exp(gc)
            gl = gc[c - 1:c, :] if spc == 1 else _sel_dot(seq_last, gc)
            et = jnp.exp(gl - gc)
            dt = jnp.exp(gl)
            qq = act_scr[pl.ds(r0, c), 0:H * HD]
            kk = act_scr[pl.ds(r0, c), H * HD:2 * H * HD]
            vv = act_scr[pl.ds(r0, c), 2 * H * HD:]
            qd_j, ktail_j, dtot_j = [], [], []
            for h, s in enumerate(heads):
                g_col = gc[:, 8 + h:9 + h]
                eg_col = eg[:, 8 + h:9 + h]
                be_col = be[:, h:h + 1]
                kh = mxu(kk[:, s])
                qk = _dot_nt(jnp.concatenate([mxu(qq[:, s]), kh], axis=0), kh)
                ed = jnp.exp(g_col - gt[8 + h:9 + h, :])
                lmat.append(be_col * qk[c:] * jnp.where(strict, ed, 0.0))
                amat.append(mxu(qk[:c] * jnp.where(incl, ed, 0.0)))
                rwu.append(mxu(jnp.concatenate([kk[:, s] * (be_col * eg_col), vv[:, s] * be_col], axis=1)))
                qd_j.append(narrow(qq[:, s] * eg_col))
                ktail_j.append(narrow(kk[:, s] * et[:, 8 + h:9 + h]))
                dtot_j.append(dt[:, 8 + h:9 + h])
            qd.append(qd_j)
            ktail.append(ktail_j)
            dtot.append(dtot_j)
            r0s.append(r0)
            sqs.append(_chunk_seq(idx, j, nu, ncs) if spc == 1 else (idx * nu + j) * spc)
        tinv = _unit_lower_inv(lmat, c, mxu, span=sl)
        wu = [_dot(mxu(t_), r_) for t_, r_ in zip(tinv, rwu)]

        if spc > 1:
            outs, finals = [], []
            for j in range(nu):
                o_j = []
                for h in range(H):
                    wuh = wu[j * H + h]
                    us, oins = [], []
                    for q in range(spc):
                        rs = slice(q * sl, (q + 1) * sl)
                        s_old = s_scr[sqs[j] + q, h]
                        qw = _dot(mxu(jnp.concatenate([qd[j][h][rs], wuh[rs, :HD]], axis=0)), mxu(s_old))
                        u = wuh[rs, HD:] - qw[sl:]
                        us.append(u)
                        oins.append(qw[:sl])
                        finals.append((sqs[j] + q, h,
                                       s_old * dtot[j][h][q * sl:q * sl + 1] + _dot_tn(ktail[j][h][rs], u)))
                    o_j.append(jnp.concatenate(oins, axis=0)
                               + _dot(amat[j * H + h], mxu(jnp.concatenate(us, axis=0))))
                outs.append(o_j)
            for j in range(nu):
                o_scr[pl.ds(r0s[j], c), :] = jnp.concatenate(outs[j], axis=1)
            for sq, h, s_new in finals:
                s_scr[sq, h] = s_new
            return carry

        own_state = ncs == 1
        loaded = [[s_scr[sqs[j], h] for h in range(H)] for j in range(nu if own_state else 1)]
        outs, finals = [], []
        st = loaded[0]
        for j in range(nu):
            if own_state:
                st = loaded[j]
            o_j, st_next = [], []
            for h in range(H):
                wuh = wu[j * H + h]
                qw = _dot(jnp.concatenate([qd[j][h], mxu(wuh[:, :HD])], axis=0), mxu(st[h]))
                u = wuh[:, HD:] - qw[c:]
                ub = mxu(u)
                o_j.append(qw[:c] + _dot(amat[j * H + h], ub))
                st_next.append(st[h] * dtot[j][h] + _dot_tn(ktail[j][h], ub))
            st = st_next
            outs.append(o_j)
            if own_state or j == nu - 1:
                finals.append((sqs[j], st))
        for j in range(nu):
            o_scr[pl.ds(r0s[j], c), :] = jnp.concatenate(outs[j], axis=1)
        for sq, st in finals:
            for h in range(H):
                s_scr[sq, h] = st[h]
        return carry

    _run_groups(group, rows // (nu * c))

    on = _head_rms_gate(o_scr[...], gn_ref[...], z_scr[...])
    o_ref[...] = x + _dot(on.astype(BF16), wo_ref[...])

    @pl.when(t == pl.num_programs(1) - 1)
    def _():
        sout_ref[...] = s_scr[...]
        for b in range(bb):
            cout_ref[b] = xp_scr[b, 8 - (CONV - 1):8, :]


def _gdn_call(x2d, s0, c0, nw, win, cw, par, gn, wo, *, nseq, seqlen, bb, tt, c, nu=2, row0=0,
              bcast_state=False):
    assert bb == 1 or tt == seqlen
    assert nseq % bb == 0 and seqlen % tt == 0 and tt % 8 == 0 and row0 % (bb * tt) == 0
    assert tt % c == 0 or (c % tt == 0 and (bb * tt) % c == 0)
    rows = bb * tt
    nt = seqlen // tt
    blk0 = row0 // rows
    s_idx = (lambda b, t: (0, 0, 0, 0)) if bcast_state else (lambda b, t: (b, 0, 0, 0))
    c_idx = (lambda b, t: (0, 0, 0)) if bcast_state else (lambda b, t: (b, 0, 0))
    slab = pltpu.VMEM((rows, D), F32)
    return pl.pallas_call(
        functools.partial(_gdn_kernel, bb=bb, tt=tt, c=c, nu=min(nu, bb * tt // c)),
        grid=(nseq // bb, nt),
        in_specs=[
            pl.BlockSpec((rows, D), lambda b, t: (blk0 + b * nt + t, 0)),
            pl.BlockSpec((bb, H, HD, HD), s_idx),
            pl.BlockSpec((bb, CONV - 1, QKV), c_idx),
            _const_spec((1, D)),
            _const_spec(win.shape),
            _const_spec((CONV, QKV)),
            _const_spec((2, LANES)),
            _const_spec((1, HD)),
            _const_spec((D, D)),
        ],
        out_specs=[
            pl.BlockSpec((rows, D), lambda b, t: (b * nt + t, 0)),
            pl.BlockSpec((bb, H, HD, HD), lambda b, t: (b, 0, 0, 0)),
            pl.BlockSpec((bb, CONV - 1, QKV), lambda b, t: (b, 0, 0)),
        ],
        out_shape=[
            jax.ShapeDtypeStruct((nseq * seqlen, D), F32),
            jax.ShapeDtypeStruct((nseq, H, HD, HD), F32),
            jax.ShapeDtypeStruct((nseq, CONV - 1, QKV), F32),
        ],
        scratch_shapes=[
            pltpu.VMEM((bb, H, HD, HD), F32),
            pltpu.VMEM((bb, tt + 8, QKV), F32),
            pltpu.VMEM((rows, QKV), F32),
            slab,
            pltpu.VMEM((rows, LANES), F32),
            pltpu.VMEM((rows, LANES), F32),
            slab,
        ],
        compiler_params=pltpu.CompilerParams(
            dimension_semantics=("arbitrary", "arbitrary"), vmem_limit_bytes=V7X_VMEM_LIMIT),
        name=f"gdn_mixer_{nseq}x{seqlen}",
    )(x2d, s0, c0, nw, win, cw, par, gn, wo)


def _swiglu(xb, w1_ref, w3_ref, w2_ref):
    acc = None
    for j in range(DFF // FB):
        fs = slice(j * FB, (j + 1) * FB)
        hh = (_silu(_dot(xb, w1_ref[:, fs])) * _dot(xb, w3_ref[:, fs])).astype(BF16)
        part = _dot(hh, w2_ref[fs, :])
        acc = part if acc is None else acc + part
    return acc


def _ffn_kernel(*refs, n_in, nblk_first):
    x_refs = refs[:n_in]
    nw_ref, w1_ref, w3_ref, w2_ref, o_ref = refs[n_in:]
    x = x_refs[0][...]
    if n_in == 2:
        x = jnp.where(pl.program_id(0) < nblk_first, x, x_refs[1][...])
    xb = _rms(x, nw_ref[...]).astype(BF16)
    o_ref[...] = x + _swiglu(xb, w1_ref, w3_ref, w2_ref)


def _two_group_specs(tm, nblk_first, nblk_total):
    last_first = nblk_first - 1
    return [
        pl.BlockSpec((tm, D), lambda i, *_: (jnp.minimum(i, last_first), 0)),
        pl.BlockSpec((tm, D), lambda i, *_: (jnp.maximum(i - nblk_first, 0), 0)),
    ]


def _ffn_call(xs, nw, w1, w3, w2, *, tm):
    nblks = [x.shape[0] // tm for x in xs]
    assert all(x.shape[0] % tm == 0 for x in xs)
    total = sum(nblks)
    if len(xs) == 2:
        x_specs = _two_group_specs(tm, nblks[0], total)
    else:
        x_specs = [pl.BlockSpec((tm, D), lambda i: (i, 0))]
    return pl.pallas_call(
        functools.partial(_ffn_kernel, n_in=len(xs), nblk_first=nblks[0]),
        grid=(total,),
        in_specs=x_specs + [_const_spec((1, D)), _const_spec((D, DFF)), _const_spec((D, DFF)), _const_spec((DFF, D))],
        out_specs=pl.BlockSpec((tm, D), lambda i: (i, 0)),
        out_shape=jax.ShapeDtypeStruct((total * tm, D), F32),
        compiler_params=pltpu.CompilerParams(dimension_semantics=("arbitrary",), vmem_limit_bytes=V7X_VMEM_LIMIT),
        name=f"ffn_dense_{total * tm}",
    )(*xs, nw, w1, w3, w2)


def _router_kernel(xa_ref, xb_ref, nw_ref, wr_ref, gate_ref, lpos_ref, lpost_ref, tab_ref, tot_ref, seg_scr, *,
                   tm, nblk_first):
    i = pl.program_id(0)

    @pl.when(i == 0)
    def _():
        seg_scr[...] = jnp.zeros_like(seg_scr)

    x = jnp.where(i < nblk_first, xa_ref[...], xb_ref[...])
    xn = _rms(x, nw_ref[...])
    logits = _dot_3x(xn, wr_ref[...])
    lane = _iota2((tm, LANES), 1).astype(F32)
    neg = jnp.float32(-jnp.inf)
    l1 = jnp.where(lane < NE, logits, neg)
    m1 = jnp.max(l1, axis=-1, keepdims=True)
    i1 = jnp.min(jnp.where(l1 == m1, lane, float(LANES)), axis=-1, keepdims=True)
    l2 = jnp.where(lane == i1, neg, l1)
    m2 = jnp.max(l2, axis=-1, keepdims=True)
    i2 = jnp.min(jnp.where(l2 == m2, lane, float(LANES)), axis=-1, keepdims=True)
    e2 = jnp.exp(m2 - m1)
    g1 = 1.0 / (1.0 + e2)
    g2 = e2 / (1.0 + e2)

    oh1 = (lane == i1).astype(F32)
    oh2 = (lane == i2).astype(F32)
    below = (_iota2((tm, tm), 0) > _iota2((tm, tm), 1)).astype(BF16)
    c1 = _dot(below, oh1.astype(BF16))
    c2 = _dot(below, oh2.astype(BF16))
    n1 = jnp.sum(oh1, axis=0, keepdims=True)
    n2 = jnp.sum(oh2, axis=0, keepdims=True)
    cnt = jnp.floor((n1 + n2 + (SEG - 1)) * (1.0 / SEG)) * SEG
    before = (_iota2((LANES, LANES), 0) < _iota2((LANES, LANES), 1)).astype(BF16)
    start = _dot(jnp.broadcast_to(cnt, (16, LANES)).astype(BF16), before)[0:1]
    p1 = jnp.sum(oh1 * (start + c1), axis=-1, keepdims=True)
    p2 = jnp.sum(oh2 * (start + n1 + c2), axis=-1, keepdims=True)
    used = seg_scr[...]
    seg_scr[...] = used + cnt
    tot_ref[...] = used + cnt

    gate_ref[:, 0:1] = g1
    gate_ref[:, 1:2] = g2
    lpos_ref[:, 0:1] = p1
    lpos_ref[:, 1:2] = p2
    sub = _iota2((8, LANES), 0)
    tab_ref[...] = jnp.where(sub == 0, cnt, jnp.where(sub == 1, start, jnp.where(sub == 2, used, 0.0)))
    slab = jnp.where(lane == 0.0, p1, jnp.where(lane == 1.0, p2, 0.0))
    eye16 = (_iota2((16, LANES), 0) == _iota2((16, LANES), 1)).astype(BF16)
    lpost_ref[...] = _sel_dot_nt(eye16, slab)[0:8]


def _router_call(xa, xb, nw, wr, *, tm):
    na, nb = xa.shape[0] // tm, xb.shape[0] // tm
    nblk = na + nb
    n = nblk * tm
    return pl.pallas_call(
        functools.partial(_router_kernel, tm=tm, nblk_first=na),
        grid=(nblk,),
        in_specs=_two_group_specs(tm, na, nblk) + [_const_spec((1, D)), _const_spec((D, LANES))],
        out_specs=[
            pl.BlockSpec((tm, 2), lambda i: (i, 0)),
            pl.BlockSpec((tm, 2), lambda i: (i, 0)),
            pl.BlockSpec((8, tm), lambda i: (i, 0)),
            pl.BlockSpec((8, LANES), lambda i: (i, 0)),
            pl.BlockSpec((1, LANES), lambda i: (0, 0)),
        ],
        out_shape=[
            jax.ShapeDtypeStruct((n, 2), F32),
            jax.ShapeDtypeStruct((n, 2), F32),
            jax.ShapeDtypeStruct((nblk * 8, tm), F32),
            jax.ShapeDtypeStruct((nblk * 8, LANES), F32),
            jax.ShapeDtypeStruct((1, LANES), F32),
        ],
        scratch_shapes=[pltpu.VMEM((1, LANES), F32)],
        compiler_params=pltpu.CompilerParams(dimension_semantics=("arbitrary",), vmem_limit_bytes=V7X_VMEM_LIMIT),
        name="moe_router",
    )(xa, xb, nw, wr)


def _local_rows(tm):
    return -(-(2 * tm + NE * (SEG - 1)) // LANES) * LANES


def _segment_copies(tab_ref, tile, local_ref, global_ref, sem, *, to_global, wait):
    for e in range(NE):
        base = (tile * NE + e) * 3
        npiece, lstart, gstart = tab_ref[base], tab_ref[base + 1], tab_ref[base + 2]

        def piece(k, carry, lstart=lstart, gstart=gstart):
            loc = local_ref.at[pl.ds(pl.multiple_of(lstart + k * SEG, SEG), SEG)]
            glo = global_ref.at[pl.ds(pl.multiple_of(gstart + k * SEG, SEG), SEG)]
            cp = pltpu.make_async_copy(loc, glo, sem) if to_global else pltpu.make_async_copy(glo, loc, sem)
            if wait:
                cp.wait()
            else:
                cp.start()
            return carry

        lax.fori_loop(0, npiece, piece, 0)


def _dispatch_kernel(tab_ref, zst_ref, xa_ref, xb_ref, lpost_ref, nw_ref, xs_ref, buf, zero_scr, sem, zsem, *,
                     tm, nblk_first, lr):
    i = pl.program_id(0)
    slot = lax.rem(i, 2)

    @pl.when(i == 0)
    def _():
        zero_scr[...] = jnp.zeros_like(zero_scr)

        def zero_copy(e):
            start = pl.multiple_of(zst_ref[e], TM)
            return pltpu.make_async_copy(zero_scr, xs_ref.at[pl.ds(start, TM)], zsem)

        for e in range(zst_ref.shape[0]):
            pl.when(zst_ref[e] >= 0)(lambda e=e: zero_copy(e).start())
        for e in range(zst_ref.shape[0]):
            pl.when(zst_ref[e] >= 0)(lambda e=e: zero_copy(e).wait())

    x = jnp.where(i < nblk_first, xa_ref[...], xb_ref[...])
    xn = _rms(x, nw_ref[...]).astype(BF16)
    lpt = lpost_ref[...]
    rio = _iota2((lr, tm), 0).astype(F32)
    sel = jnp.where(rio == lpt[0:1, :], 1.0, jnp.where(rio == lpt[1:2, :], 1.0, 0.0)).astype(BF16)
    buf[slot] = _dot(sel, xn)

    copies = functools.partial(_segment_copies, tab_ref, global_ref=xs_ref, to_global=True)
    copies(i, local_ref=buf.at[slot], sem=sem.at[slot], wait=False)
    pl.when(i > 0)(lambda: copies(i - 1, local_ref=buf.at[1 - slot], sem=sem.at[1 - slot], wait=True))
    pl.when(i == pl.num_programs(0) - 1)(
        lambda: copies(i, local_ref=buf.at[slot], sem=sem.at[slot], wait=True))


def _dispatch_call(tab, zst, xa, xb, lpost, nw, *, tm, rmax):
    na, nb = xa.shape[0] // tm, xb.shape[0] // tm
    lr = _local_rows(tm)
    return pl.pallas_call(
        functools.partial(_dispatch_kernel, tm=tm, nblk_first=na, lr=lr),
        grid_spec=pltpu.PrefetchScalarGridSpec(
            num_scalar_prefetch=2,
            grid=(na + nb,),
            in_specs=_two_group_specs(tm, na, na + nb) + [pl.BlockSpec((8, tm), lambda i, *_: (i, 0)),
                                                          pl.BlockSpec((1, D), lambda i, *_: (0, 0))],
            out_specs=pl.BlockSpec(memory_space=pl.ANY),
            scratch_shapes=[pltpu.VMEM((2, lr, D), F32), pltpu.VMEM((TM, D), F32),
                            pltpu.SemaphoreType.DMA((2,)), pltpu.SemaphoreType.DMA],
        ),
        out_shape=jax.ShapeDtypeStruct((rmax, D), F32),
        compiler_params=pltpu.CompilerParams(dimension_semantics=("arbitrary",), vmem_limit_bytes=V7X_VMEM_LIMIT),
        name="moe_dispatch",
    )(tab, zst, xa, xb, lpost, nw)


def _expert_kernel(texp_ref, tsrc_ref, tvalid_ref, xs_ref, w1_ref, w3_ref, w2_ref, ys_ref):
    valid = tvalid_ref[pl.program_id(0)]
    half = TM // 2

    @pl.when(valid > half)
    def _():
        xb = xs_ref[...].astype(BF16)
        ys_ref[...] = _swiglu(xb, w1_ref, w3_ref, w2_ref)

    @pl.when((valid > 0) & (valid <= half))
    def _():
        xb = xs_ref[0:half, :].astype(BF16)
        ys_ref[0:half, :] = _swiglu(xb, w1_ref, w3_ref, w2_ref)
        ys_ref[half:, :] = jnp.zeros((TM - half, D), F32)

    @pl.when(valid == 0)
    def _():
        ys_ref[...] = jnp.zeros_like(ys_ref)


def _expert_call(texp, tsrc, tvalid, xs, w1, w3, w2):
    ntile = xs.shape[0] // TM
    return pl.pallas_call(
        _expert_kernel,
        grid_spec=pltpu.PrefetchScalarGridSpec(
            num_scalar_prefetch=3,
            grid=(ntile,),
            in_specs=[
                pl.BlockSpec((TM, D), lambda i, te, ts, na: (ts[i], 0)),
                pl.BlockSpec((None, D, DFF), lambda i, te, ts, na: (te[i], 0, 0)),
                pl.BlockSpec((None, D, DFF), lambda i, te, ts, na: (te[i], 0, 0)),
                pl.BlockSpec((None, DFF, D), lambda i, te, ts, na: (te[i], 0, 0)),
            ],
            out_specs=pl.BlockSpec((TM, D), lambda i, te, ts, na: (i, 0)),
        ),
        out_shape=jax.ShapeDtypeStruct((xs.shape[0], D), F32),
        compiler_params=pltpu.CompilerParams(dimension_semantics=("arbitrary",), vmem_limit_bytes=V7X_VMEM_LIMIT),
        name="moe_experts",
    )(texp, tsrc, tvalid, xs, w1, w3, w2)


def _combine_kernel(tab_ref, x_ref, gate_ref, lpos_ref, nf_ref, ys_ref, o_ref, buf, sem, *, tm, blk0, lr):
    i = pl.program_id(0)
    slot = lax.rem(i, 2)

    def fetch(step, slot_):
        tile = blk0 + step
        last = (tile * NE + NE - 1) * 3
        used = pl.multiple_of(tab_ref[last + 1] + tab_ref[last] * SEG, SEG)

        def clear(k, carry):
            buf[slot_, pl.ds(pl.multiple_of(used + k * SEG, SEG), SEG), :] = jnp.zeros((SEG, D), F32)
            return carry

        lax.fori_loop(0, (lr - used) // SEG, clear, 0)
        _segment_copies(tab_ref, tile, buf.at[slot_], ys_ref, sem.at[slot_], to_global=False, wait=False)

    pl.when(i == 0)(lambda: fetch(i, slot))
    pl.when(i + 1 < pl.num_programs(0))(lambda: fetch(i + 1, 1 - slot))
    _segment_copies(tab_ref, blk0 + i, buf.at[slot], ys_ref, sem.at[slot], to_global=False, wait=True)

    y = buf[slot]
    y_hi = y.astype(BF16)
    y_lo = (y - y_hi.astype(F32)).astype(BF16)
    lp = lpos_ref[...]
    cio = _iota2((tm, lr), 1).astype(F32)
    g = gate_ref[...]
    moe = None
    for s in range(2):
        sel = jnp.where(cio == lp[:, s:s + 1], 1.0, 0.0).astype(BF16)
        picked = _dot(sel, y_hi) + _dot(sel, y_lo)
        moe = g[:, s:s + 1] * picked if moe is None else moe + g[:, s:s + 1] * picked
    o_ref[...] = _rms(x_ref[...] + moe, nf_ref[...])


def _combine_call(tab, x, gate, lpos, nf, ys, *, tm, tok0):
    n = x.shape[0]
    blk0 = tok0 // tm
    lr = _local_rows(tm)
    return pl.pallas_call(
        functools.partial(_combine_kernel, tm=tm, blk0=blk0, lr=lr),
        grid_spec=pltpu.PrefetchScalarGridSpec(
            num_scalar_prefetch=1,
            grid=(n // tm,),
            in_specs=[
                pl.BlockSpec((tm, D), lambda i, *_: (i, 0)),
                pl.BlockSpec((tm, 2), lambda i, *_: (blk0 + i, 0)),
                pl.BlockSpec((tm, 2), lambda i, *_: (blk0 + i, 0)),
                pl.BlockSpec((1, D), lambda i, *_: (0, 0)),
                pl.BlockSpec(memory_space=pl.ANY),
            ],
            out_specs=pl.BlockSpec((tm, D), lambda i, *_: (i, 0)),
            scratch_shapes=[pltpu.VMEM((2, lr, D), F32), pltpu.SemaphoreType.DMA((2,))],
        ),
        out_shape=jax.ShapeDtypeStruct((n, D), F32),
        compiler_params=pltpu.CompilerParams(dimension_semantics=("arbitrary",), vmem_limit_bytes=V7X_VMEM_LIMIT),
        name=f"moe_combine_{n}",
    )(tab, x, gate, lpos, nf, ys)


def kernel(x_prompt, x_sample, state_hgrn, state_gdn, state_gdn_conv, meta_tokens, norm_mix, norm_ffn, norm_final,
           hg_w_in, hg_lower_bounds, hg_g_norm, hg_w_o, gdn_w_in, gdn_conv_w, gdn_a_log, gdn_dt_bias, gdn_g_norm,
           gdn_w_o, ffn_w1, ffn_w3, ffn_w2, moe_w_router, moe_w1, moe_w3, moe_w2):
    bp, tp, _ = x_prompt.shape
    bs, ts, _ = x_sample.shape
    np_, ns_ = bp * tp, bs * ts
    n_tok = np_ + ns_
    row = lambda v: v.reshape(1, -1).astype(F32)

    hg_win = hg_w_in.astype(BF16)
    hg_wo = hg_w_o.astype(BF16)
    gdn_win = jnp.concatenate(
        [gdn_w_in[:, :QKV + D], gdn_w_in[:, QKV + D:], jnp.zeros((D, LANES - 2 * H), gdn_w_in.dtype)],
        axis=1).astype(BF16)
    gdn_wo = gdn_w_o.astype(BF16)
    par = jnp.zeros((2, LANES), F32)
    par = par.at[0, H:2 * H].set(gdn_dt_bias.astype(F32)).at[1, H:2 * H].set(gdn_a_log.astype(F32))
    w1d, w3d, w2d = ffn_w1.astype(BF16), ffn_w3.astype(BF16), ffn_w2.astype(BF16)
    w1e, w3e, w2e = moe_w1.astype(BF16), moe_w3.astype(BF16), moe_w2.astype(BF16)
    wr = jnp.concatenate([moe_w_router.astype(F32), jnp.zeros((D, LANES - NE), F32)], axis=1)
    nm0, nm1 = row(norm_mix[0]), row(norm_mix[1])
    nf0, nf1 = row(norm_ffn[0]), row(norm_ffn[1])
    nfin = row(norm_final)
    hg_gn, gdn_gn = row(hg_g_norm), row(gdn_g_norm)
    lbnd = hg_lower_bounds.astype(F32)
    cw = gdn_conv_w.astype(F32)

    hg_args = (nm0, hg_win, lbnd, hg_gn, hg_wo)
    gdn_args = (nm1, gdn_win, cw, par, gdn_gn, gdn_wo)
    xm = meta_tokens.astype(F32)
    xp2 = x_prompt.reshape(np_, D)
    xs2 = x_sample.reshape(ns_, D)

    zero_state = jnp.zeros((1, H, HD, HD), F32)
    h1m, hg_m = _hgrn_call(xm, zero_state, *hg_args, nseq=1, seqlen=N_META, bb=1, tt=N_META, c=N_META)
    h1p, hg_p = _hgrn_call(xp2, hg_m, *hg_args, nseq=bp, seqlen=tp, bb=1, tt=256, c=64, nu=4, bcast_state=True)
    h1s, hg_s = _hgrn_call(xs2, state_hgrn.astype(F32), *hg_args, nseq=bs, seqlen=ts, bb=8, tt=ts, c=64, nu=1)
    h2m = _ffn_call([h1m], nf0, w1d, w3d, w2d, tm=N_META)
    h2 = _ffn_call([h1p, h1s], nf0, w1d, w3d, w2d, tm=512)

    zero_conv = jnp.zeros((1, CONV - 1, QKV), F32)
    _, gdn_m, conv_m = _gdn_call(h2m, zero_state, zero_conv, *gdn_args, nseq=1, seqlen=N_META, bb=1, tt=N_META, c=16)
    h3p, gdn_p, conv_p = _gdn_call(h2, gdn_m, conv_m, *gdn_args, nseq=bp, seqlen=tp, bb=1, tt=256, c=128, nu=2,
                                   bcast_state=True)
    h3s, gdn_s, conv_s = _gdn_call(h2, state_gdn.astype(F32), state_gdn_conv.astype(F32), *gdn_args,
                                   nseq=bs, seqlen=ts, bb=8, tt=ts, c=64, nu=1, row0=np_)

    yp, ysm = _moe_final(h3p, h3s, nf1, wr, w1e, w3e, w2e, nfin, tm=256)

    sd, gd, cd = state_hgrn.dtype, state_gdn.dtype, state_gdn_conv.dtype
    return (yp.reshape(bp, tp, D), ysm.reshape(bs, ts, D),
            hg_p.astype(sd), hg_s.astype(sd),
            gdn_p.astype(gd), gdn_s.astype(gd), conv_p.astype(cd), conv_s.astype(cd))


def _moe_final(h3p, h3s, nf1, wr, w1e, w3e, w2e, nfin, *, tm):
    np_ = h3p.shape[0]
    n_tok = np_ + h3s.shape[0]
    nblk = n_tok // tm
    gate, lpos, lpost, tab, tot = _router_call(h3p, h3s, nf1, wr, tm=tm)
    totals = tot[0, :NE].astype(jnp.int32)
    padded = ((totals + TM - 1) // TM) * TM
    ends = jnp.cumsum(padded)
    offs = ends - padded
    tab3 = tab.reshape(nblk, 8, LANES)[:, :3, :NE].astype(jnp.int32)
    seg_tab = jnp.stack([tab3[:, 0] // SEG, tab3[:, 1], tab3[:, 2] + offs[None, :]], axis=-1).reshape(-1)
    ntile = -(-(2 * n_tok + nblk * NE * (SEG - 1) + NE * (TM - 1)) // TM)
    nact = ends[-1] // TM
    tiles = jnp.arange(ntile, dtype=jnp.int32)
    texp_all = jnp.minimum(jnp.sum(tiles[:, None] * TM >= ends[None, :], axis=1), NE - 1).astype(jnp.int32)
    last = jnp.maximum(nact - 1, 0)
    active = tiles < nact
    texp = jnp.where(active, texp_all, texp_all[last]).astype(jnp.int32)
    tsrc = jnp.where(active, tiles, last).astype(jnp.int32)
    tvalid = jnp.where(active, jnp.clip(totals[texp_all] - (tiles * TM - offs[texp_all]), 0, TM), 0).astype(jnp.int32)
    tail = nact + jnp.arange(ntile - 2 * n_tok // TM)
    zst = jnp.concatenate([jnp.where(padded > 0, ends - TM, -1),
                           jnp.where(tail < ntile, tail * TM, -1)]).astype(jnp.int32)
    xs_sorted = _dispatch_call(seg_tab, zst, h3p, h3s, lpost, nf1, tm=tm, rmax=ntile * TM)
    ys = _expert_call(texp, tsrc, tvalid, xs_sorted, w1e, w3e, w2e)
    yp = _combine_call(seg_tab, h3p, gate, lpos, nfin, ys, tm=tm, tok0=0)
    ysm = _combine_call(seg_tab, h3s, gate, lpos, nfin, ys, tm=tm, tok0=np_)
    return yp, ysm
```

```python
import functools

import jax
import jax.numpy as jnp
from jax import lax
from jax.experimental import pallas as pl
from jax.experimental.pallas import tpu as pltpu

D = 1024
H = 8
HD = 128
QKV = 3 * H * HD
CONV = 4
DFF = 3584
NE = 8
N_META = 16
EPS = 1e-6
LANES = 128
FB = 512
TM = 512
HG_SUB = 16
SEG = 8
V7X_VMEM_LIMIT = 56 * 1024 * 1024

F32 = jnp.float32
BF16 = jnp.bfloat16


def _dot(a, b):
    return jnp.dot(a, b, preferred_element_type=F32)


def _dot_nt(a, b):
    return lax.dot_general(a, b, (((1,), (1,)), ((), ())), preferred_element_type=F32)


def _dot_tn(a, b):
    return lax.dot_general(a, b, (((0,), (0,)), ((), ())), preferred_element_type=F32)


def _split3(x):
    x1 = x.astype(BF16)
    r1 = x - x1.astype(F32)
    x2 = r1.astype(BF16)
    x3 = (r1 - x2.astype(F32)).astype(BF16)
    return x1, x2, x3


def _sel_dot(sel, x):
    x1, x2, x3 = _split3(x)
    return _dot(sel, x1) + _dot(sel, x2) + _dot(sel, x3)


def _dot_3x(a, b):
    a_hi = a.astype(BF16)
    a_lo = (a - a_hi.astype(F32)).astype(BF16)
    b_hi = b.astype(BF16)
    b_lo = (b - b_hi.astype(F32)).astype(BF16)
    return _dot(a_hi, b_hi) + (_dot(a_hi, b_lo) + _dot(a_lo, b_hi))


def _sel_dot_nt(sel, x):
    x1, x2, x3 = _split3(x)
    return _dot_nt(sel, x1) + _dot_nt(sel, x2) + _dot_nt(sel, x3)


def _rms(x, g):
    return x * lax.rsqrt(jnp.mean(x * x, axis=-1, keepdims=True) + EPS) * g


def _silu(x):
    return x * jax.nn.sigmoid(x)


def _iota2(shape, dim):
    return lax.broadcasted_iota(jnp.int32, shape, dim)


def _head_rms_gate(o, g, gate):
    parts = []
    for h in range(H):
        oh = o[:, h * HD:(h + 1) * HD]
        parts.append(_rms(oh, g))
    return jnp.concatenate(parts, axis=1) * gate


def _chunk_row(idx, j, nu, c):
    if isinstance(idx, int):
        return (idx * nu + j) * c
    return pl.multiple_of((idx * nu + j) * c, c)


def _chunk_seq(idx, j, nu, ncs):
    return (idx * nu + j) // ncs


def _chunk_ops(tri, c):
    assert c % 16 == 0
    tri_b = tri.astype(BF16)
    return (lambda x: _sel_dot(tri_b, x)), (lambda x: x.astype(BF16))


def _run_groups(group, n):
    if n == 1:
        group(0, 0)
    else:
        lax.fori_loop(0, n, group, 0)


def _const_spec(shape):
    nd = len(shape)
    return pl.BlockSpec(shape, lambda *_: (0,) * nd, pipeline_mode=pl.Buffered(1))


def _hgrn_kernel(x_ref, s0_ref, nw_ref, win_ref, lb_ref, gn_ref, wo_ref,
                 o_ref, sout_ref,
                 st_scr, q_scr, k_scr, lf_scr, v_scr, g_scr, *, bb, tt, c, nu):
    t = pl.program_id(1)
    rows = bb * tt
    sl = min(c, tt)
    spc = c // sl
    ncs = max(tt // c, 1)
    assert c % sl == 0 and (ncs == 1 or ncs % nu == 0)

    @pl.when(t == 0)
    def _():
        for b in range(bb):
            for h in range(H):
                st_scr[b, h] = s0_ref[b, h].T

    x = x_ref[...]
    xn = _rms(x, nw_ref[...]).astype(BF16)
    lbn = lb_ref[...]
    e = jnp.exp(lbn - jnp.max(lbn, axis=0, keepdims=True))
    lb = e[0:1] / jnp.sum(e, axis=0, keepdims=True)
    q_scr[...] = _silu(_dot(xn, win_ref[:, 0:D])) * HD ** -0.5
    f = lb + (1.0 - lb) * jax.nn.sigmoid(_dot(xn, win_ref[:, D:2 * D]))
    k_scr[...] = 1.0 - f
    lf_scr[...] = jnp.log(f)
    v_scr[...] = _dot(xn, win_ref[:, 2 * D:3 * D])
    g_scr[...] = _silu(_dot(xn, win_ref[:, 3 * D:4 * D]))

    row = _iota2((c, c), 0)
    col = _iota2((c, c), 1)
    tri = (row >= col) & ((row // sl) == (col // sl))
    cumsum, mxu = _chunk_ops(tri, c)
    heads = [slice(h * HD, (h + 1) * HD) for h in range(H)]
    nsub = c // HG_SUB if sl > HG_SUB else 1
    if nsub > 1:
        local_tri = (tri & ((row // HG_SUB) == (col // HG_SUB))).astype(BF16)
        slab_row = _iota2((c, D), 0)
    if spc > 1:
        seq_last = (col == (row // sl) * sl + (sl - 1)).astype(BF16)

    def chunk_terms(r0):
        lf = lf_scr[pl.ds(r0, c), :]
        bc = cumsum(lf)
        tot = bc[c - 1:c, :] if spc == 1 else _sel_dot(seq_last, bc)
        qq = q_scr[pl.ds(r0, c), :]
        kk = k_scr[pl.ds(r0, c), :]
        v32 = v_scr[pl.ds(r0, c), :]
        vv = mxu(v32)
        qd32 = qq * jnp.exp(bc)
        kt32 = kk * jnp.exp(tot - bc)
        qdj = mxu(qd32)
        ktail = mxu(kt32)
        if nsub == 1:
            kinv = mxu(kk * jnp.exp(-bc))
            scj = [_dot_nt(qdj[:, s], kinv[:, s]) for s in heads]
        else:
            bl = _sel_dot(local_tri, lf)
            qloc = mxu(qq * jnp.exp(bl))
            parts = [[] for _ in heads]
            for i in range(nsub):
                lo, hi = i * HG_SUB, (i + 1) * HG_SUB
                ref_pt = bc[lo - 1:lo, :] if i else 0.0
                kci = mxu(jnp.where(slab_row < hi, kk * jnp.exp(ref_pt - bc), 0.0))
                for h, s in enumerate(heads):
                    parts[h].append(_dot_nt(qloc[lo:hi, s], kci[:, s]))
            scj = [jnp.concatenate(p, axis=0) for p in parts]
        scj = [mxu(jnp.where(tri, sc, 0.0)) for sc in scj]
        oij = [_dot(scj[h], vv[:, heads[h]]) for h in range(H)]
        if spc > 1:
            return qd32, jnp.exp(tot), (v32, kt32), oij
        kvj = [_dot_tn(vv[:, s], ktail[:, s]) for s in heads]
        return qdj, jnp.exp(tot), kvj, oij

    def group(idx, carry):
        qd, dec, kv, oi, r0s, sqs = [], [], [], [], [], []
        for j in range(nu):
            r0 = _chunk_row(idx, j, nu, c)
            qdj, decj, kvj, oij = chunk_terms(r0)
            kv.append(kvj)
            oi.append(oij)
            qd.append(qdj)
            dec.append(decj)
            r0s.append(r0)
            sqs.append(_chunk_seq(idx, j, nu, ncs) if spc == 1 else (idx * nu + j) * spc)

        if spc > 1:
            outs, finals = [], []
            for j in range(nu):
                v32, kt32 = kv[j]
                o_j = []
                for h, s in enumerate(heads):
                    parts = []
                    for q in range(spc):
                        rs = slice(q * sl, (q + 1) * sl)
                        st = st_scr[sqs[j] + q, h]
                        parts.append(oi[j][h][rs] + _dot_nt(qd[j][rs, s], st))
                        finals.append((sqs[j] + q, h,
                                       st * dec[j][q * sl:q * sl + 1, s] + _dot_tn(v32[rs, s], kt32[rs, s])))
                    o_j.append(jnp.concatenate(parts, axis=0))
                outs.append(o_j)
            for j in range(nu):
                q_scr[pl.ds(r0s[j], c), :] = jnp.concatenate(outs[j], axis=1)
            for sq, h, st_new in finals:
                st_scr[sq, h] = st_new
            return carry

        own_state = ncs == 1
        loaded = [[st_scr[sqs[j], h] for h in range(H)] for j in range(nu if own_state else 1)]
        outs, finals = [], []
        st = loaded[0]
        for j in range(nu):
            if own_state:
                st = loaded[j]
            outs.append([oi[j][h] + _dot_nt(qd[j][:, heads[h]], mxu(st[h])) for h in range(H)])
            st = [st[h] * dec[j][:, heads[h]] + kv[j][h] for h in range(H)]
            if own_state or j == nu - 1:
                finals.append((sqs[j], st))
        for j in range(nu):
            q_scr[pl.ds(r0s[j], c), :] = jnp.concatenate(outs[j], axis=1)
        for sq, st in finals:
            for h in range(H):
                st_scr[sq, h] = st[h]
        return carry

    _run_groups(group, rows // (nu * c))

    on = _head_rms_gate(q_scr[...], gn_ref[...], g_scr[...])
    o_ref[...] = x + _dot(on.astype(BF16), wo_ref[...])

    @pl.when(t == pl.num_programs(1) - 1)
    def _():
        for b in range(bb):
            for h in range(H):
                sout_ref[b, h] = st_scr[b, h].T


def _hgrn_call(x2d, s0, nw, win, lbnd, gn, wo, *, nseq, seqlen, bb, tt, c, nu=4, row0=0, bcast_state=False):
    assert bb == 1 or tt == seqlen
    assert nseq % bb == 0 and seqlen % tt == 0 and row0 % (bb * tt) == 0
    assert tt % c == 0 or (c % tt == 0 and (bb * tt) % c == 0)
    rows = bb * tt
    nu = min(nu, rows // c)
    nt = seqlen // tt
    blk0 = row0 // rows
    s_idx = (lambda b, t: (0, 0, 0, 0)) if bcast_state else (lambda b, t: (b, 0, 0, 0))
    slab = pltpu.VMEM((rows, D), F32)
    return pl.pallas_call(
        functools.partial(_hgrn_kernel, bb=bb, tt=tt, c=c, nu=nu),
        grid=(nseq // bb, nt),
        in_specs=[
            pl.BlockSpec((rows, D), lambda b, t: (blk0 + b * nt + t, 0)),
            pl.BlockSpec((bb, H, HD, HD), s_idx),
            _const_spec((1, D)),
            _const_spec((D, 4 * D)),
            _const_spec(lbnd.shape),
            _const_spec((1, HD)),
            _const_spec((D, D)),
        ],
        out_specs=[
            pl.BlockSpec((rows, D), lambda b, t: (b * nt + t, 0)),
            pl.BlockSpec((bb, H, HD, HD), lambda b, t: (b, 0, 0, 0)),
        ],
        out_shape=[
            jax.ShapeDtypeStruct((nseq * seqlen, D), F32),
            jax.ShapeDtypeStruct((nseq, H, HD, HD), F32),
        ],
        scratch_shapes=[pltpu.VMEM((bb, H, HD, HD), F32), slab, slab, slab, slab, slab],
        compiler_params=pltpu.CompilerParams(
            dimension_semantics=("arbitrary", "arbitrary"), vmem_limit_bytes=V7X_VMEM_LIMIT),
        name=f"hgrn_mixer_{nseq}x{seqlen}",
    )(x2d, s0, nw, win, lbnd, gn, wo)


def _neumann_inv(ls, eye, levels, mxu):
    xs = [eye - l for l in ls]
    ps = [mxu(l) for l in ls]
    for _ in range(levels - 1):
        ps = [mxu(_dot(p, p)) for p in ps]
        xs = [x + _dot(mxu(x), p) for x, p in zip(xs, ps)]
    return xs


def _unit_lower_inv(ls, c, mxu, span=None):
    span = c if span is None else span
    row = _iota2((c, c), 0)
    col = _iota2((c, c), 1)
    eye = (row == col).astype(F32)
    if span <= 16:
        return _neumann_inv(ls, eye, span.bit_length() - 1, mxu)
    sub = 16
    nblk = c // sub
    assert c == nblk * sub and nblk & (nblk - 1) == 0
    diag = (row // sub) == (col // sub)
    lds = [jnp.where(diag, l, 0.0) for l in ls]
    dinvs = [mxu(d) for d in _neumann_inv(lds, eye, 4, mxu)]
    ms = [_dot(d, mxu(l - ld)) for d, l, ld in zip(dinvs, ls, lds)]
    ys = _neumann_inv(ms, eye, nblk.bit_length() - 1, mxu)
    return [_dot(mxu(y), d) for y, d in zip(ys, dinvs)]


def _gdn_kernel(x_ref, s0_ref, c0_ref, nw_ref, win_ref, cw_ref, par_ref, gn_ref, wo_ref,
                o_ref, sout_ref, cout_ref,
                s_scr, xp_scr, act_scr, z_scr, la_scr, be_scr, o_scr, *, bb, tt, c, nu):
    t = pl.program_id(1)
    rows = bb * tt
    sl = min(c, tt)
    spc = c // sl
    ncs = max(tt // c, 1)
    assert c % sl == 0 and (ncs == 1 or ncs % nu == 0)

    @pl.when(t == 0)
    def _():
        s_scr[...] = s0_ref[...]
        for b in range(bb):
            xp_scr[b, 8 - (CONV - 1):8, :] = c0_ref[b]

    x = x_ref[...]
    xn = _rms(x, nw_ref[...]).astype(BF16)
    pre = _dot(xn, win_ref[:, 0:QKV])
    for b in range(bb):
        xp_scr[b, 8:8 + tt, :] = pre[b * tt:(b + 1) * tt]
    z_scr[...] = _silu(_dot(xn, win_ref[:, QKV:QKV + D]))
    ba = _dot(xn, win_ref[:, QKV + D:QKV + D + LANES])
    par = par_ref[...]
    sp = ba + par[0:1]
    softplus = jnp.maximum(sp, 0.0) + jnp.log(1.0 + jnp.exp(-jnp.abs(sp)))
    la_scr[...] = -jnp.exp(par[1:2]) * softplus
    be_scr[...] = jax.nn.sigmoid(ba)

    cw = cw_ref[...]
    for b in range(bb):
        conv = xp_scr[b, 8 - (CONV - 1):8 - (CONV - 1) + tt, :] * cw[0:1]
        for w in range(1, CONV):
            conv = conv + xp_scr[b, 8 - (CONV - 1) + w:8 - (CONV - 1) + w + tt, :] * cw[w:w + 1]
        act = _silu(conv)
        for j in range(2 * H):
            a = act[:, j * HD:(j + 1) * HD]
            scale = HD ** -0.5 if j < H else 1.0
            act_scr[b * tt:(b + 1) * tt, j * HD:(j + 1) * HD] = (
                a * lax.rsqrt(jnp.sum(a * a, axis=-1, keepdims=True) + EPS) * scale)
        act_scr[b * tt:(b + 1) * tt, 2 * H * HD:] = act[:, 2 * H * HD:]
        tail = xp_scr[b, tt:tt + 8, :]
        xp_scr[b, 0:8, :] = tail

    row = _iota2((c, c), 0)
    col = _iota2((c, c), 1)
    same_seq = (row // sl) == (col // sl)
    incl = (row >= col) & same_seq
    strict = (row > col) & same_seq
    cumsum, mxu = _chunk_ops(incl, c)
    eye_l = (_iota2((LANES, LANES), 0) == _iota2((LANES, LANES), 1)).astype(BF16)
    transpose = lambda m: _sel_dot_nt(eye_l, m)
    if spc > 1:
        seq_last = (col == (row // sl) * sl + (sl - 1)).astype(BF16)
    narrow = mxu if spc == 1 else (lambda a: a)

    heads = [slice(h * HD, (h + 1) * HD) for h in range(H)]

    def group(idx, carry):
        qd, ktail, dtot, lmat, amat, rwu, r0s, sqs = [], [], [], [], [], [], [], []
        for j in range(nu):
            r0 = _chunk_row(idx, j, nu, c)
            gc = cumsum(la_scr[pl.ds(r0, c), :])
            gt = transpose(gc)
            be = be_scr[pl.ds(r0, c), :]
            eg = jnp.exp(gc)
            gl = gc[c - 1:c, :] if spc == 1 else _sel_dot(seq_last, gc)
            et = jnp.exp(gl - gc)
            dt = jnp.exp(gl)
            qq = act_scr[pl.ds(r0, c), 0:H * HD]
            kk = act_scr[pl.ds(r0, c), H * HD:2 * H * HD]
            vv = act_scr[pl.ds(r0, c), 2 * H * HD:]
            qd_j, ktail_j, dtot_j = [], [], []
            for h, s in enumerate(heads):
                g_col = gc[:, 8 + h:9 + h]
                eg_col = eg[:, 8 + h:9 + h]
                be_col = be[:, h:h + 1]
                kh = mxu(kk[:, s])
                qk = _dot_nt(jnp.concatenate([mxu(qq[:, s]), kh], axis=0), kh)
                ed = jnp.exp(g_col - gt[8 + h:9 + h, :])
                lmat.append(be_col * qk[c:] * jnp.where(strict, ed, 0.0))
                amat.append(mxu(qk[:c] * jnp.where(incl, ed, 0.0)))
                rwu.append(mxu(jnp.concatenate([kk[:, s] * (be_col * eg_col), vv[:, s] * be_col], axis=1)))
                qd_j.append(narrow(qq[:, s] * eg_col))
                ktail_j.append(narrow(kk[:, s] * et[:, 8 + h:9 + h]))
                dtot_j.append(dt[:, 8 + h:9 + h])
            qd.append(qd_j)
            ktail.append(ktail_j)
            dtot.append(dtot_j)
            r0s.append(r0)
            sqs.append(_chunk_seq(idx, j, nu, ncs) if spc == 1 else (idx * nu + j) * spc)
        tinv = _unit_lower_inv(lmat, c, mxu, span=sl)
        wu = [_dot(mxu(t_), r_) for t_, r_ in zip(tinv, rwu)]

        if spc > 1:
            outs, finals = [], []
            for j in range(nu):
                o_j = []
                for h in range(H):
                    wuh = wu[j * H + h]
                    us, oins = [], []
                    for q in range(spc):
                        rs = slice(q * sl, (q + 1) * sl)
                        s_old = s_scr[sqs[j] + q, h]
                        qw = _dot(mxu(jnp.concatenate([qd[j][h][rs], wuh[rs, :HD]], axis=0)), mxu(s_old))
                        u = wuh[rs, HD:] - qw[sl:]
                        us.append(u)
                        oins.append(qw[:sl])
                        finals.append((sqs[j] + q, h,
                                       s_old * dtot[j][h][q * sl:q * sl + 1] + _dot_tn(ktail[j][h][rs], u)))
                    o_j.append(jnp.concatenate(oins, axis=0)
                               + _dot(amat[j * H + h], mxu(jnp.concatenate(us, axis=0))))
                outs.append(o_j)
            for j in range(nu):
                o_scr[pl.ds(r0s[j], c), :] = jnp.concatenate(outs[j], axis=1)
            for sq, h, s_new in finals:
                s_scr[sq, h] = s_new
            return carry

        own_state = ncs == 1
        loaded = [[s_scr[sqs[j], h] for h in range(H)] for j in range(nu if own_state else 1)]
        outs, finals = [], []
        st = loaded[0]
        for j in range(nu):
            if own_state:
                st = loaded[j]
            o_j, st_next = [], []
            for h in range(H):
                wuh = wu[j * H + h]
                qw = _dot(jnp.concatenate([qd[j][h], mxu(wuh[:, :HD])], axis=0), mxu(st[h]))
                u = wuh[:, HD:] - qw[c:]
                ub = mxu(u)
                o_j.append(qw[:c] + _dot(amat[j * H + h], ub))
                st_next.append(st[h] * dtot[j][h] + _dot_tn(ktail[j][h], ub))
            st = st_next
            outs.append(o_j)
            if own_state or j == nu - 1:
                finals.append((sqs[j], st))
        for j in range(nu):
            o_scr[pl.ds(r0s[j], c), :] = jnp.concatenate(outs[j], axis=1)
        for sq, st in finals:
            for h in range(H):
                s_scr[sq, h] = st[h]
        return carry

    _run_groups(group, rows // (nu * c))

    on = _head_rms_gate(o_scr[...], gn_ref[...], z_scr[...])
    o_ref[...] = x + _dot(on.astype(BF16), wo_ref[...])

    @pl.when(t == pl.num_programs(1) - 1)
    def _():
        sout_ref[...] = s_scr[...]
        for b in range(bb):
            cout_ref[b] = xp_scr[b, 8 - (CONV - 1):8, :]


def _gdn_call(x2d, s0, c0, nw, win, cw, par, gn, wo, *, nseq, seqlen, bb, tt, c, nu=2, row0=0,
              bcast_state=False):
    assert bb == 1 or tt == seqlen
    assert nseq % bb == 0 and seqlen % tt == 0 and tt % 8 == 0 and row0 % (bb * tt) == 0
    assert tt % c == 0 or (c % tt == 0 and (bb * tt) % c == 0)
    rows = bb * tt
    nt = seqlen // tt
    blk0 = row0 // rows
    s_idx = (lambda b, t: (0, 0, 0, 0)) if bcast_state else (lambda b, t: (b, 0, 0, 0))
    c_idx = (lambda b, t: (0, 0, 0)) if bcast_state else (lambda b, t: (b, 0, 0))
    slab = pltpu.VMEM((rows, D), F32)
    return pl.pallas_call(
        functools.partial(_gdn_kernel, bb=bb, tt=tt, c=c, nu=min(nu, bb * tt // c)),
        grid=(nseq // bb, nt),
        in_specs=[
            pl.BlockSpec((rows, D), lambda b, t: (blk0 + b * nt + t, 0)),
            pl.BlockSpec((bb, H, HD, HD), s_idx),
            pl.BlockSpec((bb, CONV - 1, QKV), c_idx),
            _const_spec((1, D)),
            _const_spec(win.shape),
            _const_spec((CONV, QKV)),
            _const_spec((2, LANES)),
            _const_spec((1, HD)),
            _const_spec((D, D)),
        ],
        out_specs=[
            pl.BlockSpec((rows, D), lambda b, t: (b * nt + t, 0)),
            pl.BlockSpec((bb, H, HD, HD), lambda b, t: (b, 0, 0, 0)),
            pl.BlockSpec((bb, CONV - 1, QKV), lambda b, t: (b, 0, 0)),
        ],
        out_shape=[
            jax.ShapeDtypeStruct((nseq * seqlen, D), F32),
            jax.ShapeDtypeStruct((nseq, H, HD, HD), F32),
            jax.ShapeDtypeStruct((nseq, CONV - 1, QKV), F32),
        ],
        scratch_shapes=[
            pltpu.VMEM((bb, H, HD, HD), F32),
            pltpu.VMEM((bb, tt + 8, QKV), F32),
            pltpu.VMEM((rows, QKV), F32),
            slab,
            pltpu.VMEM((rows, LANES), F32),
            pltpu.VMEM((rows, LANES), F32),
            slab,
        ],
        compiler_params=pltpu.CompilerParams(
            dimension_semantics=("arbitrary", "arbitrary"), vmem_limit_bytes=V7X_VMEM_LIMIT),
        name=f"gdn_mixer_{nseq}x{seqlen}",
    )(x2d, s0, c0, nw, win, cw, par, gn, wo)


def _swiglu(xb, w1_ref, w3_ref, w2_ref):
    acc = None
    for j in range(DFF // FB):
        fs = slice(j * FB, (j + 1) * FB)
        hh = (_silu(_dot(xb, w1_ref[:, fs])) * _dot(xb, w3_ref[:, fs])).astype(BF16)
        part = _dot(hh, w2_ref[fs, :])
        acc = part if acc is None else acc + part
    return acc


def _ffn_kernel(*refs, n_in, nblk_first):
    x_refs = refs[:n_in]
    nw_ref, w1_ref, w3_ref, w2_ref, o_ref = refs[n_in:]
    x = x_refs[0][...]
    if n_in == 2:
        x = jnp.where(pl.program_id(0) < nblk_first, x, x_refs[1][...])
    xb = _rms(x, nw_ref[...]).astype(BF16)
    o_ref[...] = x + _swiglu(xb, w1_ref, w3_ref, w2_ref)


def _two_group_specs(tm, nblk_first, nblk_total):
    last_first = nblk_first - 1
    return [
        pl.BlockSpec((tm, D), lambda i, *_: (jnp.minimum(i, last_first), 0)),
        pl.BlockSpec((tm, D), lambda i, *_: (jnp.maximum(i - nblk_first, 0), 0)),
    ]


def _ffn_call(xs, nw, w1, w3, w2, *, tm):
    nblks = [x.shape[0] // tm for x in xs]
    assert all(x.shape[0] % tm == 0 for x in xs)
    total = sum(nblks)
    if len(xs) == 2:
        x_specs = _two_group_specs(tm, nblks[0], total)
    else:
        x_specs = [pl.BlockSpec((tm, D), lambda i: (i, 0))]
    return pl.pallas_call(
        functools.partial(_ffn_kernel, n_in=len(xs), nblk_first=nblks[0]),
        grid=(total,),
        in_specs=x_specs + [_const_spec((1, D)), _const_spec((D, DFF)), _const_spec((D, DFF)), _const_spec((DFF, D))],
        out_specs=pl.BlockSpec((tm, D), lambda i: (i, 0)),
        out_shape=jax.ShapeDtypeStruct((total * tm, D), F32),
        compiler_params=pltpu.CompilerParams(dimension_semantics=("arbitrary",), vmem_limit_bytes=V7X_VMEM_LIMIT),
        name=f"ffn_dense_{total * tm}",
    )(*xs, nw, w1, w3, w2)


def _router_kernel(xa_ref, xb_ref, nw_ref, wr_ref, gate_ref, lpos_ref, lpost_ref, tab_ref, tot_ref, seg_scr, *,
                   tm, nblk_first):
    i = pl.program_id(0)

    @pl.when(i == 0)
    def _():
        seg_scr[...] = jnp.zeros_like(seg_scr)

    x = jnp.where(i < nblk_first, xa_ref[...], xb_ref[...])
    xn = _rms(x, nw_ref[...])
    logits = _dot_3x(xn, wr_ref[...])
    lane = _iota2((tm, LANES), 1).astype(F32)
    neg = jnp.float32(-jnp.inf)
    l1 = jnp.where(lane < NE, logits, neg)
    m1 = jnp.max(l1, axis=-1, keepdims=True)
    i1 = jnp.min(jnp.where(l1 == m1, lane, float(LANES)), axis=-1, keepdims=True)
    l2 = jnp.where(lane == i1, neg, l1)
    m2 = jnp.max(l2, axis=-1, keepdims=True)
    i2 = jnp.min(jnp.where(l2 == m2, lane, float(LANES)), axis=-1, keepdims=True)
    e2 = jnp.exp(m2 - m1)
    g1 = 1.0 / (1.0 + e2)
    g2 = e2 / (1.0 + e2)

    oh1 = (lane == i1).astype(F32)
    oh2 = (lane == i2).astype(F32)
    below = (_iota2((tm, tm), 0) > _iota2((tm, tm), 1)).astype(BF16)
    c1 = _dot(below, oh1.astype(BF16))
    c2 = _dot(below, oh2.astype(BF16))
    n1 = jnp.sum(oh1, axis=0, keepdims=True)
    n2 = jnp.sum(oh2, axis=0, keepdims=True)
    cnt = jnp.floor((n1 + n2 + (SEG - 1)) * (1.0 / SEG)) * SEG
    before = (_iota2((LANES, LANES), 0) < _iota2((LANES, LANES), 1)).astype(BF16)
    start = _dot(jnp.broadcast_to(cnt, (16, LANES)).astype(BF16), before)[0:1]
    p1 = jnp.sum(oh1 * (start + c1), axis=-1, keepdims=True)
    p2 = jnp.sum(oh2 * (start + n1 + c2), axis=-1, keepdims=True)
    used = seg_scr[...]
    seg_scr[...] = used + cnt
    tot_ref[...] = used + cnt

    gate_ref[:, 0:1] = g1
    gate_ref[:, 1:2] = g2
    lpos_ref[:, 0:1] = p1
    lpos_ref[:, 1:2] = p2
    sub = _iota2((8, LANES), 0)
    tab_ref[...] = jnp.where(sub == 0, cnt, jnp.where(sub == 1, start, jnp.where(sub == 2, used, 0.0)))
    slab = jnp.where(lane == 0.0, p1, jnp.where(lane == 1.0, p2, 0.0))
    eye16 = (_iota2((16, LANES), 0) == _iota2((16, LANES), 1)).astype(BF16)
    lpost_ref[...] = _sel_dot_nt(eye16, slab)[0:8]


def _router_call(xa, xb, nw, wr, *, tm):
    na, nb = xa.shape[0] // tm, xb.shape[0] // tm
    nblk = na + nb
    n = nblk * tm
    return pl.pallas_call(
        functools.partial(_router_kernel, tm=tm, nblk_first=na),
        grid=(nblk,),
        in_specs=_two_group_specs(tm, na, nblk) + [_const_spec((1, D)), _const_spec((D, LANES))],
        out_specs=[
            pl.BlockSpec((tm, 2), lambda i: (i, 0)),
            pl.BlockSpec((tm, 2), lambda i: (i, 0)),
            pl.BlockSpec((8, tm), lambda i: (i, 0)),
            pl.BlockSpec((8, LANES), lambda i: (i, 0)),
            pl.BlockSpec((1, LANES), lambda i: (0, 0)),
        ],
        out_shape=[
            jax.ShapeDtypeStruct((n, 2), F32),
            jax.ShapeDtypeStruct((n, 2), F32),
            jax.ShapeDtypeStruct((nblk * 8, tm), F32),
            jax.ShapeDtypeStruct((nblk * 8, LANES), F32),
            jax.ShapeDtypeStruct((1, LANES), F32),
        ],
        scratch_shapes=[pltpu.VMEM((1, LANES), F32)],
        compiler_params=pltpu.CompilerParams(dimension_semantics=("arbitrary",), vmem_limit_bytes=V7X_VMEM_LIMIT),
        name="moe_router",
    )(xa, xb, nw, wr)


def _local_rows(tm):
    return -(-(2 * tm + NE * (SEG - 1)) // LANES) * LANES


def _segment_copies(tab_ref, tile, local_ref, global_ref, sem, *, to_global, wait):
    for e in range(NE):
        base = (tile * NE + e) * 3
        npiece, lstart, gstart = tab_ref[base], tab_ref[base + 1], tab_ref[base + 2]

        def piece(k, carry, lstart=lstart, gstart=gstart):
            loc = local_ref.at[pl.ds(pl.multiple_of(lstart + k * SEG, SEG), SEG)]
            glo = global_ref.at[pl.ds(pl.multiple_of(gstart + k * SEG, SEG), SEG)]
            cp = pltpu.make_async_copy(loc, glo, sem) if to_global else pltpu.make_async_copy(glo, loc, sem)
            if wait:
                cp.wait()
            else:
                cp.start()
            return carry

        lax.fori_loop(0, npiece, piece, 0)


def _dispatch_kernel(tab_ref, zst_ref, xa_ref, xb_ref, lpost_ref, nw_ref, xs_ref, buf, zero_scr, sem, zsem, *,
                     tm, nblk_first, lr):
    i = pl.program_id(0)
    slot = lax.rem(i, 2)

    @pl.when(i == 0)
    def _():
        zero_scr[...] = jnp.zeros_like(zero_scr)

        def zero_copy(e):
            start = pl.multiple_of(zst_ref[e], TM)
            return pltpu.make_async_copy(zero_scr, xs_ref.at[pl.ds(start, TM)], zsem)

        for e in range(zst_ref.shape[0]):
            pl.when(zst_ref[e] >= 0)(lambda e=e: zero_copy(e).start())
        for e in range(zst_ref.shape[0]):
            pl.when(zst_ref[e] >= 0)(lambda e=e: zero_copy(e).wait())

    x = jnp.where(i < nblk_first, xa_ref[...], xb_ref[...])
    xn = _rms(x, nw_ref[...]).astype(BF16)
    lpt = lpost_ref[...]
    rio = _iota2((lr, tm), 0).astype(F32)
    sel = jnp.where(rio == lpt[0:1, :], 1.0, jnp.where(rio == lpt[1:2, :], 1.0, 0.0)).astype(BF16)
    buf[slot] = _dot(sel, xn)

    copies = functools.partial(_segment_copies, tab_ref, global_ref=xs_ref, to_global=True)
    copies(i, local_ref=buf.at[slot], sem=sem.at[slot], wait=False)
    pl.when(i > 0)(lambda: copies(i - 1, local_ref=buf.at[1 - slot], sem=sem.at[1 - slot], wait=True))
    pl.when(i == pl.num_programs(0) - 1)(
        lambda: copies(i, local_ref=buf.at[slot], sem=sem.at[slot], wait=True))


def _dispatch_call(tab, zst, xa, xb, lpost, nw, *, tm, rmax):
    na, nb = xa.shape[0] // tm, xb.shape[0] // tm
    lr = _local_rows(tm)
    return pl.pallas_call(
        functools.partial(_dispatch_kernel, tm=tm, nblk_first=na, lr=lr),
        grid_spec=pltpu.PrefetchScalarGridSpec(
            num_scalar_prefetch=2,
            grid=(na + nb,),
            in_specs=_two_group_specs(tm, na, na + nb) + [pl.BlockSpec((8, tm), lambda i, *_: (i, 0)),
                                                          pl.BlockSpec((1, D), lambda i, *_: (0, 0))],
            out_specs=pl.BlockSpec(memory_space=pl.ANY),
            scratch_shapes=[pltpu.VMEM((2, lr, D), F32), pltpu.VMEM((TM, D), F32),
                            pltpu.SemaphoreType.DMA((2,)), pltpu.SemaphoreType.DMA],
        ),
        out_shape=jax.ShapeDtypeStruct((rmax, D), F32),
        compiler_params=pltpu.CompilerParams(dimension_semantics=("arbitrary",), vmem_limit_bytes=V7X_VMEM_LIMIT),
        name="moe_dispatch",
    )(tab, zst, xa, xb, lpost, nw)


def _expert_kernel(texp_ref, tsrc_ref, tvalid_ref, xs_ref, w1_ref, w3_ref, w2_ref, ys_ref):
    valid = tvalid_ref[pl.program_id(0)]
    half = TM // 2

    @pl.when(valid > half)
    def _():
        xb = xs_ref[...].astype(BF16)
        ys_ref[...] = _swiglu(xb, w1_ref, w3_ref, w2_ref)

    @pl.when((valid > 0) & (valid <= half))
    def _():
        xb = xs_ref[0:half, :].astype(BF16)
        ys_ref[0:half, :] = _swiglu(xb, w1_ref, w3_ref, w2_ref)
        ys_ref[half:, :] = jnp.zeros((TM - half, D), F32)

    @pl.when(valid == 0)
    def _():
        ys_ref[...] = jnp.zeros_like(ys_ref)


def _expert_call(texp, tsrc, tvalid, xs, w1, w3, w2):
    ntile = xs.shape[0] // TM
    return pl.pallas_call(
        _expert_kernel,
        grid_spec=pltpu.PrefetchScalarGridSpec(
            num_scalar_prefetch=3,
            grid=(ntile,),
            in_specs=[
                pl.BlockSpec((TM, D), lambda i, te, ts, na: (ts[i], 0)),
                pl.BlockSpec((None, D, DFF), lambda i, te, ts, na: (te[i], 0, 0)),
                pl.BlockSpec((None, D, DFF), lambda i, te, ts, na: (te[i], 0, 0)),
                pl.BlockSpec((None, DFF, D), lambda i, te, ts, na: (te[i], 0, 0)),
            ],
            out_specs=pl.BlockSpec((TM, D), lambda i, te, ts, na: (i, 0)),
        ),
        out_shape=jax.ShapeDtypeStruct((xs.shape[0], D), F32),
        compiler_params=pltpu.CompilerParams(dimension_semantics=("arbitrary",), vmem_limit_bytes=V7X_VMEM_LIMIT),
        name="moe_experts",
    )(texp, tsrc, tvalid, xs, w1, w3, w2)


def _combine_kernel(tab_ref, x_ref, gate_ref, lpos_ref, nf_ref, ys_ref, o_ref, buf, sem, *, tm, blk0, lr):
    i = pl.program_id(0)
    slot = lax.rem(i, 2)

    def fetch(step, slot_):
        tile = blk0 + step
        last = (tile * NE + NE - 1) * 3
        used = pl.multiple_of(tab_ref[last + 1] + tab_ref[last] * SEG, SEG)

        def clear(k, carry):
            buf[slot_, pl.ds(pl.multiple_of(used + k * SEG, SEG), SEG), :] = jnp.zeros((SEG, D), F32)
            return carry

        lax.fori_loop(0, (lr - used) // SEG, clear, 0)
        _segment_copies(tab_ref, tile, buf.at[slot_], ys_ref, sem.at[slot_], to_global=False, wait=False)

    pl.when(i == 0)(lambda: fetch(i, slot))
    pl.when(i + 1 < pl.num_programs(0))(lambda: fetch(i + 1, 1 - slot))
    _segment_copies(tab_ref, blk0 + i, buf.at[slot], ys_ref, sem.at[slot], to_global=False, wait=True)

    yb = buf[slot].astype(BF16)
    lp = lpos_ref[...]
    cio = _iota2((tm, lr), 1).astype(F32)
    g = gate_ref[...]
    moe = None
    for s in range(2):
        sel = jnp.where(cio == lp[:, s:s + 1], 1.0, 0.0).astype(BF16)
        picked = _dot(sel, yb)
        moe = g[:, s:s + 1] * picked if moe is None else moe + g[:, s:s + 1] * picked
    o_ref[...] = _rms(x_ref[...] + moe, nf_ref[...])


def _combine_call(tab, x, gate, lpos, nf, ys, *, tm, tok0):
    n = x.shape[0]
    blk0 = tok0 // tm
    lr = _local_rows(tm)
    return pl.pallas_call(
        functools.partial(_combine_kernel, tm=tm, blk0=blk0, lr=lr),
        grid_spec=pltpu.PrefetchScalarGridSpec(
            num_scalar_prefetch=1,
            grid=(n // tm,),
            in_specs=[
                pl.BlockSpec((tm, D), lambda i, *_: (i, 0)),
                pl.BlockSpec((tm, 2), lambda i, *_: (blk0 + i, 0)),
                pl.BlockSpec((tm, 2), lambda i, *_: (blk0 + i, 0)),
                pl.BlockSpec((1, D), lambda i, *_: (0, 0)),
                pl.BlockSpec(memory_space=pl.ANY),
            ],
            out_specs=pl.BlockSpec((tm, D), lambda i, *_: (i, 0)),
            scratch_shapes=[pltpu.VMEM((2, lr, D), F32), pltpu.SemaphoreType.DMA((2,))],
        ),
        out_shape=jax.ShapeDtypeStruct((n, D), F32),
        compiler_params=pltpu.CompilerParams(dimension_semantics=("arbitrary",), vmem_limit_bytes=V7X_VMEM_LIMIT),
        name=f"moe_combine_{n}",
    )(tab, x, gate, lpos, nf, ys)


def kernel(x_prompt, x_sample, state_hgrn, state_gdn, state_gdn_conv, meta_tokens, norm_mix, norm_ffn, norm_final,
           hg_w_in, hg_lower_bounds, hg_g_norm, hg_w_o, gdn_w_in, gdn_conv_w, gdn_a_log, gdn_dt_bias, gdn_g_norm,
           gdn_w_o, ffn_w1, ffn_w3, ffn_w2, moe_w_router, moe_w1, moe_w3, moe_w2):
    bp, tp, _ = x_prompt.shape
    bs, ts, _ = x_sample.shape
    np_, ns_ = bp * tp, bs * ts
    n_tok = np_ + ns_
    row = lambda v: v.reshape(1, -1).astype(F32)

    hg_win = hg_w_in.astype(BF16)
    hg_wo = hg_w_o.astype(BF16)
    gdn_win = jnp.concatenate(
        [gdn_w_in[:, :QKV + D], gdn_w_in[:, QKV + D:], jnp.zeros((D, LANES - 2 * H), gdn_w_in.dtype)],
        axis=1).astype(BF16)
    gdn_wo = gdn_w_o.astype(BF16)
    par = jnp.zeros((2, LANES), F32)
    par = par.at[0, H:2 * H].set(gdn_dt_bias.astype(F32)).at[1, H:2 * H].set(gdn_a_log.astype(F32))
    w1d, w3d, w2d = ffn_w1.astype(BF16), ffn_w3.astype(BF16), ffn_w2.astype(BF16)
    w1e, w3e, w2e = moe_w1.astype(BF16), moe_w3.astype(BF16), moe_w2.astype(BF16)
    wr = jnp.concatenate([moe_w_router.astype(F32), jnp.zeros((D, LANES - NE), F32)], axis=1)
    nm0, nm1 = row(norm_mix[0]), row(norm_mix[1])
    nf0, nf1 = row(norm_ffn[0]), row(norm_ffn[1])
    nfin = row(norm_final)
    hg_gn, gdn_gn = row(hg_g_norm), row(gdn_g_norm)
    lbnd = hg_lower_bounds.astype(F32)
    cw = gdn_conv_w.astype(F32)

    hg_args = (nm0, hg_win, lbnd, hg_gn, hg_wo)
    gdn_args = (nm1, gdn_win, cw, par, gdn_gn, gdn_wo)
    xm = meta_tokens.astype(F32)
    xp2 = x_prompt.reshape(np_, D)
    xs2 = x_sample.reshape(ns_, D)

    zero_state = jnp.zeros((1, H, HD, HD), F32)
    h1m, hg_m = _hgrn_call(xm, zero_state, *hg_args, nseq=1, seqlen=N_META, bb=1, tt=N_META, c=N_META)
    h1p, hg_p = _hgrn_call(xp2, hg_m, *hg_args, nseq=bp, seqlen=tp, bb=1, tt=256, c=64, nu=4, bcast_state=True)
    h1s, hg_s = _hgrn_call(xs2, state_hgrn.astype(F32), *hg_args, nseq=bs, seqlen=ts, bb=8, tt=ts, c=64, nu=1)
    h2m = _ffn_call([h1m], nf0, w1d, w3d, w2d, tm=N_META)
    h2 = _ffn_call([h1p, h1s], nf0, w1d, w3d, w2d, tm=512)

    zero_conv = jnp.zeros((1, CONV - 1, QKV), F32)
    _, gdn_m, conv_m = _gdn_call(h2m, zero_state, zero_conv, *gdn_args, nseq=1, seqlen=N_META, bb=1, tt=N_META, c=16)
    h3p, gdn_p, conv_p = _gdn_call(h2, gdn_m, conv_m, *gdn_args, nseq=bp, seqlen=tp, bb=1, tt=256, c=128, nu=2,
                                   bcast_state=True)
    h3s, gdn_s, conv_s = _gdn_call(h2, state_gdn.astype(F32), state_gdn_conv.astype(F32), *gdn_args,
                                   nseq=bs, seqlen=ts, bb=8, tt=ts, c=64, nu=1, row0=np_)

    yp, ysm = _moe_final(h3p, h3s, nf1, wr, w1e, w3e, w2e, nfin, tm=256)

    sd, gd, cd = state_hgrn.dtype, state_gdn.dtype, state_gdn_conv.dtype
    return (yp.reshape(bp, tp, D), ysm.reshape(bs, ts, D),
            hg_p.astype(sd), hg_s.astype(sd),
            gdn_p.astype(gd), gdn_s.astype(gd), conv_p.astype(cd), conv_s.astype(cd))


def _moe_final(h3p, h3s, nf1, wr, w1e, w3e, w2e, nfin, *, tm):
    np_ = h3p.shape[0]
    n_tok = np_ + h3s.shape[0]
    nblk = n_tok // tm
    gate, lpos, lpost, tab, tot = _router_call(h3p, h3s, nf1, wr, tm=tm)
    totals = tot[0, :NE].astype(jnp.int32)
    padded = ((totals + TM - 1) // TM) * TM
    ends = jnp.cumsum(padded)
    offs = ends - padded
    tab3 = tab.reshape(nblk, 8, LANES)[:, :3, :NE].astype(jnp.int32)
    seg_tab = jnp.stack([tab3[:, 0] // SEG, tab3[:, 1], tab3[:, 2] + offs[None, :]], axis=-1).reshape(-1)
    ntile = -(-(2 * n_tok + nblk * NE * (SEG - 1) + NE * (TM - 1)) // TM)
    nact = ends[-1] // TM
    tiles = jnp.arange(ntile, dtype=jnp.int32)
    texp_all = jnp.minimum(jnp.sum(tiles[:, None] * TM >= ends[None, :], axis=1), NE - 1).astype(jnp.int32)
    last = jnp.maximum(nact - 1, 0)
    active = tiles < nact
    texp = jnp.where(active, texp_all, texp_all[last]).astype(jnp.int32)
    tsrc = jnp.where(active, tiles, last).astype(jnp.int32)
    tvalid = jnp.where(active, jnp.clip(totals[texp_all] - (tiles * TM - offs[texp_all]), 0, TM), 0).astype(jnp.int32)
    tail = nact + jnp.arange(ntile - 2 * n_tok // TM)
    zst = jnp.concatenate([jnp.where(padded > 0, ends - TM, -1),
                           jnp.where(tail < ntile, tail * TM, -1)]).astype(jnp.int32)
    xs_sorted = _dispatch_call(seg_tab, zst, h3p, h3s, lpost, nf1, tm=tm, rmax=ntile * TM)
    ys = _expert_call(texp, tsrc, tvalid, xs_sorted, w1e, w3e, w2e)
    yp = _combine_call(seg_tab, h3p, gate, lpos, nfin, ys, tm=tm, tok0=0)
    ysm = _combine_call(seg_tab, h3s, gate, lpos, nfin, ys, tm=tm, tok0=np_)
    return yp, ysm
```

```python
import functools

import jax
import jax.numpy as jnp
from jax import lax
from jax.experimental import pallas as pl
from jax.experimental.pallas import tpu as pltpu

D = 1024
H = 8
HD = 128
QKV = 3 * H * HD
CONV = 4
DFF = 3584
NE = 8
N_META = 16
EPS = 1e-6
LANES = 128
FB = 512
TM = 512
HG_SUB = 16
SEG = 8
V7X_VMEM_LIMIT = 56 * 1024 * 1024

F32 = jnp.float32
BF16 = jnp.bfloat16


def _dot(a, b):
    return jnp.dot(a, b, preferred_element_type=F32)


def _dot_nt(a, b):
    return lax.dot_general(a, b, (((1,), (1,)), ((), ())), preferred_element_type=F32)


def _dot_tn(a, b):
    return lax.dot_general(a, b, (((0,), (0,)), ((), ())), preferred_element_type=F32)


def _split3(x):
    x1 = x.astype(BF16)
    r1 = x - x1.astype(F32)
    x2 = r1.astype(BF16)
    x3 = (r1 - x2.astype(F32)).astype(BF16)
    return x1, x2, x3


def _sel_dot(sel, x):
    x1, x2, x3 = _split3(x)
    return _dot(sel, x1) + _dot(sel, x2) + _dot(sel, x3)


def _dot_3x(a, b):
    a_hi = a.astype(BF16)
    a_lo = (a - a_hi.astype(F32)).astype(BF16)
    b_hi = b.astype(BF16)
    b_lo = (b - b_hi.astype(F32)).astype(BF16)
    return _dot(a_hi, b_hi) + (_dot(a_hi, b_lo) + _dot(a_lo, b_hi))


def _sel_dot_nt(sel, x):
    x1, x2, x3 = _split3(x)
    return _dot_nt(sel, x1) + _dot_nt(sel, x2) + _dot_nt(sel, x3)


def _rms(x, g):
    return x * lax.rsqrt(jnp.mean(x * x, axis=-1, keepdims=True) + EPS) * g


def _silu(x):
    return x * jax.nn.sigmoid(x)


def _iota2(shape, dim):
    return lax.broadcasted_iota(jnp.int32, shape, dim)


def _head_rms_gate(o, g, gate):
    parts = []
    for h in range(H):
        oh = o[:, h * HD:(h + 1) * HD]
        parts.append(_rms(oh, g))
    return jnp.concatenate(parts, axis=1) * gate


def _chunk_row(idx, j, nu, c):
    if isinstance(idx, int):
        return (idx * nu + j) * c
    return pl.multiple_of((idx * nu + j) * c, c)


def _chunk_seq(idx, j, nu, ncs):
    return (idx * nu + j) // ncs


def _chunk_ops(tri, c):
    assert c % 16 == 0
    tri_b = tri.astype(BF16)
    return (lambda x: _sel_dot(tri_b, x)), (lambda x: x.astype(BF16))


def _run_groups(group, n):
    if n == 1:
        group(0, 0)
    else:
        lax.fori_loop(0, n, group, 0)


def _const_spec(shape):
    nd = len(shape)
    return pl.BlockSpec(shape, lambda *_: (0,) * nd, pipeline_mode=pl.Buffered(1))


def _hgrn_kernel(x_ref, s0_ref, nw_ref, win_ref, lb_ref, gn_ref, wo_ref,
                 o_ref, sout_ref,
                 st_scr, q_scr, k_scr, lf_scr, v_scr, g_scr, *, bb, tt, c, nu):
    t = pl.program_id(1)
    rows = bb * tt
    sl = min(c, tt)
    spc = c // sl
    ncs = max(tt // c, 1)
    assert c % sl == 0 and (ncs == 1 or ncs % nu == 0)

    @pl.when(t == 0)
    def _():
        for b in range(bb):
            for h in range(H):
                st_scr[b, h] = s0_ref[b, h].T

    x = x_ref[...]
    xn = _rms(x, nw_ref[...]).astype(BF16)
    lbn = lb_ref[...]
    e = jnp.exp(lbn - jnp.max(lbn, axis=0, keepdims=True))
    lb = e[0:1] / jnp.sum(e, axis=0, keepdims=True)
    q_scr[...] = _silu(_dot(xn, win_ref[:, 0:D])) * HD ** -0.5
    f = lb + (1.0 - lb) * jax.nn.sigmoid(_dot(xn, win_ref[:, D:2 * D]))
    k_scr[...] = 1.0 - f
    lf_scr[...] = jnp.log(f)
    v_scr[...] = _dot(xn, win_ref[:, 2 * D:3 * D])
    g_scr[...] = _silu(_dot(xn, win_ref[:, 3 * D:4 * D]))

    row = _iota2((c, c), 0)
    col = _iota2((c, c), 1)
    tri = (row >= col) & ((row // sl) == (col // sl))
    cumsum, mxu = _chunk_ops(tri, c)
    heads = [slice(h * HD, (h + 1) * HD) for h in range(H)]
    nsub = c // HG_SUB if sl > HG_SUB else 1
    if nsub > 1:
        local_tri = (tri & ((row // HG_SUB) == (col // HG_SUB))).astype(BF16)
        slab_row = _iota2((c, D), 0)
    if spc > 1:
        seq_last = (col == (row // sl) * sl + (sl - 1)).astype(BF16)

    def chunk_terms(r0):
        lf = lf_scr[pl.ds(r0, c), :]
        bc = cumsum(lf)
        tot = bc[c - 1:c, :] if spc == 1 else _sel_dot(seq_last, bc)
        qq = q_scr[pl.ds(r0, c), :]
        kk = k_scr[pl.ds(r0, c), :]
        v32 = v_scr[pl.ds(r0, c), :]
        vv = mxu(v32)
        qd32 = qq * jnp.exp(bc)
        kt32 = kk * jnp.exp(tot - bc)
        qdj = mxu(qd32)
        ktail = mxu(kt32)
        if nsub == 1:
            kinv = mxu(kk * jnp.exp(-bc))
            scj = [_dot_nt(qdj[:, s], kinv[:, s]) for s in heads]
        else:
            bl = _sel_dot(local_tri, lf)
            qloc = mxu(qq * jnp.exp(bl))
            parts = [[] for _ in heads]
            for i in range(nsub):
                lo, hi = i * HG_SUB, (i + 1) * HG_SUB
                ref_pt = bc[lo - 1:lo, :] if i else 0.0
                kci = mxu(jnp.where(slab_row < hi, kk * jnp.exp(ref_pt - bc), 0.0))
                for h, s in enumerate(heads):
                    parts[h].append(_dot_nt(qloc[lo:hi, s], kci[:, s]))
            scj = [jnp.concatenate(p, axis=0) for p in parts]
        scj = [mxu(jnp.where(tri, sc, 0.0)) for sc in scj]
        oij = [_dot(scj[h], vv[:, heads[h]]) for h in range(H)]
        if spc > 1:
            return qd32, jnp.exp(tot), (v32, kt32), oij
        kvj = [_dot_tn(vv[:, s], ktail[:, s]) for s in heads]
        return qdj, jnp.exp(tot), kvj, oij

    def group(idx, carry):
        qd, dec, kv, oi, r0s, sqs = [], [], [], [], [], []
        for j in range(nu):
            r0 = _chunk_row(idx, j, nu, c)
            qdj, decj, kvj, oij = chunk_terms(r0)
            kv.append(kvj)
            oi.append(oij)
            qd.append(qdj)
            dec.append(decj)
            r0s.append(r0)
            sqs.append(_chunk_seq(idx, j, nu, ncs) if spc == 1 else (idx * nu + j) * spc)

        if spc > 1:
            own_block = (_iota2((c, spc * HD), 1) // HD) == (_iota2((c, spc * HD), 0) // sl)
            outs, finals = [], []
            for j in range(nu):
                v32, kt32 = kv[j]
                o_j = []
                for h, s in enumerate(heads):
                    st_all = jnp.concatenate([st_scr[sqs[j] + q, h] for q in range(spc)], axis=0)
                    own = jnp.where(own_block, _dot_nt(mxu(qd[j][:, s]), mxu(st_all)), 0.0)
                    inter = own[:, 0:HD]
                    for q in range(1, spc):
                        inter = inter + own[:, q * HD:(q + 1) * HD]
                    o_j.append(oi[j][h] + inter)
                    decay = jnp.concatenate(
                        [jnp.broadcast_to(dec[j][q * sl:q * sl + 1, s], (HD, HD)) for q in range(spc)], axis=0)
                    v_spread = mxu(jnp.where(own_block, jnp.concatenate([v32[:, s]] * spc, axis=1), 0.0))
                    st_new = st_all * decay + _dot_tn(v_spread, mxu(kt32[:, s]))
                    finals.extend((sqs[j] + q, h, st_new[q * HD:(q + 1) * HD]) for q in range(spc))
                outs.append(o_j)
            for j in range(nu):
                q_scr[pl.ds(r0s[j], c), :] = jnp.concatenate(outs[j], axis=1)
            for sq, h, st_new in finals:
                st_scr[sq, h] = st_new
            return carry

        own_state = ncs == 1
        loaded = [[st_scr[sqs[j], h] for h in range(H)] for j in range(nu if own_state else 1)]
        outs, finals = [], []
        st = loaded[0]
        for j in range(nu):
            if own_state:
                st = loaded[j]
            outs.append([oi[j][h] + _dot_nt(qd[j][:, heads[h]], mxu(st[h])) for h in range(H)])
            st = [st[h] * dec[j][:, heads[h]] + kv[j][h] for h in range(H)]
            if own_state or j == nu - 1:
                finals.append((sqs[j], st))
        for j in range(nu):
            q_scr[pl.ds(r0s[j], c), :] = jnp.concatenate(outs[j], axis=1)
        for sq, st in finals:
            for h in range(H):
                st_scr[sq, h] = st[h]
        return carry

    _run_groups(group, rows // (nu * c))

    on = _head_rms_gate(q_scr[...], gn_ref[...], g_scr[...])
    o_ref[...] = x + _dot(on.astype(BF16), wo_ref[...])

    @pl.when(t == pl.num_programs(1) - 1)
    def _():
        for b in range(bb):
            for h in range(H):
                sout_ref[b, h] = st_scr[b, h].T


def _hgrn_call(x2d, s0, nw, win, lbnd, gn, wo, *, nseq, seqlen, bb, tt, c, nu=4, row0=0, bcast_state=False):
    assert bb == 1 or tt == seqlen
    assert nseq % bb == 0 and seqlen % tt == 0 and row0 % (bb * tt) == 0
    assert tt % c == 0 or (c % tt == 0 and (bb * tt) % c == 0)
    rows = bb * tt
    nu = min(nu, rows // c)
    nt = seqlen // tt
    blk0 = row0 // rows
    s_idx = (lambda b, t: (0, 0, 0, 0)) if bcast_state else (lambda b, t: (b, 0, 0, 0))
    slab = pltpu.VMEM((rows, D), F32)
    return pl.pallas_call(
        functools.partial(_hgrn_kernel, bb=bb, tt=tt, c=c, nu=nu),
        grid=(nseq // bb, nt),
        in_specs=[
            pl.BlockSpec((rows, D), lambda b, t: (blk0 + b * nt + t, 0)),
            pl.BlockSpec((bb, H, HD, HD), s_idx),
            _const_spec((1, D)),
            _const_spec((D, 4 * D)),
            _const_spec(lbnd.shape),
            _const_spec((1, HD)),
            _const_spec((D, D)),
        ],
        out_specs=[
            pl.BlockSpec((rows, D), lambda b, t: (b * nt + t, 0)),
            pl.BlockSpec((bb, H, HD, HD), lambda b, t: (b, 0, 0, 0)),
        ],
        out_shape=[
            jax.ShapeDtypeStruct((nseq * seqlen, D), F32),
            jax.ShapeDtypeStruct((nseq, H, HD, HD), F32),
        ],
        scratch_shapes=[pltpu.VMEM((bb, H, HD, HD), F32), slab, slab, slab, slab, slab],
        compiler_params=pltpu.CompilerParams(
            dimension_semantics=("arbitrary", "arbitrary"), vmem_limit_bytes=V7X_VMEM_LIMIT),
        name=f"hgrn_mixer_{nseq}x{seqlen}",
    )(x2d, s0, nw, win, lbnd, gn, wo)


def _neumann_inv(ls, eye, levels, mxu):
    xs = [eye - l for l in ls]
    ps = [mxu(l) for l in ls]
    for _ in range(levels - 1):
        ps = [mxu(_dot(p, p)) for p in ps]
        xs = [x + _dot(mxu(x), p) for x, p in zip(xs, ps)]
    return xs


def _unit_lower_inv(ls, c, mxu, span=None):
    span = c if span is None else span
    row = _iota2((c, c), 0)
    col = _iota2((c, c), 1)
    eye = (row == col).astype(F32)
    if span <= 16:
        return _neumann_inv(ls, eye, span.bit_length() - 1, mxu)
    sub = 16
    nblk = c // sub
    assert c == nblk * sub and nblk & (nblk - 1) == 0
    diag = (row // sub) == (col // sub)
    lds = [jnp.where(diag, l, 0.0) for l in ls]
    dinvs = [mxu(d) for d in _neumann_inv(lds, eye, 4, mxu)]
    ms = [_dot(d, mxu(l - ld)) for d, l, ld in zip(dinvs, ls, lds)]
    ys = _neumann_inv(ms, eye, nblk.bit_length() - 1, mxu)
    return [_dot(mxu(y), d) for y, d in zip(ys, dinvs)]


def _gdn_kernel(x_ref, s0_ref, c0_ref, nw_ref, win_ref, cw_ref, par_ref, gn_ref, wo_ref,
                o_ref, sout_ref, cout_ref,
                s_scr, xp_scr, act_scr, z_scr, la_scr, be_scr, o_scr, *, bb, tt, c, nu):
    t = pl.program_id(1)
    rows = bb * tt
    sl = min(c, tt)
    spc = c // sl
    ncs = max(tt // c, 1)
    assert c % sl == 0 and (ncs == 1 or ncs % nu == 0)

    @pl.when(t == 0)
    def _():
        s_scr[...] = s0_ref[...]
        for b in range(bb):
            xp_scr[b, 8 - (CONV - 1):8, :] = c0_ref[b]

    x = x_ref[...]
    xn = _rms(x, nw_ref[...]).astype(BF16)
    pre = _dot(xn, win_ref[:, 0:QKV])
    for b in range(bb):
        xp_scr[b, 8:8 + tt, :] = pre[b * tt:(b + 1) * tt]
    z_scr[...] = _silu(_dot(xn, win_ref[:, QKV:QKV + D]))
    ba = _dot(xn, win_ref[:, QKV + D:QKV + D + LANES])
    par = par_ref[...]
    sp = ba + par[0:1]
    softplus = jnp.maximum(sp, 0.0) + jnp.log(1.0 + jnp.exp(-jnp.abs(sp)))
    la_scr[...] = -jnp.exp(par[1:2]) * softplus
    be_scr[...] = jax.nn.sigmoid(ba)

    cw = cw_ref[...]
    for b in range(bb):
        conv = xp_scr[b, 8 - (CONV - 1):8 - (CONV - 1) + tt, :] * cw[0:1]
        for w in range(1, CONV):
            conv = conv + xp_scr[b, 8 - (CONV - 1) + w:8 - (CONV - 1) + w + tt, :] * cw[w:w + 1]
        act = _silu(conv)
        for j in range(2 * H):
            a = act[:, j * HD:(j + 1) * HD]
            scale = HD ** -0.5 if j < H else 1.0
            act_scr[b * tt:(b + 1) * tt, j * HD:(j + 1) * HD] = (
                a * lax.rsqrt(jnp.sum(a * a, axis=-1, keepdims=True) + EPS) * scale)
        act_scr[b * tt:(b + 1) * tt, 2 * H * HD:] = act[:, 2 * H * HD:]
        tail = xp_scr[b, tt:tt + 8, :]
        xp_scr[b, 0:8, :] = tail

    row = _iota2((c, c), 0)
    col = _iota2((c, c), 1)
    same_seq = (row // sl) == (col // sl)
    incl = (row >= col) & same_seq
    strict = (row > col) & same_seq
    cumsum, mxu = _chunk_ops(incl, c)
    eye_l = (_iota2((LANES, LANES), 0) == _iota2((LANES, LANES), 1)).astype(BF16)
    transpose = lambda m: _sel_dot_nt(eye_l, m)
    if spc > 1:
        seq_last = (col == (row // sl) * sl + (sl - 1)).astype(BF16)
    narrow = mxu if spc == 1 else (lambda a: a)

    heads = [slice(h * HD, (h + 1) * HD) for h in range(H)]

    def group(idx, carry):
        qd, ktail, dtot, lmat, amat, rwu, r0s, sqs = [], [], [], [], [], [], [], []
        for j in range(nu):
            r0 = _chunk_row(idx, j, nu, c)
            gc = cumsum(la_scr[pl.ds(r0, c), :])
            gt = transpose(gc)
            be = be_scr[pl.ds(r0, c), :]
            eg = jnp.exp(gc)
            gl = gc[c - 1:c, :] if spc == 1 else _sel_dot(seq_last, gc)
            et = jnp.exp(gl - gc)
            dt = jnp.exp(gl)
            qq = act_scr[pl.ds(r0, c), 0:H * HD]
            kk = act_scr[pl.ds(r0, c), H * HD:2 * H * HD]
            vv = act_scr[pl.ds(r0, c), 2 * H * HD:]
            qd_j, ktail_j, dtot_j = [], [], []
            for h, s in enumerate(heads):
                g_col = gc[:, 8 + h:9 + h]
                eg_col = eg[:, 8 + h:9 + h]
                be_col = be[:, h:h + 1]
                kh = mxu(kk[:, s])
                qk = _dot_nt(jnp.concatenate([mxu(qq[:, s]), kh], axis=0), kh)
                ed = jnp.exp(g_col - gt[8 + h:9 + h, :])
                lmat.append(be_col * qk[c:] * jnp.where(strict, ed, 0.0))
                amat.append(mxu(qk[:c] * jnp.where(incl, ed, 0.0)))
                rwu.append(mxu(jnp.concatenate([kk[:, s] * (be_col * eg_col), vv[:, s] * be_col], axis=1)))
                qd_j.append(narrow(qq[:, s] * eg_col))
                ktail_j.append(narrow(kk[:, s] * et[:, 8 + h:9 + h]))
                dtot_j.append(dt[:, 8 + h:9 + h])
            qd.append(qd_j)
            ktail.append(ktail_j)
            dtot.append(dtot_j)
            r0s.append(r0)
            sqs.append(_chunk_seq(idx, j, nu, ncs) if spc == 1 else (idx * nu + j) * spc)
        tinv = _unit_lower_inv(lmat, c, mxu, span=sl)
        wu = [_dot(mxu(t_), r_) for t_, r_ in zip(tinv, rwu)]

        if spc > 1:
            own_block = (_iota2((c, spc * HD), 1) // HD) == (_iota2((c, spc * HD), 0) // sl)
            spread = lambda a: mxu(jnp.where(own_block, jnp.concatenate([a] * spc, axis=1), 0.0))
            outs, finals = [], []
            for j in range(nu):
                o_j = []
                for h in range(H):
                    wuh = wu[j * H + h]
                    s_all = jnp.concatenate([s_scr[sqs[j] + q, h] for q in range(spc)], axis=0)
                    qw = _dot(jnp.concatenate([spread(qd[j][h]), spread(wuh[:, :HD])], axis=0), mxu(s_all))
                    u = wuh[:, HD:] - qw[c:]
                    ub = mxu(u)
                    o_j.append(qw[:c] + _dot(amat[j * H + h], ub))
                    decay = jnp.concatenate(
                        [jnp.broadcast_to(dtot[j][h][q * sl:q * sl + 1], (HD, HD)) for q in range(spc)], axis=0)
                    s_new = s_all * decay + _dot_tn(spread(ktail[j][h]), ub)
                    finals.extend((sqs[j] + q, h, s_new[q * HD:(q + 1) * HD]) for q in range(spc))
                outs.append(o_j)
            for j in range(nu):
                o_scr[pl.ds(r0s[j], c), :] = jnp.concatenate(outs[j], axis=1)
            for sq, h, s_new in finals:
                s_scr[sq, h] = s_new
            return carry

        own_state = ncs == 1
        loaded = [[s_scr[sqs[j], h] for h in range(H)] for j in range(nu if own_state else 1)]
        outs, finals = [], []
        st = loaded[0]
        for j in range(nu):
            if own_state:
                st = loaded[j]
            o_j, st_next = [], []
            for h in range(H):
                wuh = wu[j * H + h]
                qw = _dot(jnp.concatenate([qd[j][h], mxu(wuh[:, :HD])], axis=0), mxu(st[h]))
                u = wuh[:, HD:] - qw[c:]
                ub = mxu(u)
                o_j.append(qw[:c] + _dot(amat[j * H + h], ub))
                st_next.append(st[h] * dtot[j][h] + _dot_tn(ktail[j][h], ub))
            st = st_next
            outs.append(o_j)
            if own_state or j == nu - 1:
                finals.append((sqs[j], st))
        for j in range(nu):
            o_scr[pl.ds(r0s[j], c), :] = jnp.concatenate(outs[j], axis=1)
        for sq, st in finals:
            for h in range(H):
                s_scr[sq, h] = st[h]
        return carry

    _run_groups(group, rows // (nu * c))

    on = _head_rms_gate(o_scr[...], gn_ref[...], z_scr[...])
    o_ref[...] = x + _dot(on.astype(BF16), wo_ref[...])

    @pl.when(t == pl.num_programs(1) - 1)
    def _():
        sout_ref[...] = s_scr[...]
        for b in range(bb):
            cout_ref[b] = xp_scr[b, 8 - (CONV - 1):8, :]


def _gdn_call(x2d, s0, c0, nw, win, cw, par, gn, wo, *, nseq, seqlen, bb, tt, c, nu=2, row0=0,
              bcast_state=False):
    assert bb == 1 or tt == seqlen
    assert nseq % bb == 0 and seqlen % tt == 0 and tt % 8 == 0 and row0 % (bb * tt) == 0
    assert tt % c == 0 or (c % tt == 0 and (bb * tt) % c == 0)
    rows = bb * tt
    nt = seqlen // tt
    blk0 = row0 // rows
    s_idx = (lambda b, t: (0, 0, 0, 0)) if bcast_state else (lambda b, t: (b, 0, 0, 0))
    c_idx = (lambda b, t: (0, 0, 0)) if bcast_state else (lambda b, t: (b, 0, 0))
    slab = pltpu.VMEM((rows, D), F32)
    return pl.pallas_call(
        functools.partial(_gdn_kernel, bb=bb, tt=tt, c=c, nu=min(nu, bb * tt // c)),
        grid=(nseq // bb, nt),
        in_specs=[
            pl.BlockSpec((rows, D), lambda b, t: (blk0 + b * nt + t, 0)),
            pl.BlockSpec((bb, H, HD, HD), s_idx),
            pl.BlockSpec((bb, CONV - 1, QKV), c_idx),
            _const_spec((1, D)),
            _const_spec(win.shape),
            _const_spec((CONV, QKV)),
            _const_spec((2, LANES)),
            _const_spec((1, HD)),
            _const_spec((D, D)),
        ],
        out_specs=[
            pl.BlockSpec((rows, D), lambda b, t: (b * nt + t, 0)),
            pl.BlockSpec((bb, H, HD, HD), lambda b, t: (b, 0, 0, 0)),
            pl.BlockSpec((bb, CONV - 1, QKV), lambda b, t: (b, 0, 0)),
        ],
        out_shape=[
            jax.ShapeDtypeStruct((nseq * seqlen, D), F32),
            jax.ShapeDtypeStruct((nseq, H, HD, HD), F32),
            jax.ShapeDtypeStruct((nseq, CONV - 1, QKV), F32),
        ],
        scratch_shapes=[
            pltpu.VMEM((bb, H, HD, HD), F32),
            pltpu.VMEM((bb, tt + 8, QKV), F32),
            pltpu.VMEM((rows, QKV), F32),
            slab,
            pltpu.VMEM((rows, LANES), F32),
            pltpu.VMEM((rows, LANES), F32),
            slab,
        ],
        compiler_params=pltpu.CompilerParams(
            dimension_semantics=("arbitrary", "arbitrary"), vmem_limit_bytes=V7X_VMEM_LIMIT),
        name=f"gdn_mixer_{nseq}x{seqlen}",
    )(x2d, s0, c0, nw, win, cw, par, gn, wo)


def _swiglu(xb, w1_ref, w3_ref, w2_ref):
    acc = None
    for j in range(DFF // FB):
        fs = slice(j * FB, (j + 1) * FB)
        hh = (_silu(_dot(xb, w1_ref[:, fs])) * _dot(xb, w3_ref[:, fs])).astype(BF16)
        part = _dot(hh, w2_ref[fs, :])
        acc = part if acc is None else acc + part
    return acc


def _ffn_kernel(*refs, n_in, nblk_first):
    x_refs = refs[:n_in]
    nw_ref, w1_ref, w3_ref, w2_ref, o_ref = refs[n_in:]
    x = x_refs[0][...]
    if n_in == 2:
        x = jnp.where(pl.program_id(0) < nblk_first, x, x_refs[1][...])
    xb = _rms(x, nw_ref[...]).astype(BF16)
    o_ref[...] = x + _swiglu(xb, w1_ref, w3_ref, w2_ref)


def _two_group_specs(tm, nblk_first, nblk_total):
    last_first = nblk_first - 1
    return [
        pl.BlockSpec((tm, D), lambda i, *_: (jnp.minimum(i, last_first), 0)),
        pl.BlockSpec((tm, D), lambda i, *_: (jnp.maximum(i - nblk_first, 0), 0)),
    ]


def _ffn_call(xs, nw, w1, w3, w2, *, tm):
    nblks = [x.shape[0] // tm for x in xs]
    assert all(x.shape[0] % tm == 0 for x in xs)
    total = sum(nblks)
    if len(xs) == 2:
        x_specs = _two_group_specs(tm, nblks[0], total)
    else:
        x_specs = [pl.BlockSpec((tm, D), lambda i: (i, 0))]
    return pl.pallas_call(
        functools.partial(_ffn_kernel, n_in=len(xs), nblk_first=nblks[0]),
        grid=(total,),
        in_specs=x_specs + [_const_spec((1, D)), _const_spec((D, DFF)), _const_spec((D, DFF)), _const_spec((DFF, D))],
        out_specs=pl.BlockSpec((tm, D), lambda i: (i, 0)),
        out_shape=jax.ShapeDtypeStruct((total * tm, D), F32),
        compiler_params=pltpu.CompilerParams(dimension_semantics=("arbitrary",), vmem_limit_bytes=V7X_VMEM_LIMIT),
        name=f"ffn_dense_{total * tm}",
    )(*xs, nw, w1, w3, w2)


def _router_kernel(xa_ref, xb_ref, nw_ref, wr_ref, gate_ref, lpos_ref, lpost_ref, tab_ref, tot_ref, seg_scr, *,
                   tm, nblk_first):
    i = pl.program_id(0)

    @pl.when(i == 0)
    def _():
        seg_scr[...] = jnp.zeros_like(seg_scr)

    x = jnp.where(i < nblk_first, xa_ref[...], xb_ref[...])
    xn = _rms(x, nw_ref[...])
    logits = _dot_3x(xn, wr_ref[...])
    lane = _iota2((tm, LANES), 1).astype(F32)
    neg = jnp.float32(-jnp.inf)
    l1 = jnp.where(lane < NE, logits, neg)
    m1 = jnp.max(l1, axis=-1, keepdims=True)
    i1 = jnp.min(jnp.where(l1 == m1, lane, float(LANES)), axis=-1, keepdims=True)
    l2 = jnp.where(lane == i1, neg, l1)
    m2 = jnp.max(l2, axis=-1, keepdims=True)
    i2 = jnp.min(jnp.where(l2 == m2, lane, float(LANES)), axis=-1, keepdims=True)
    e2 = jnp.exp(m2 - m1)
    g1 = 1.0 / (1.0 + e2)
    g2 = e2 / (1.0 + e2)

    oh1 = (lane == i1).astype(F32)
    oh2 = (lane == i2).astype(F32)
    below = (_iota2((tm, tm), 0) > _iota2((tm, tm), 1)).astype(BF16)
    c1 = _dot(below, oh1.astype(BF16))
    c2 = _dot(below, oh2.astype(BF16))
    n1 = jnp.sum(oh1, axis=0, keepdims=True)
    n2 = jnp.sum(oh2, axis=0, keepdims=True)
    cnt = jnp.floor((n1 + n2 + (SEG - 1)) * (1.0 / SEG)) * SEG
    before = (_iota2((LANES, LANES), 0) < _iota2((LANES, LANES), 1)).astype(BF16)
    start = _dot(jnp.broadcast_to(cnt, (16, LANES)).astype(BF16), before)[0:1]
    p1 = jnp.sum(oh1 * (start + c1), axis=-1, keepdims=True)
    p2 = jnp.sum(oh2 * (start + n1 + c2), axis=-1, keepdims=True)
    used = seg_scr[...]
    seg_scr[...] = used + cnt
    tot_ref[...] = used + cnt

    gate_ref[:, 0:1] = g1
    gate_ref[:, 1:2] = g2
    lpos_ref[:, 0:1] = p1
    lpos_ref[:, 1:2] = p2
    sub = _iota2((8, LANES), 0)
    tab_ref[...] = jnp.where(sub == 0, cnt, jnp.where(sub == 1, start, jnp.where(sub == 2, used, 0.0)))
    slab = jnp.where(lane == 0.0, p1, jnp.where(lane == 1.0, p2, 0.0))
    eye16 = (_iota2((16, LANES), 0) == _iota2((16, LANES), 1)).astype(BF16)
    lpost_ref[...] = _sel_dot_nt(eye16, slab)[0:8]


def _router_call(xa, xb, nw, wr, *, tm):
    na, nb = xa.shape[0] // tm, xb.shape[0] // tm
    nblk = na + nb
    n = nblk * tm
    return pl.pallas_call(
        functools.partial(_router_kernel, tm=tm, nblk_first=na),
        grid=(nblk,),
        in_specs=_two_group_specs(tm, na, nblk) + [_const_spec((1, D)), _const_spec((D, LANES))],
        out_specs=[
            pl.BlockSpec((tm, 2), lambda i: (i, 0)),
            pl.BlockSpec((tm, 2), lambda i: (i, 0)),
            pl.BlockSpec((8, tm), lambda i: (i, 0)),
            pl.BlockSpec((8, LANES), lambda i: (i, 0)),
            pl.BlockSpec((1, LANES), lambda i: (0, 0)),
        ],
        out_shape=[
            jax.ShapeDtypeStruct((n, 2), F32),
            jax.ShapeDtypeStruct((n, 2), F32),
            jax.ShapeDtypeStruct((nblk * 8, tm), F32),
            jax.ShapeDtypeStruct((nblk * 8, LANES), F32),
            jax.ShapeDtypeStruct((1, LANES), F32),
        ],
        scratch_shapes=[pltpu.VMEM((1, LANES), F32)],
        compiler_params=pltpu.CompilerParams(dimension_semantics=("arbitrary",), vmem_limit_bytes=V7X_VMEM_LIMIT),
        name="moe_router",
    )(xa, xb, nw, wr)


def _local_rows(tm):
    return -(-(2 * tm + NE * (SEG - 1)) // LANES) * LANES


def _segment_copies(tab_ref, tile, local_ref, global_ref, sem, *, to_global, wait):
    for e in range(NE):
        base = (tile * NE + e) * 3
        npiece, lstart, gstart = tab_ref[base], tab_ref[base + 1], tab_ref[base + 2]

        def piece(k, carry, lstart=lstart, gstart=gstart):
            loc = local_ref.at[pl.ds(pl.multiple_of(lstart + k * SEG, SEG), SEG)]
            glo = global_ref.at[pl.ds(pl.multiple_of(gstart + k * SEG, SEG), SEG)]
            cp = pltpu.make_async_copy(loc, glo, sem) if to_global else pltpu.make_async_copy(glo, loc, sem)
            if wait:
                cp.wait()
            else:
                cp.start()
            return carry

        lax.fori_loop(0, npiece, piece, 0)


def _dispatch_kernel(tab_ref, zst_ref, xa_ref, xb_ref, lpost_ref, nw_ref, xs_ref, buf, zero_scr, sem, zsem, *,
                     tm, nblk_first, lr):
    i = pl.program_id(0)
    slot = lax.rem(i, 2)

    @pl.when(i == 0)
    def _():
        zero_scr[...] = jnp.zeros_like(zero_scr)

        def zero_copy(e):
            start = pl.multiple_of(zst_ref[e], TM)
            return pltpu.make_async_copy(zero_scr, xs_ref.at[pl.ds(start, TM)], zsem)

        for e in range(zst_ref.shape[0]):
            pl.when(zst_ref[e] >= 0)(lambda e=e: zero_copy(e).start())
        for e in range(zst_ref.shape[0]):
            pl.when(zst_ref[e] >= 0)(lambda e=e: zero_copy(e).wait())

    x = jnp.where(i < nblk_first, xa_ref[...], xb_ref[...])
    xn = _rms(x, nw_ref[...]).astype(BF16)
    lpt = lpost_ref[...]
    rio = _iota2((lr, tm), 0).astype(F32)
    sel = jnp.where(rio == lpt[0:1, :], 1.0, jnp.where(rio == lpt[1:2, :], 1.0, 0.0)).astype(BF16)
    buf[slot] = _dot(sel, xn)

    copies = functools.partial(_segment_copies, tab_ref, global_ref=xs_ref, to_global=True)
    copies(i, local_ref=buf.at[slot], sem=sem.at[slot], wait=False)
    pl.when(i > 0)(lambda: copies(i - 1, local_ref=buf.at[1 - slot], sem=sem.at[1 - slot], wait=True))
    pl.when(i == pl.num_programs(0) - 1)(
        lambda: copies(i, local_ref=buf.at[slot], sem=sem.at[slot], wait=True))


def _dispatch_call(tab, zst, xa, xb, lpost, nw, *, tm, rmax):
    na, nb = xa.shape[0] // tm, xb.shape[0] // tm
    lr = _local_rows(tm)
    return pl.pallas_call(
        functools.partial(_dispatch_kernel, tm=tm, nblk_first=na, lr=lr),
        grid_spec=pltpu.PrefetchScalarGridSpec(
            num_scalar_prefetch=2,
            grid=(na + nb,),
            in_specs=_two_group_specs(tm, na, na + nb) + [pl.BlockSpec((8, tm), lambda i, *_: (i, 0)),
                                                          pl.BlockSpec((1, D), lambda i, *_: (0, 0))],
            out_specs=pl.BlockSpec(memory_space=pl.ANY),
            scratch_shapes=[pltpu.VMEM((2, lr, D), F32), pltpu.VMEM((TM, D), F32),
                            pltpu.SemaphoreType.DMA((2,)), pltpu.SemaphoreType.DMA],
        ),
        out_shape=jax.ShapeDtypeStruct((rmax, D), F32),
        compiler_params=pltpu.CompilerParams(dimension_semantics=("arbitrary",), vmem_limit_bytes=V7X_VMEM_LIMIT),
        name="moe_dispatch",
    )(tab, zst, xa, xb, lpost, nw)


def _expert_kernel(texp_ref, tsrc_ref, tvalid_ref, xs_ref, w1_ref, w3_ref, w2_ref, ys_ref):
    valid = tvalid_ref[pl.program_id(0)]
    half = TM // 2

    @pl.when(valid > half)
    def _():
        xb = xs_ref[...].astype(BF16)
        ys_ref[...] = _swiglu(xb, w1_ref, w3_ref, w2_ref)

    @pl.when((valid > 0) & (valid <= half))
    def _():
        xb = xs_ref[0:half, :].astype(BF16)
        ys_ref[0:half, :] = _swiglu(xb, w1_ref, w3_ref, w2_ref)
        ys_ref[half:, :] = jnp.zeros((TM - half, D), F32)

    @pl.when(valid == 0)
    def _():
        ys_ref[...] = jnp.zeros_like(ys_ref)


def _expert_call(texp, tsrc, tvalid, xs, w1, w3, w2):
    ntile = xs.shape[0] // TM
    return pl.pallas_call(
        _expert_kernel,
        grid_spec=pltpu.PrefetchScalarGridSpec(
            num_scalar_prefetch=3,
            grid=(ntile,),
            in_specs=[
                pl.BlockSpec((TM, D), lambda i, te, ts, na: (ts[i], 0)),
                pl.BlockSpec((None, D, DFF), lambda i, te, ts, na: (te[i], 0, 0)),
                pl.BlockSpec((None, D, DFF), lambda i, te, ts, na: (te[i], 0, 0)),
                pl.BlockSpec((None, DFF, D), lambda i, te, ts, na: (te[i], 0, 0)),
            ],
            out_specs=pl.BlockSpec((TM, D), lambda i, te, ts, na: (i, 0)),
        ),
        out_shape=jax.ShapeDtypeStruct((xs.shape[0], D), F32),
        compiler_params=pltpu.CompilerParams(dimension_semantics=("arbitrary",), vmem_limit_bytes=V7X_VMEM_LIMIT),
        name="moe_experts",
    )(texp, tsrc, tvalid, xs, w1, w3, w2)


def _combine_kernel(tab_ref, x_ref, gate_ref, lpos_ref, nf_ref, ys_ref, o_ref, buf, sem, *, tm, blk0, lr):
    i = pl.program_id(0)
    slot = lax.rem(i, 2)

    def fetch(step, slot_):
        tile = blk0 + step
        last = (tile * NE + NE - 1) * 3
        used = pl.multiple_of(tab_ref[last + 1] + tab_ref[last] * SEG, SEG)

        def clear(k, carry):
            buf[slot_, pl.ds(pl.multiple_of(used + k * SEG, SEG), SEG), :] = jnp.zeros((SEG, D), F32)
            return carry

        lax.fori_loop(0, (lr - used) // SEG, clear, 0)
        _segment_copies(tab_ref, tile, buf.at[slot_], ys_ref, sem.at[slot_], to_global=False, wait=False)

    pl.when(i == 0)(lambda: fetch(i, slot))
    pl.when(i + 1 < pl.num_programs(0))(lambda: fetch(i + 1, 1 - slot))
    _segment_copies(tab_ref, blk0 + i, buf.at[slot], ys_ref, sem.at[slot], to_global=False, wait=True)

    yb = buf[slot].astype(BF16)
    lp = lpos_ref[...]
    cio = _iota2((tm, lr), 1).astype(F32)
    g = gate_ref[...]
    moe = None
    for s in range(2):
        sel = jnp.where(cio == lp[:, s:s + 1], 1.0, 0.0).astype(BF16)
        picked = _dot(sel, yb)
        moe = g[:, s:s + 1] * picked if moe is None else moe + g[:, s:s + 1] * picked
    o_ref[...] = _rms(x_ref[...] + moe, nf_ref[...])


def _combine_call(tab, x, gate, lpos, nf, ys, *, tm, tok0):
    n = x.shape[0]
    blk0 = tok0 // tm
    lr = _local_rows(tm)
    return pl.pallas_call(
        functools.partial(_combine_kernel, tm=tm, blk0=blk0, lr=lr),
        grid_spec=pltpu.PrefetchScalarGridSpec(
            num_scalar_prefetch=1,
            grid=(n // tm,),
            in_specs=[
                pl.BlockSpec((tm, D), lambda i, *_: (i, 0)),
                pl.BlockSpec((tm, 2), lambda i, *_: (blk0 + i, 0)),
                pl.BlockSpec((tm, 2), lambda i, *_: (blk0 + i, 0)),
                pl.BlockSpec((1, D), lambda i, *_: (0, 0)),
                pl.BlockSpec(memory_space=pl.ANY),
            ],
            out_specs=pl.BlockSpec((tm, D), lambda i, *_: (i, 0)),
            scratch_shapes=[pltpu.VMEM((2, lr, D), F32), pltpu.SemaphoreType.DMA((2,))],
        ),
        out_shape=jax.ShapeDtypeStruct((n, D), F32),
        compiler_params=pltpu.CompilerParams(dimension_semantics=("arbitrary",), vmem_limit_bytes=V7X_VMEM_LIMIT),
        name=f"moe_combine_{n}",
    )(tab, x, gate, lpos, nf, ys)


def kernel(x_prompt, x_sample, state_hgrn, state_gdn, state_gdn_conv, meta_tokens, norm_mix, norm_ffn, norm_final,
           hg_w_in, hg_lower_bounds, hg_g_norm, hg_w_o, gdn_w_in, gdn_conv_w, gdn_a_log, gdn_dt_bias, gdn_g_norm,
           gdn_w_o, ffn_w1, ffn_w3, ffn_w2, moe_w_router, moe_w1, moe_w3, moe_w2):
    bp, tp, _ = x_prompt.shape
    bs, ts, _ = x_sample.shape
    np_, ns_ = bp * tp, bs * ts
    n_tok = np_ + ns_
    row = lambda v: v.reshape(1, -1).astype(F32)

    hg_win = hg_w_in.astype(BF16)
    hg_wo = hg_w_o.astype(BF16)
    gdn_win = jnp.concatenate(
        [gdn_w_in[:, :QKV + D], gdn_w_in[:, QKV + D:], jnp.zeros((D, LANES - 2 * H), gdn_w_in.dtype)],
        axis=1).astype(BF16)
    gdn_wo = gdn_w_o.astype(BF16)
    par = jnp.zeros((2, LANES), F32)
    par = par.at[0, H:2 * H].set(gdn_dt_bias.astype(F32)).at[1, H:2 * H].set(gdn_a_log.astype(F32))
    w1d, w3d, w2d = ffn_w1.astype(BF16), ffn_w3.astype(BF16), ffn_w2.astype(BF16)
    w1e, w3e, w2e = moe_w1.astype(BF16), moe_w3.astype(BF16), moe_w2.astype(BF16)
    wr = jnp.concatenate([moe_w_router.astype(F32), jnp.zeros((D, LANES - NE), F32)], axis=1)
    nm0, nm1 = row(norm_mix[0]), row(norm_mix[1])
    nf0, nf1 = row(norm_ffn[0]), row(norm_ffn[1])
    nfin = row(norm_final)
    hg_gn, gdn_gn = row(hg_g_norm), row(gdn_g_norm)
    lbnd = hg_lower_bounds.astype(F32)
    cw = gdn_conv_w.astype(F32)

    hg_args = (nm0, hg_win, lbnd, hg_gn, hg_wo)
    gdn_args = (nm1, gdn_win, cw, par, gdn_gn, gdn_wo)
    xm = meta_tokens.astype(F32)
    xp2 = x_prompt.reshape(np_, D)
    xs2 = x_sample.reshape(ns_, D)

    zero_state = jnp.zeros((1, H, HD, HD), F32)
    h1m, hg_m = _hgrn_call(xm, zero_state, *hg_args, nseq=1, seqlen=N_META, bb=1, tt=N_META, c=N_META)
    h1p, hg_p = _hgrn_call(xp2, hg_m, *hg_args, nseq=bp, seqlen=tp, bb=1, tt=256, c=64, nu=4, bcast_state=True)
    h1s, hg_s = _hgrn_call(xs2, state_hgrn.astype(F32), *hg_args, nseq=bs, seqlen=ts, bb=8, tt=ts, c=64, nu=1)
    h2m = _ffn_call([h1m], nf0, w1d, w3d, w2d, tm=N_META)
    h2 = _ffn_call([h1p, h1s], nf0, w1d, w3d, w2d, tm=512)

    zero_conv = jnp.zeros((1, CONV - 1, QKV), F32)
    _, gdn_m, conv_m = _gdn_call(h2m, zero_state, zero_conv, *gdn_args, nseq=1, seqlen=N_META, bb=1, tt=N_META, c=16)
    h3p, gdn_p, conv_p = _gdn_call(h2, gdn_m, conv_m, *gdn_args, nseq=bp, seqlen=tp, bb=1, tt=256, c=128, nu=2,
                                   bcast_state=True)
    h3s, gdn_s, conv_s = _gdn_call(h2, state_gdn.astype(F32), state_gdn_conv.astype(F32), *gdn_args,
                                   nseq=bs, seqlen=ts, bb=8, tt=ts, c=64, nu=1, row0=np_)

    yp, ysm = _moe_final(h3p, h3s, nf1, wr, w1e, w3e, w2e, nfin, tm=256)

    sd, gd, cd = state_hgrn.dtype, state_gdn.dtype, state_gdn_conv.dtype
    return (yp.reshape(bp, tp, D), ysm.reshape(bs, ts, D),
            hg_p.astype(sd), hg_s.astype(sd),
            gdn_p.astype(gd), gdn_s.astype(gd), conv_p.astype(cd), conv_s.astype(cd))


def _moe_final(h3p, h3s, nf1, wr, w1e, w3e, w2e, nfin, *, tm):
    np_ = h3p.shape[0]
    n_tok = np_ + h3s.shape[0]
    nblk = n_tok // tm
    gate, lpos, lpost, tab, tot = _router_call(h3p, h3s, nf1, wr, tm=tm)
    totals = tot[0, :NE].astype(jnp.int32)
    padded = ((totals + TM - 1) // TM) * TM
    ends = jnp.cumsum(padded)
    offs = ends - padded
    tab3 = tab.reshape(nblk, 8, LANES)[:, :3, :NE].astype(jnp.int32)
    seg_tab = jnp.stack([tab3[:, 0] // SEG, tab3[:, 1], tab3[:, 2] + offs[None, :]], axis=-1).reshape(-1)
    ntile = -(-(2 * n_tok + nblk * NE * (SEG - 1) + NE * (TM - 1)) // TM)
    nact = ends[-1] // TM
    tiles = jnp.arange(ntile, dtype=jnp.int32)
    texp_all = jnp.minimum(jnp.sum(tiles[:, None] * TM >= ends[None, :], axis=1), NE - 1).astype(jnp.int32)
    last = jnp.maximum(nact - 1, 0)
    active = tiles < nact
    texp = jnp.where(active, texp_all, texp_all[last]).astype(jnp.int32)
    tsrc = jnp.where(active, tiles, last).astype(jnp.int32)
    tvalid = jnp.where(active, jnp.clip(totals[texp_all] - (tiles * TM - offs[texp_all]), 0, TM), 0).astype(jnp.int32)
    tail = nact + jnp.arange(ntile - 2 * n_tok // TM)
    zst = jnp.concatenate([jnp.where(padded > 0, ends - TM, -1),
                           jnp.where(tail < ntile, tail * TM, -1)]).astype(jnp.int32)
    xs_sorted = _dispatch_call(seg_tab, zst, h3p, h3s, lpost, nf1, tm=tm, rmax=ntile * TM)
    ys = _expert_call(texp, tsrc, tvalid, xs_sorted, w1e, w3e, w2e)
    yp = _combine_call(seg_tab, h3p, gate, lpos, nfin, ys, tm=tm, tok0=0)
    ysm = _combine_call(seg_tab, h3s, gate, lpos, nfin, ys, tm=tm, tok0=np_)
    return yp, ysm
```

```python
import functools

import jax
import jax.numpy as jnp
from jax import lax
from jax.experimental import pallas as pl
from jax.experimental.pallas import tpu as pltpu

D = 1024
H = 8
HD = 128
QKV = 3 * H * HD
CONV = 4
DFF = 3584
NE = 8
N_META = 16
EPS = 1e-6
LANES = 128
FB = 512
TM = 512
HG_SUB = 16
SEG = 8
V7X_VMEM_LIMIT = 56 * 1024 * 1024

F32 = jnp.float32
BF16 = jnp.bfloat16


def _dot(a, b):
    return jnp.dot(a, b, preferred_element_type=F32)


def _dot_nt(a, b):
    return lax.dot_general(a, b, (((1,), (1,)), ((), ())), preferred_element_type=F32)


def _dot_tn(a, b):
    return lax.dot_general(a, b, (((0,), (0,)), ((), ())), preferred_element_type=F32)


def _split3(x):
    x1 = x.astype(BF16)
    r1 = x - x1.astype(F32)
    x2 = r1.astype(BF16)
    x3 = (r1 - x2.astype(F32)).astype(BF16)
    return x1, x2, x3


def _sel_dot(sel, x):
    x1, x2, x3 = _split3(x)
    return _dot(sel, x1) + _dot(sel, x2) + _dot(sel, x3)


def _dot_3x(a, b):
    a_hi = a.astype(BF16)
    a_lo = (a - a_hi.astype(F32)).astype(BF16)
    b_hi = b.astype(BF16)
    b_lo = (b - b_hi.astype(F32)).astype(BF16)
    return _dot(a_hi, b_hi) + (_dot(a_hi, b_lo) + _dot(a_lo, b_hi))


def _sel_dot_nt(sel, x):
    x1, x2, x3 = _split3(x)
    return _dot_nt(sel, x1) + _dot_nt(sel, x2) + _dot_nt(sel, x3)


def _rms(x, g):
    return x * lax.rsqrt(jnp.mean(x * x, axis=-1, keepdims=True) + EPS) * g


def _silu(x):
    return x * jax.nn.sigmoid(x)


def _iota2(shape, dim):
    return lax.broadcasted_iota(jnp.int32, shape, dim)


def _head_rms_gate(o, g, gate):
    parts = []
    for h in range(H):
        oh = o[:, h * HD:(h + 1) * HD]
        parts.append(_rms(oh, g))
    return jnp.concatenate(parts, axis=1) * gate


def _chunk_row(idx, j, nu, c):
    if isinstance(idx, int):
        return (idx * nu + j) * c
    return pl.multiple_of((idx * nu + j) * c, c)


def _chunk_seq(idx, j, nu, ncs):
    return (idx * nu + j) // ncs


def _chunk_ops(tri, c):
    assert c % 16 == 0
    tri_b = tri.astype(BF16)
    return (lambda x: _sel_dot(tri_b, x)), (lambda x: x.astype(BF16))


def _run_groups(group, n):
    if n == 1:
        group(0, 0)
    else:
        lax.fori_loop(0, n, group, 0)


def _const_spec(shape):
    nd = len(shape)
    return pl.BlockSpec(shape, lambda *_: (0,) * nd, pipeline_mode=pl.Buffered(1))


def _hgrn_kernel(x_ref, s0_ref, nw_ref, win_ref, lb_ref, gn_ref, wo_ref,
                 o_ref, sout_ref,
                 st_scr, q_scr, k_scr, lf_scr, v_scr, g_scr, *, bb, tt, c, nu):
    t = pl.program_id(1)
    rows = bb * tt
    sl = min(c, tt)
    spc = c // sl
    ncs = max(tt // c, 1)
    assert c % sl == 0 and (ncs == 1 or ncs % nu == 0)

    @pl.when(t == 0)
    def _():
        for b in range(bb):
            for h in range(H):
                st_scr[b, h] = s0_ref[b, h].T

    x = x_ref[...]
    xn = _rms(x, nw_ref[...]).astype(BF16)
    lbn = lb_ref[...]
    e = jnp.exp(lbn - jnp.max(lbn, axis=0, keepdims=True))
    lb = e[0:1] / jnp.sum(e, axis=0, keepdims=True)
    q_scr[...] = _silu(_dot(xn, win_ref[:, 0:D])) * HD ** -0.5
    f = lb + (1.0 - lb) * jax.nn.sigmoid(_dot(xn, win_ref[:, D:2 * D]))
    k_scr[...] = 1.0 - f
    lf_scr[...] = jnp.log(f)
    v_scr[...] = _dot(xn, win_ref[:, 2 * D:3 * D])
    g_scr[...] = _silu(_dot(xn, win_ref[:, 3 * D:4 * D]))

    row = _iota2((c, c), 0)
    col = _iota2((c, c), 1)
    tri = (row >= col) & ((row // sl) == (col // sl))
    cumsum, mxu = _chunk_ops(tri, c)
    heads = [slice(h * HD, (h + 1) * HD) for h in range(H)]
    nsub = c // HG_SUB if sl > HG_SUB else 1
    if nsub > 1:
        local_tri = (tri & ((row // HG_SUB) == (col // HG_SUB))).astype(BF16)
        slab_row = _iota2((c, D), 0)
    if spc > 1:
        seq_last = (col == (row // sl) * sl + (sl - 1)).astype(BF16)

    def chunk_terms(r0):
        lf = lf_scr[pl.ds(r0, c), :]
        bc = cumsum(lf)
        tot = bc[c - 1:c, :] if spc == 1 else _sel_dot(seq_last, bc)
        qq = q_scr[pl.ds(r0, c), :]
        kk = k_scr[pl.ds(r0, c), :]
        v32 = v_scr[pl.ds(r0, c), :]
        vv = mxu(v32)
        qd32 = qq * jnp.exp(bc)
        kt32 = kk * jnp.exp(tot - bc)
        qdj = mxu(qd32)
        ktail = mxu(kt32)
        if nsub == 1:
            kinv = mxu(kk * jnp.exp(-bc))
            scj = [_dot_nt(qdj[:, s], kinv[:, s]) for s in heads]
        else:
            bl = _sel_dot(local_tri, lf)
            qloc = mxu(qq * jnp.exp(bl))
            parts = [[] for _ in heads]
            for i in range(nsub):
                lo, hi = i * HG_SUB, (i + 1) * HG_SUB
                ref_pt = bc[lo - 1:lo, :] if i else 0.0
                kci = mxu(jnp.where(slab_row < hi, kk * jnp.exp(ref_pt - bc), 0.0))
                for h, s in enumerate(heads):
                    parts[h].append(_dot_nt(qloc[lo:hi, s], kci[:, s]))
            scj = [jnp.concatenate(p, axis=0) for p in parts]
        scj = [mxu(jnp.where(tri, sc, 0.0)) for sc in scj]
        oij = [_dot(scj[h], vv[:, heads[h]]) for h in range(H)]
        if spc > 1:
            return qd32, jnp.exp(tot), (v32, kt32), oij
        kvj = [_dot_tn(vv[:, s], ktail[:, s]) for s in heads]
        return qdj, jnp.exp(tot), kvj, oij

    def group(idx, carry):
        qd, dec, kv, oi, r0s, sqs = [], [], [], [], [], []
        for j in range(nu):
            r0 = _chunk_row(idx, j, nu, c)
            qdj, decj, kvj, oij = chunk_terms(r0)
            kv.append(kvj)
            oi.append(oij)
            qd.append(qdj)
            dec.append(decj)
            r0s.append(r0)
            sqs.append(_chunk_seq(idx, j, nu, ncs) if spc == 1 else (idx * nu + j) * spc)

        if spc > 1:
            own_block = (_iota2((c, spc * HD), 1) // HD) == (_iota2((c, spc * HD), 0) // sl)
            outs, finals = [], []
            for j in range(nu):
                v32, kt32 = kv[j]
                o_j = []
                for h, s in enumerate(heads):
                    st_all = jnp.concatenate([st_scr[sqs[j] + q, h] for q in range(spc)], axis=0)
                    own = jnp.where(own_block, _dot_nt(mxu(qd[j][:, s]), mxu(st_all)), 0.0)
                    inter = own[:, 0:HD]
                    for q in range(1, spc):
                        inter = inter + own[:, q * HD:(q + 1) * HD]
                    o_j.append(oi[j][h] + inter)
                    decay = jnp.concatenate(
                        [jnp.broadcast_to(dec[j][q * sl:q * sl + 1, s], (HD, HD)) for q in range(spc)], axis=0)
                    v_spread = mxu(jnp.where(own_block, jnp.concatenate([v32[:, s]] * spc, axis=1), 0.0))
                    st_new = st_all * decay + _dot_tn(v_spread, mxu(kt32[:, s]))
                    finals.extend((sqs[j] + q, h, st_new[q * HD:(q + 1) * HD]) for q in range(spc))
                outs.append(o_j)
            for j in range(nu):
                q_scr[pl.ds(r0s[j], c), :] = jnp.concatenate(outs[j], axis=1)
            for sq, h, st_new in finals:
                st_scr[sq, h] = st_new
            return carry

        own_state = ncs == 1
        loaded = [[st_scr[sqs[j], h] for h in range(H)] for j in range(nu if own_state else 1)]
        outs, finals = [], []
        st = loaded[0]
        for j in range(nu):
            if own_state:
                st = loaded[j]
            outs.append([oi[j][h] + _dot_nt(qd[j][:, heads[h]], mxu(st[h])) for h in range(H)])
            st = [st[h] * dec[j][:, heads[h]] + kv[j][h] for h in range(H)]
            if own_state or j == nu - 1:
                finals.append((sqs[j], st))
        for j in range(nu):
            q_scr[pl.ds(r0s[j], c), :] = jnp.concatenate(outs[j], axis=1)
        for sq, st in finals:
            for h in range(H):
                st_scr[sq, h] = st[h]
        return carry

    _run_groups(group, rows // (nu * c))

    on = _head_rms_gate(q_scr[...], gn_ref[...], g_scr[...])
    o_ref[...] = x + _dot(on.astype(BF16), wo_ref[...])

    @pl.when(t == pl.num_programs(1) - 1)
    def _():
        for b in range(bb):
            for h in range(H):
                sout_ref[b, h] = st_scr[b, h].T


def _hgrn_call(x2d, s0, nw, win, lbnd, gn, wo, *, nseq, seqlen, bb, tt, c, nu=4, row0=0, bcast_state=False):
    assert bb == 1 or tt == seqlen
    assert nseq % bb == 0 and seqlen % tt == 0 and row0 % (bb * tt) == 0
    assert tt % c == 0 or (c % tt == 0 and (bb * tt) % c == 0)
    rows = bb * tt
    nu = min(nu, rows // c)
    nt = seqlen // tt
    blk0 = row0 // rows
    s_idx = (lambda b, t: (0, 0, 0, 0)) if bcast_state else (lambda b, t: (b, 0, 0, 0))
    slab = pltpu.VMEM((rows, D), F32)
    return pl.pallas_call(
        functools.partial(_hgrn_kernel, bb=bb, tt=tt, c=c, nu=nu),
        grid=(nseq // bb, nt),
        in_specs=[
            pl.BlockSpec((rows, D), lambda b, t: (blk0 + b * nt + t, 0)),
            pl.BlockSpec((bb, H, HD, HD), s_idx),
            _const_spec((1, D)),
            _const_spec((D, 4 * D)),
            _const_spec(lbnd.shape),
            _const_spec((1, HD)),
            _const_spec((D, D)),
        ],
        out_specs=[
            pl.BlockSpec((rows, D), lambda b, t: (b * nt + t, 0)),
            pl.BlockSpec((bb, H, HD, HD), lambda b, t: (b, 0, 0, 0)),
        ],
        out_shape=[
            jax.ShapeDtypeStruct((nseq * seqlen, D), F32),
            jax.ShapeDtypeStruct((nseq, H, HD, HD), F32),
        ],
        scratch_shapes=[pltpu.VMEM((bb, H, HD, HD), F32), slab, slab, slab, slab, slab],
        compiler_params=pltpu.CompilerParams(
            dimension_semantics=("arbitrary", "arbitrary"), vmem_limit_bytes=V7X_VMEM_LIMIT),
        name=f"hgrn_mixer_{nseq}x{seqlen}",
    )(x2d, s0, nw, win, lbnd, gn, wo)


def _neumann_inv(ls, eye, levels, mxu):
    xs = [eye - l for l in ls]
    ps = [mxu(l) for l in ls]
    for _ in range(levels - 1):
        ps = [mxu(_dot(p, p)) for p in ps]
        xs = [x + _dot(mxu(x), p) for x, p in zip(xs, ps)]
    return xs


def _unit_lower_inv(ls, c, mxu, span=None):
    span = c if span is None else span
    row = _iota2((c, c), 0)
    col = _iota2((c, c), 1)
    eye = (row == col).astype(F32)
    if span <= 16:
        return _neumann_inv(ls, eye, span.bit_length() - 1, mxu)
    sub = 16
    nblk = c // sub
    assert c == nblk * sub and nblk & (nblk - 1) == 0
    diag = (row // sub) == (col // sub)
    lds = [jnp.where(diag, l, 0.0) for l in ls]
    dinvs = [mxu(d) for d in _neumann_inv(lds, eye, 4, mxu)]
    ms = [_dot(d, mxu(l - ld)) for d, l, ld in zip(dinvs, ls, lds)]
    ys = _neumann_inv(ms, eye, nblk.bit_length() - 1, mxu)
    return [_dot(mxu(y), d) for y, d in zip(ys, dinvs)]


def _gdn_kernel(x_ref, s0_ref, c0_ref, nw_ref, win_ref, cw_ref, par_ref, gn_ref, wo_ref,
                o_ref, sout_ref, cout_ref,
                s_scr, xp_scr, act_scr, z_scr, la_scr, be_scr, o_scr, *, bb, tt, c, nu):
    t = pl.program_id(1)
    rows = bb * tt
    sl = min(c, tt)
    spc = c // sl
    ncs = max(tt // c, 1)
    assert c % sl == 0 and (ncs == 1 or ncs % nu == 0)

    @pl.when(t == 0)
    def _():
        s_scr[...] = s0_ref[...]
        for b in range(bb):
            xp_scr[b, 8 - (CONV - 1):8, :] = c0_ref[b]

    x = x_ref[...]
    xn = _rms(x, nw_ref[...]).astype(BF16)
    pre = _dot(xn, win_ref[:, 0:QKV])
    for b in range(bb):
        xp_scr[b, 8:8 + tt, :] = pre[b * tt:(b + 1) * tt]
    z_scr[...] = _silu(_dot(xn, win_ref[:, QKV:QKV + D]))
    ba = _dot(xn, win_ref[:, QKV + D:QKV + D + LANES])
    par = par_ref[...]
    sp = ba + par[0:1]
    softplus = jnp.maximum(sp, 0.0) + jnp.log(1.0 + jnp.exp(-jnp.abs(sp)))
    la_scr[...] = -jnp.exp(par[1:2]) * softplus
    be_scr[...] = jax.nn.sigmoid(ba)

    cw = cw_ref[...]
    for b in range(bb):
        conv = xp_scr[b, 8 - (CONV - 1):8 - (CONV - 1) + tt, :] * cw[0:1]
        for w in range(1, CONV):
            conv = conv + xp_scr[b, 8 - (CONV - 1) + w:8 - (CONV - 1) + w + tt, :] * cw[w:w + 1]
        act = _silu(conv)
        for j in range(2 * H):
            a = act[:, j * HD:(j + 1) * HD]
            scale = HD ** -0.5 if j < H else 1.0
            act_scr[b * tt:(b + 1) * tt, j * HD:(j + 1) * HD] = (
                a * lax.rsqrt(jnp.sum(a * a, axis=-1, keepdims=True) + EPS) * scale)
        act_scr[b * tt:(b + 1) * tt, 2 * H * HD:] = act[:, 2 * H * HD:]
        tail = xp_scr[b, tt:tt + 8, :]
        xp_scr[b, 0:8, :] = tail

    row = _iota2((c, c), 0)
    col = _iota2((c, c), 1)
    same_seq = (row // sl) == (col // sl)
    incl = (row >= col) & same_seq
    strict = (row > col) & same_seq
    cumsum, mxu = _chunk_ops(incl, c)
    eye_l = (_iota2((LANES, LANES), 0) == _iota2((LANES, LANES), 1)).astype(BF16)
    transpose = lambda m: _sel_dot_nt(eye_l, m)
    if spc > 1:
        seq_last = (col == (row // sl) * sl + (sl - 1)).astype(BF16)
    narrow = mxu if spc == 1 else (lambda a: a)

    heads = [slice(h * HD, (h + 1) * HD) for h in range(H)]

    def group(idx, carry):
        qd, ktail, dtot, lmat, amat, rwu, r0s, sqs = [], [], [], [], [], [], [], []
        for j in range(nu):
            r0 = _chunk_row(idx, j, nu, c)
            gc = cumsum(la_scr[pl.ds(r0, c), :])
            gt = transpose(gc)
            be = be_scr[pl.ds(r0, c), :]
            eg = jnp.exp(gc)
            gl = gc[c - 1:c, :] if spc == 1 else _sel_dot(seq_last, gc)
            et = jnp.exp(gl - gc)
            dt = jnp.exp(gl)
            qq = act_scr[pl.ds(r0, c), 0:H * HD]
            kk = act_scr[pl.ds(r0, c), H * HD:2 * H * HD]
            vv = act_scr[pl.ds(r0, c), 2 * H * HD:]
            qd_j, ktail_j, dtot_j = [], [], []
            for h, s in enumerate(heads):
                g_col = gc[:, 8 + h:9 + h]
                eg_col = eg[:, 8 + h:9 + h]
                be_col = be[:, h:h + 1]
                kh = mxu(kk[:, s])
                qk = _dot_nt(jnp.concatenate([mxu(qq[:, s]), kh], axis=0), kh)
                ed = jnp.exp(g_col - gt[8 + h:9 + h, :])
                lmat.append(be_col * qk[c:] * jnp.where(strict, ed, 0.0))
                amat.append(mxu(qk[:c] * jnp.where(incl, ed, 0.0)))
                rwu.append(mxu(jnp.concatenate([kk[:, s] * (be_col * eg_col), vv[:, s] * be_col], axis=1)))
                qd_j.append(narrow(qq[:, s] * eg_col))
                ktail_j.append(narrow(kk[:, s] * et[:, 8 + h:9 + h]))
                dtot_j.append(dt[:, 8 + h:9 + h])
            qd.append(qd_j)
            ktail.append(ktail_j)
            dtot.append(dtot_j)
            r0s.append(r0)
            sqs.append(_chunk_seq(idx, j, nu, ncs) if spc == 1 else (idx * nu + j) * spc)
        tinv = _unit_lower_inv(lmat, c, mxu, span=sl)
        wu = [_dot(mxu(t_), r_) for t_, r_ in zip(tinv, rwu)]

        if spc > 1:
            own_block = (_iota2((c, spc * HD), 1) // HD) == (_iota2((c, spc * HD), 0) // sl)
            spread = lambda a: mxu(jnp.where(own_block, jnp.concatenate([a] * spc, axis=1), 0.0))
            outs, finals = [], []
            for j in range(nu):
                o_j = []
                for h in range(H):
                    wuh = wu[j * H + h]
                    s_all = jnp.concatenate([s_scr[sqs[j] + q, h] for q in range(spc)], axis=0)
                    qw = _dot(jnp.concatenate([spread(qd[j][h]), spread(wuh[:, :HD])], axis=0), mxu(s_all))
                    u = wuh[:, HD:] - qw[c:]
                    ub = mxu(u)
                    o_j.append(qw[:c] + _dot(amat[j * H + h], ub))
                    decay = jnp.concatenate(
                        [jnp.broadcast_to(dtot[j][h][q * sl:q * sl + 1], (HD, HD)) for q in range(spc)], axis=0)
                    s_new = s_all * decay + _dot_tn(spread(ktail[j][h]), ub)
                    finals.extend((sqs[j] + q, h, s_new[q * HD:(q + 1) * HD]) for q in range(spc))
                outs.append(o_j)
            for j in range(nu):
                o_scr[pl.ds(r0s[j], c), :] = jnp.concatenate(outs[j], axis=1)
            for sq, h, s_new in finals:
                s_scr[sq, h] = s_new
            return carry

        own_state = ncs == 1
        loaded = [[s_scr[sqs[j], h] for h in range(H)] for j in range(nu if own_state else 1)]
        outs, finals = [], []
        st = loaded[0]
        for j in range(nu):
            if own_state:
                st = loaded[j]
            o_j, st_next = [], []
            for h in range(H):
                wuh = wu[j * H + h]
                qw = _dot(jnp.concatenate([qd[j][h], mxu(wuh[:, :HD])], axis=0), mxu(st[h]))
                u = wuh[:, HD:] - qw[c:]
                ub = mxu(u)
                o_j.append(qw[:c] + _dot(amat[j * H + h], ub))
                st_next.append(st[h] * dtot[j][h] + _dot_tn(ktail[j][h], ub))
            st = st_next
            outs.append(o_j)
            if own_state or j == nu - 1:
                finals.append((sqs[j], st))
        for j in range(nu):
            o_scr[pl.ds(r0s[j], c), :] = jnp.concatenate(outs[j], axis=1)
        for sq, st in finals:
            for h in range(H):
                s_scr[sq, h] = st[h]
        return carry

    _run_groups(group, rows // (nu * c))

    on = _head_rms_gate(o_scr[...], gn_ref[...], z_scr[...])
    o_ref[...] = x + _dot(on.astype(BF16), wo_ref[...])

    @pl.when(t == pl.num_programs(1) - 1)
    def _():
        sout_ref[...] = s_scr[...]
        for b in range(bb):
            cout_ref[b] = xp_scr[b, 8 - (CONV - 1):8, :]


def _gdn_call(x2d, s0, c0, nw, win, cw, par, gn, wo, *, nseq, seqlen, bb, tt, c, nu=2, row0=0,
              bcast_state=False):
    assert bb == 1 or tt == seqlen
    assert nseq % bb == 0 and seqlen % tt == 0 and tt % 8 == 0 and row0 % (bb * tt) == 0
    assert tt % c == 0 or (c % tt == 0 and (bb * tt) % c == 0)
    rows = bb * tt
    nt = seqlen // tt
    blk0 = row0 // rows
    s_idx = (lambda b, t: (0, 0, 0, 0)) if bcast_state else (lambda b, t: (b, 0, 0, 0))
    c_idx = (lambda b, t: (0, 0, 0)) if bcast_state else (lambda b, t: (b, 0, 0))
    slab = pltpu.VMEM((rows, D), F32)
    return pl.pallas_call(
        functools.partial(_gdn_kernel, bb=bb, tt=tt, c=c, nu=min(nu, bb * tt // c)),
        grid=(nseq // bb, nt),
        in_specs=[
            pl.BlockSpec((rows, D), lambda b, t: (blk0 + b * nt + t, 0)),
            pl.BlockSpec((bb, H, HD, HD), s_idx),
            pl.BlockSpec((bb, CONV - 1, QKV), c_idx),
            _const_spec((1, D)),
            _const_spec(win.shape),
            _const_spec((CONV, QKV)),
            _const_spec((2, LANES)),
            _const_spec((1, HD)),
            _const_spec((D, D)),
        ],
        out_specs=[
            pl.BlockSpec((rows, D), lambda b, t: (b * nt + t, 0)),
            pl.BlockSpec((bb, H, HD, HD), lambda b, t: (b, 0, 0, 0)),
            pl.BlockSpec((bb, CONV - 1, QKV), lambda b, t: (b, 0, 0)),
        ],
        out_shape=[
            jax.ShapeDtypeStruct((nseq * seqlen, D), F32),
            jax.ShapeDtypeStruct((nseq, H, HD, HD), F32),
            jax.ShapeDtypeStruct((nseq, CONV - 1, QKV), F32),
        ],
        scratch_shapes=[
            pltpu.VMEM((bb, H, HD, HD), F32),
            pltpu.VMEM((bb, tt + 8, QKV), F32),
            pltpu.VMEM((rows, QKV), F32),
            slab,
            pltpu.VMEM((rows, LANES), F32),
            pltpu.VMEM((rows, LANES), F32),
            slab,
        ],
        compiler_params=pltpu.CompilerParams(
            dimension_semantics=("arbitrary", "arbitrary"), vmem_limit_bytes=V7X_VMEM_LIMIT),
        name=f"gdn_mixer_{nseq}x{seqlen}",
    )(x2d, s0, c0, nw, win, cw, par, gn, wo)


def _swiglu(xb, w1_ref, w3_ref, w2_ref):
    acc = None
    for j in range(DFF // FB):
        fs = slice(j * FB, (j + 1) * FB)
        hh = (_silu(_dot(xb, w1_ref[:, fs])) * _dot(xb, w3_ref[:, fs])).astype(BF16)
        part = _dot(hh, w2_ref[fs, :])
        acc = part if acc is None else acc + part
    return acc


def _ffn_kernel(*refs, n_in, nblk_first):
    x_refs = refs[:n_in]
    nw_ref, w1_ref, w3_ref, w2_ref, o_ref = refs[n_in:]
    x = x_refs[0][...]
    if n_in == 2:
        x = jnp.where(pl.program_id(0) < nblk_first, x, x_refs[1][...])
    xb = _rms(x, nw_ref[...]).astype(BF16)
    o_ref[...] = x + _swiglu(xb, w1_ref, w3_ref, w2_ref)


def _two_group_specs(tm, nblk_first, nblk_total):
    last_first = nblk_first - 1
    return [
        pl.BlockSpec((tm, D), lambda i, *_: (jnp.minimum(i, last_first), 0)),
        pl.BlockSpec((tm, D), lambda i, *_: (jnp.maximum(i - nblk_first, 0), 0)),
    ]


def _ffn_call(xs, nw, w1, w3, w2, *, tm):
    nblks = [x.shape[0] // tm for x in xs]
    assert all(x.shape[0] % tm == 0 for x in xs)
    total = sum(nblks)
    if len(xs) == 2:
        x_specs = _two_group_specs(tm, nblks[0], total)
    else:
        x_specs = [pl.BlockSpec((tm, D), lambda i: (i, 0))]
    return pl.pallas_call(
        functools.partial(_ffn_kernel, n_in=len(xs), nblk_first=nblks[0]),
        grid=(total,),
        in_specs=x_specs + [_const_spec((1, D)), _const_spec((D, DFF)), _const_spec((D, DFF)), _const_spec((DFF, D))],
        out_specs=pl.BlockSpec((tm, D), lambda i: (i, 0)),
        out_shape=jax.ShapeDtypeStruct((total * tm, D), F32),
        compiler_params=pltpu.CompilerParams(dimension_semantics=("arbitrary",), vmem_limit_bytes=V7X_VMEM_LIMIT),
        name=f"ffn_dense_{total * tm}",
    )(*xs, nw, w1, w3, w2)


def _router_kernel(xa_ref, xb_ref, nw_ref, wr_ref, gate_ref, lpos_ref, lpost_ref, tab_ref, tot_ref, seg_scr, *,
                   tm, nblk_first):
    i = pl.program_id(0)

    @pl.when(i == 0)
    def _():
        seg_scr[...] = jnp.zeros_like(seg_scr)

    x = jnp.where(i < nblk_first, xa_ref[...], xb_ref[...])
    xn = _rms(x, nw_ref[...])
    logits = _dot_3x(xn, wr_ref[...])
    lane = _iota2((tm, LANES), 1).astype(F32)
    neg = jnp.float32(-jnp.inf)
    l1 = jnp.where(lane < NE, logits, neg)
    m1 = jnp.max(l1, axis=-1, keepdims=True)
    i1 = jnp.min(jnp.where(l1 == m1, lane, float(LANES)), axis=-1, keepdims=True)
    l2 = jnp.where(lane == i1, neg, l1)
    m2 = jnp.max(l2, axis=-1, keepdims=True)
    i2 = jnp.min(jnp.where(l2 == m2, lane, float(LANES)), axis=-1, keepdims=True)
    e2 = jnp.exp(m2 - m1)
    g1 = 1.0 / (1.0 + e2)
    g2 = e2 / (1.0 + e2)

    oh1 = (lane == i1).astype(F32)
    oh2 = (lane == i2).astype(F32)
    below = (_iota2((tm, tm), 0) > _iota2((tm, tm), 1)).astype(BF16)
    c1 = _dot(below, oh1.astype(BF16))
    c2 = _dot(below, oh2.astype(BF16))
    n1 = jnp.sum(oh1, axis=0, keepdims=True)
    n2 = jnp.sum(oh2, axis=0, keepdims=True)
    cnt = jnp.floor((n1 + n2 + (SEG - 1)) * (1.0 / SEG)) * SEG
    before = (_iota2((LANES, LANES), 0) < _iota2((LANES, LANES), 1)).astype(BF16)
    start = _dot(jnp.broadcast_to(cnt, (16, LANES)).astype(BF16), before)[0:1]
    p1 = jnp.sum(oh1 * (start + c1), axis=-1, keepdims=True)
    p2 = jnp.sum(oh2 * (start + n1 + c2), axis=-1, keepdims=True)
    used = seg_scr[...]
    seg_scr[...] = used + cnt
    tot_ref[...] = used + cnt

    gate_ref[:, 0:1] = g1
    gate_ref[:, 1:2] = g2
    lpos_ref[:, 0:1] = p1
    lpos_ref[:, 1:2] = p2
    sub = _iota2((8, LANES), 0)
    tab_ref[...] = jnp.where(sub == 0, cnt, jnp.where(sub == 1, start, jnp.where(sub == 2, used, 0.0)))
    slab = jnp.where(lane == 0.0, p1, jnp.where(lane == 1.0, p2, 0.0))
    eye16 = (_iota2((16, LANES), 0) == _iota2((16, LANES), 1)).astype(BF16)
    lpost_ref[...] = _sel_dot_nt(eye16, slab)[0:8]


def _router_call(xa, xb, nw, wr, *, tm):
    na, nb = xa.shape[0] // tm, xb.shape[0] // tm
    nblk = na + nb
    n = nblk * tm
    return pl.pallas_call(
        functools.partial(_router_kernel, tm=tm, nblk_first=na),
        grid=(nblk,),
        in_specs=_two_group_specs(tm, na, nblk) + [_const_spec((1, D)), _const_spec((D, LANES))],
        out_specs=[
            pl.BlockSpec((tm, 2), lambda i: (i, 0)),
            pl.BlockSpec((tm, 2), lambda i: (i, 0)),
            pl.BlockSpec((8, tm), lambda i: (i, 0)),
            pl.BlockSpec((8, LANES), lambda i: (i, 0)),
            pl.BlockSpec((1, LANES), lambda i: (0, 0)),
        ],
        out_shape=[
            jax.ShapeDtypeStruct((n, 2), F32),
            jax.ShapeDtypeStruct((n, 2), F32),
            jax.ShapeDtypeStruct((nblk * 8, tm), F32),
            jax.ShapeDtypeStruct((nblk * 8, LANES), F32),
            jax.ShapeDtypeStruct((1, LANES), F32),
        ],
        scratch_shapes=[pltpu.VMEM((1, LANES), F32)],
        compiler_params=pltpu.CompilerParams(dimension_semantics=("arbitrary",), vmem_limit_bytes=V7X_VMEM_LIMIT),
        name="moe_router",
    )(xa, xb, nw, wr)


def _local_rows(tm):
    return -(-(2 * tm + NE * (SEG - 1)) // LANES) * LANES


def _segment_copies(tab_ref, tile, local_ref, global_ref, sem, *, to_global, wait):
    for e in range(NE):
        base = (tile * NE + e) * 3
        npiece, lstart, gstart = tab_ref[base], tab_ref[base + 1], tab_ref[base + 2]

        def piece(k, carry, lstart=lstart, gstart=gstart):
            loc = local_ref.at[pl.ds(pl.multiple_of(lstart + k * SEG, SEG), SEG)]
            glo = global_ref.at[pl.ds(pl.multiple_of(gstart + k * SEG, SEG), SEG)]
            cp = pltpu.make_async_copy(loc, glo, sem) if to_global else pltpu.make_async_copy(glo, loc, sem)
            if wait:
                cp.wait()
            else:
                cp.start()
            return carry

        lax.fori_loop(0, npiece, piece, 0)


def _dispatch_kernel(tab_ref, zst_ref, xa_ref, xb_ref, lpost_ref, nw_ref, xs_ref, buf, zero_scr, sem, zsem, *,
                     tm, nblk_first, lr):
    i = pl.program_id(0)
    slot = lax.rem(i, 2)

    def zero_copy(e):
        start = pl.multiple_of(zst_ref[e], TM)
        return pltpu.make_async_copy(zero_scr, xs_ref.at[pl.ds(start, TM)], zsem)

    @pl.when(i == 0)
    def _():
        zero_scr[...] = jnp.zeros_like(zero_scr)
        for e in range(zst_ref.shape[0]):
            pl.when(zst_ref[e] >= 0)(lambda e=e: zero_copy(e).start())

    x = jnp.where(i < nblk_first, xa_ref[...], xb_ref[...])
    xn = _rms(x, nw_ref[...]).astype(BF16)
    lpt = lpost_ref[...]
    rio = _iota2((lr, tm), 0).astype(F32)
    sel = jnp.where(rio == lpt[0:1, :], 1.0, jnp.where(rio == lpt[1:2, :], 1.0, 0.0)).astype(BF16)
    buf[slot] = _dot(sel, xn)

    @pl.when(i == 0)
    def _():
        for e in range(zst_ref.shape[0]):
            pl.when(zst_ref[e] >= 0)(lambda e=e: zero_copy(e).wait())

    copies = functools.partial(_segment_copies, tab_ref, global_ref=xs_ref, to_global=True)
    copies(i, local_ref=buf.at[slot], sem=sem.at[slot], wait=False)
    pl.when(i > 0)(lambda: copies(i - 1, local_ref=buf.at[1 - slot], sem=sem.at[1 - slot], wait=True))
    pl.when(i == pl.num_programs(0) - 1)(
        lambda: copies(i, local_ref=buf.at[slot], sem=sem.at[slot], wait=True))


def _dispatch_call(tab, zst, xa, xb, lpost, nw, *, tm, rmax):
    na, nb = xa.shape[0] // tm, xb.shape[0] // tm
    lr = _local_rows(tm)
    return pl.pallas_call(
        functools.partial(_dispatch_kernel, tm=tm, nblk_first=na, lr=lr),
        grid_spec=pltpu.PrefetchScalarGridSpec(
            num_scalar_prefetch=2,
            grid=(na + nb,),
            in_specs=_two_group_specs(tm, na, na + nb) + [pl.BlockSpec((8, tm), lambda i, *_: (i, 0)),
                                                          pl.BlockSpec((1, D), lambda i, *_: (0, 0))],
            out_specs=pl.BlockSpec(memory_space=pl.ANY),
            scratch_shapes=[pltpu.VMEM((2, lr, D), F32), pltpu.VMEM((TM, D), F32),
                            pltpu.SemaphoreType.DMA((2,)), pltpu.SemaphoreType.DMA],
        ),
        out_shape=jax.ShapeDtypeStruct((rmax, D), F32),
        compiler_params=pltpu.CompilerParams(dimension_semantics=("arbitrary",), vmem_limit_bytes=V7X_VMEM_LIMIT),
        name="moe_dispatch",
    )(tab, zst, xa, xb, lpost, nw)


def _expert_kernel(texp_ref, tsrc_ref, tvalid_ref, xs_ref, w1_ref, w3_ref, w2_ref, ys_ref):
    valid = tvalid_ref[pl.program_id(0)]
    half = TM // 2

    @pl.when(valid > half)
    def _():
        xb = xs_ref[...].astype(BF16)
        ys_ref[...] = _swiglu(xb, w1_ref, w3_ref, w2_ref)

    @pl.when((valid > 0) & (valid <= half))
    def _():
        xb = xs_ref[0:half, :].astype(BF16)
        ys_ref[0:half, :] = _swiglu(xb, w1_ref, w3_ref, w2_ref)
        ys_ref[half:, :] = jnp.zeros((TM - half, D), F32)

    @pl.when(valid == 0)
    def _():
        ys_ref[...] = jnp.zeros_like(ys_ref)


def _expert_call(texp, tsrc, tvalid, xs, w1, w3, w2):
    ntile = xs.shape[0] // TM
    return pl.pallas_call(
        _expert_kernel,
        grid_spec=pltpu.PrefetchScalarGridSpec(
            num_scalar_prefetch=3,
            grid=(ntile,),
            in_specs=[
                pl.BlockSpec((TM, D), lambda i, te, ts, na: (ts[i], 0)),
                pl.BlockSpec((None, D, DFF), lambda i, te, ts, na: (te[i], 0, 0)),
                pl.BlockSpec((None, D, DFF), lambda i, te, ts, na: (te[i], 0, 0)),
                pl.BlockSpec((None, DFF, D), lambda i, te, ts, na: (te[i], 0, 0)),
            ],
            out_specs=pl.BlockSpec((TM, D), lambda i, te, ts, na: (i, 0)),
        ),
        out_shape=jax.ShapeDtypeStruct((xs.shape[0], D), F32),
        compiler_params=pltpu.CompilerParams(dimension_semantics=("arbitrary",), vmem_limit_bytes=V7X_VMEM_LIMIT),
        name="moe_experts",
    )(texp, tsrc, tvalid, xs, w1, w3, w2)


def _combine_kernel(tab_ref, x_ref, gate_ref, lpos_ref, nf_ref, ys_ref, o_ref, buf, sem, *, tm, blk0, lr):
    i = pl.program_id(0)
    slot = lax.rem(i, 2)

    def fetch(step, slot_):
        tile = blk0 + step
        last = (tile * NE + NE - 1) * 3
        used = pl.multiple_of(tab_ref[last + 1] + tab_ref[last] * SEG, SEG)

        def clear(k, carry):
            buf[slot_, pl.ds(pl.multiple_of(used + k * SEG, SEG), SEG), :] = jnp.zeros((SEG, D), F32)
            return carry

        lax.fori_loop(0, (lr - used) // SEG, clear, 0)
        _segment_copies(tab_ref, tile, buf.at[slot_], ys_ref, sem.at[slot_], to_global=False, wait=False)

    pl.when(i == 0)(lambda: fetch(i, slot))
    pl.when(i + 1 < pl.num_programs(0))(lambda: fetch(i + 1, 1 - slot))
    _segment_copies(tab_ref, blk0 + i, buf.at[slot], ys_ref, sem.at[slot], to_global=False, wait=True)

    yb = buf[slot].astype(BF16)
    lp = lpos_ref[...]
    cio = _iota2((tm, lr), 1).astype(F32)
    g = gate_ref[...]
    moe = None
    for s in range(2):
        sel = jnp.where(cio == lp[:, s:s + 1], 1.0, 0.0).astype(BF16)
        picked = _dot(sel, yb)
        moe = g[:, s:s + 1] * picked if moe is None else moe + g[:, s:s + 1] * picked
    o_ref[...] = _rms(x_ref[...] + moe, nf_ref[...])


def _combine_call(tab, x, gate, lpos, nf, ys, *, tm, tok0):
    n = x.shape[0]
    blk0 = tok0 // tm
    lr = _local_rows(tm)
    return pl.pallas_call(
        functools.partial(_combine_kernel, tm=tm, blk0=blk0, lr=lr),
        grid_spec=pltpu.PrefetchScalarGridSpec(
            num_scalar_prefetch=1,
            grid=(n // tm,),
            in_specs=[
                pl.BlockSpec((tm, D), lambda i, *_: (i, 0)),
                pl.BlockSpec((tm, 2), lambda i, *_: (blk0 + i, 0)),
                pl.BlockSpec((tm, 2), lambda i, *_: (blk0 + i, 0)),
                pl.BlockSpec((1, D), lambda i, *_: (0, 0)),
                pl.BlockSpec(memory_space=pl.ANY),
            ],
            out_specs=pl.BlockSpec((tm, D), lambda i, *_: (i, 0)),
            scratch_shapes=[pltpu.VMEM((2, lr, D), F32), pltpu.SemaphoreType.DMA((2,))],
        ),
        out_shape=jax.ShapeDtypeStruct((n, D), F32),
        compiler_params=pltpu.CompilerParams(dimension_semantics=("arbitrary",), vmem_limit_bytes=V7X_VMEM_LIMIT),
        name=f"moe_combine_{n}",
    )(tab, x, gate, lpos, nf, ys)


def kernel(x_prompt, x_sample, state_hgrn, state_gdn, state_gdn_conv, meta_tokens, norm_mix, norm_ffn, norm_final,
           hg_w_in, hg_lower_bounds, hg_g_norm, hg_w_o, gdn_w_in, gdn_conv_w, gdn_a_log, gdn_dt_bias, gdn_g_norm,
           gdn_w_o, ffn_w1, ffn_w3, ffn_w2, moe_w_router, moe_w1, moe_w3, moe_w2):
    bp, tp, _ = x_prompt.shape
    bs, ts, _ = x_sample.shape
    np_, ns_ = bp * tp, bs * ts
    n_tok = np_ + ns_
    row = lambda v: v.reshape(1, -1).astype(F32)

    hg_win = hg_w_in.astype(BF16)
    hg_wo = hg_w_o.astype(BF16)
    gdn_win = jnp.concatenate(
        [gdn_w_in[:, :QKV + D], gdn_w_in[:, QKV + D:], jnp.zeros((D, LANES - 2 * H), gdn_w_in.dtype)],
        axis=1).astype(BF16)
    gdn_wo = gdn_w_o.astype(BF16)
    par = jnp.zeros((2, LANES), F32)
    par = par.at[0, H:2 * H].set(gdn_dt_bias.astype(F32)).at[1, H:2 * H].set(gdn_a_log.astype(F32))
    w1d, w3d, w2d = ffn_w1.astype(BF16), ffn_w3.astype(BF16), ffn_w2.astype(BF16)
    w1e, w3e, w2e = moe_w1.astype(BF16), moe_w3.astype(BF16), moe_w2.astype(BF16)
    wr = jnp.concatenate([moe_w_router.astype(F32), jnp.zeros((D, LANES - NE), F32)], axis=1)
    nm0, nm1 = row(norm_mix[0]), row(norm_mix[1])
    nf0, nf1 = row(norm_ffn[0]), row(norm_ffn[1])
    nfin = row(norm_final)
    hg_gn, gdn_gn = row(hg_g_norm), row(gdn_g_norm)
    lbnd = hg_lower_bounds.astype(F32)
    cw = gdn_conv_w.astype(F32)

    hg_args = (nm0, hg_win, lbnd, hg_gn, hg_wo)
    gdn_args = (nm1, gdn_win, cw, par, gdn_gn, gdn_wo)
    xm = meta_tokens.astype(F32)
    xp2 = x_prompt.reshape(np_, D)
    xs2 = x_sample.reshape(ns_, D)

    zero_state = jnp.zeros((1, H, HD, HD), F32)
    h1m, hg_m = _hgrn_call(xm, zero_state, *hg_args, nseq=1, seqlen=N_META, bb=1, tt=N_META, c=N_META)
    h1p, hg_p = _hgrn_call(xp2, hg_m, *hg_args, nseq=bp, seqlen=tp, bb=1, tt=256, c=64, nu=4, bcast_state=True)
    h1s, hg_s = _hgrn_call(xs2, state_hgrn.astype(F32), *hg_args, nseq=bs, seqlen=ts, bb=8, tt=ts, c=64, nu=1)
    h2m = _ffn_call([h1m], nf0, w1d, w3d, w2d, tm=N_META)
    h2 = _ffn_call([h1p, h1s], nf0, w1d, w3d, w2d, tm=512)

    zero_conv = jnp.zeros((1, CONV - 1, QKV), F32)
    _, gdn_m, conv_m = _gdn_call(h2m, zero_state, zero_conv, *gdn_args, nseq=1, seqlen=N_META, bb=1, tt=N_META, c=16)
    h3p, gdn_p, conv_p = _gdn_call(h2, gdn_m, conv_m, *gdn_args, nseq=bp, seqlen=tp, bb=1, tt=256, c=128, nu=2,
                                   bcast_state=True)
    h3s, gdn_s, conv_s = _gdn_call(h2, state_gdn.astype(F32), state_gdn_conv.astype(F32), *gdn_args,
                                   nseq=bs, seqlen=ts, bb=8, tt=ts, c=64, nu=1, row0=np_)

    yp, ysm = _moe_final(h3p, h3s, nf1, wr, w1e, w3e, w2e, nfin, tm=256)

    sd, gd, cd = state_hgrn.dtype, state_gdn.dtype, state_gdn_conv.dtype
    return (yp.reshape(bp, tp, D), ysm.reshape(bs, ts, D),
            hg_p.astype(sd), hg_s.astype(sd),
            gdn_p.astype(gd), gdn_s.astype(gd), conv_p.astype(cd), conv_s.astype(cd))


def _moe_final(h3p, h3s, nf1, wr, w1e, w3e, w2e, nfin, *, tm):
    np_ = h3p.shape[0]
    n_tok = np_ + h3s.shape[0]
    nblk = n_tok // tm
    gate, lpos, lpost, tab, tot = _router_call(h3p, h3s, nf1, wr, tm=tm)
    totals = tot[0, :NE].astype(jnp.int32)
    padded = ((totals + TM - 1) // TM) * TM
    ends = jnp.cumsum(padded)
    offs = ends - padded
    tab3 = tab.reshape(nblk, 8, LANES)[:, :3, :NE].astype(jnp.int32)
    seg_tab = jnp.stack([tab3[:, 0] // SEG, tab3[:, 1], tab3[:, 2] + offs[None, :]], axis=-1).reshape(-1)
    ntile = -(-(2 * n_tok + nblk * NE * (SEG - 1) + NE * (TM - 1)) // TM)
    nact = ends[-1] // TM
    tiles = jnp.arange(ntile, dtype=jnp.int32)
    texp_all = jnp.minimum(jnp.sum(tiles[:, None] * TM >= ends[None, :], axis=1), NE - 1).astype(jnp.int32)
    last = jnp.maximum(nact - 1, 0)
    active = tiles < nact
    texp = jnp.where(active, texp_all, texp_all[last]).astype(jnp.int32)
    tsrc = jnp.where(active, tiles, last).astype(jnp.int32)
    tvalid = jnp.where(active, jnp.clip(totals[texp_all] - (tiles * TM - offs[texp_all]), 0, TM), 0).astype(jnp.int32)
    tail = nact + jnp.arange(ntile - 2 * n_tok // TM)
    zst = jnp.concatenate([jnp.where(padded > 0, ends - TM, -1),
                           jnp.where(tail < ntile, tail * TM, -1)]).astype(jnp.int32)
    xs_sorted = _dispatch_call(seg_tab, zst, h3p, h3s, lpost, nf1, tm=tm, rmax=ntile * TM)
    ys = _expert_call(texp, tsrc, tvalid, xs_sorted, w1e, w3e, w2e)
    yp = _combine_call(seg_tab, h3p, gate, lpos, nfin, ys, tm=tm, tok0=0)
    ysm = _combine_call(seg_tab, h3s, gate, lpos, nfin, ys, tm=tm, tok0=np_)
    return yp, ysm
```

```python
import functools

import jax
import jax.numpy as jnp
from jax import lax
from jax.experimental import pallas as pl
from jax.experimental.pallas import tpu as pltpu

D = 1024
H = 8
HD = 128
QKV = 3 * H * HD
CONV = 4
DFF = 3584
NE = 8
N_META = 16
EPS = 1e-6
LANES = 128
FB = 512
TM = 512
HG_SUB = 16
SEG = 8
V7X_VMEM_LIMIT = 56 * 1024 * 1024

F32 = jnp.float32
BF16 = jnp.bfloat16


def _dot(a, b):
    return jnp.dot(a, b, preferred_element_type=F32)


def _dot_nt(a, b):
    return lax.dot_general(a, b, (((1,), (1,)), ((), ())), preferred_element_type=F32)


def _dot_tn(a, b):
    return lax.dot_general(a, b, (((0,), (0,)), ((), ())), preferred_element_type=F32)


def _split3(x):
    x1 = x.astype(BF16)
    r1 = x - x1.astype(F32)
    x2 = r1.astype(BF16)
    x3 = (r1 - x2.astype(F32)).astype(BF16)
    return x1, x2, x3


def _sel_dot(sel, x):
    x1, x2, x3 = _split3(x)
    return _dot(sel, x1) + _dot(sel, x2) + _dot(sel, x3)


def _dot_3x(a, b):
    a_hi = a.astype(BF16)
    a_lo = (a - a_hi.astype(F32)).astype(BF16)
    b_hi = b.astype(BF16)
    b_lo = (b - b_hi.astype(F32)).astype(BF16)
    return _dot(a_hi, b_hi) + (_dot(a_hi, b_lo) + _dot(a_lo, b_hi))


def _sel_dot_nt(sel, x):
    x1, x2, x3 = _split3(x)
    return _dot_nt(sel, x1) + _dot_nt(sel, x2) + _dot_nt(sel, x3)


def _rms(x, g):
    return x * lax.rsqrt(jnp.mean(x * x, axis=-1, keepdims=True) + EPS) * g


def _silu(x):
    return x * jax.nn.sigmoid(x)


def _iota2(shape, dim):
    return lax.broadcasted_iota(jnp.int32, shape, dim)


def _head_rms_gate(o, g, gate):
    parts = []
    for h in range(H):
        oh = o[:, h * HD:(h + 1) * HD]
        parts.append(_rms(oh, g))
    return jnp.concatenate(parts, axis=1) * gate


def _chunk_row(idx, j, nu, c):
    if isinstance(idx, int):
        return (idx * nu + j) * c
    return pl.multiple_of((idx * nu + j) * c, c)


def _chunk_seq(idx, j, nu, ncs):
    return (idx * nu + j) // ncs


def _chunk_ops(tri, c):
    assert c % 16 == 0
    tri_b = tri.astype(BF16)
    return (lambda x: _sel_dot(tri_b, x)), (lambda x: x.astype(BF16))


def _run_groups(group, n):
    if n == 1:
        group(0, 0)
    else:
        lax.fori_loop(0, n, group, 0)


def _const_spec(shape):
    nd = len(shape)
    return pl.BlockSpec(shape, lambda *_: (0,) * nd, pipeline_mode=pl.Buffered(1))


def _hgrn_kernel(x_ref, s0_ref, nw_ref, win_ref, lb_ref, gn_ref, wo_ref,
                 o_ref, sout_ref,
                 st_scr, q_scr, k_scr, lf_scr, v_scr, g_scr, *, bb, tt, c, nu):
    t = pl.program_id(1)
    rows = bb * tt
    sl = min(c, tt)
    spc = c // sl
    ncs = max(tt // c, 1)
    assert c % sl == 0 and (ncs == 1 or ncs % nu == 0)

    @pl.when(t == 0)
    def _():
        for b in range(bb):
            for h in range(H):
                st_scr[b, h] = s0_ref[b, h].T

    x = x_ref[...]
    xn = _rms(x, nw_ref[...]).astype(BF16)
    lbn = lb_ref[...]
    e = jnp.exp(lbn - jnp.max(lbn, axis=0, keepdims=True))
    lb = e[0:1] / jnp.sum(e, axis=0, keepdims=True)
    q_scr[...] = _silu(_dot(xn, win_ref[:, 0:D])) * HD ** -0.5
    f = lb + (1.0 - lb) * jax.nn.sigmoid(_dot(xn, win_ref[:, D:2 * D]))
    k_scr[...] = 1.0 - f
    lf_scr[...] = jnp.log(f)
    v_scr[...] = _dot(xn, win_ref[:, 2 * D:3 * D])
    g_scr[...] = _silu(_dot(xn, win_ref[:, 3 * D:4 * D]))

    row = _iota2((c, c), 0)
    col = _iota2((c, c), 1)
    tri = (row >= col) & ((row // sl) == (col // sl))
    cumsum, mxu = _chunk_ops(tri, c)
    heads = [slice(h * HD, (h + 1) * HD) for h in range(H)]
    nsub = c // HG_SUB if sl > HG_SUB else 1
    if nsub > 1:
        local_tri = (tri & ((row // HG_SUB) == (col // HG_SUB))).astype(BF16)
        slab_row = _iota2((c, D), 0)
    if spc > 1:
        seq_last = (col == (row // sl) * sl + (sl - 1)).astype(BF16)

    def chunk_terms(r0):
        lf = lf_scr[pl.ds(r0, c), :]
        bc = cumsum(lf)
        tot = bc[c - 1:c, :] if spc == 1 else _sel_dot(seq_last, bc)
        qq = q_scr[pl.ds(r0, c), :]
        kk = k_scr[pl.ds(r0, c), :]
        v32 = v_scr[pl.ds(r0, c), :]
        vv = mxu(v32)
        qd32 = qq * jnp.exp(bc)
        kt32 = kk * jnp.exp(tot - bc)
        qdj = mxu(qd32)
        ktail = mxu(kt32)
        if nsub == 1:
            kinv = mxu(kk * jnp.exp(-bc))
            scj = [_dot_nt(qdj[:, s], kinv[:, s]) for s in heads]
        else:
            bl = _sel_dot(local_tri, lf)
            qloc = mxu(qq * jnp.exp(bl))
            parts = [[] for _ in heads]
            for i in range(nsub):
                lo, hi = i * HG_SUB, (i + 1) * HG_SUB
                ref_pt = bc[lo - 1:lo, :] if i else 0.0
                kci = mxu(jnp.where(slab_row < hi, kk * jnp.exp(ref_pt - bc), 0.0))
                for h, s in enumerate(heads):
                    parts[h].append(_dot_nt(qloc[lo:hi, s], kci[:, s]))
            scj = [jnp.concatenate(p, axis=0) for p in parts]
        scj = [mxu(jnp.where(tri, sc, 0.0)) for sc in scj]
        oij = [_dot(scj[h], vv[:, heads[h]]) for h in range(H)]
        if spc > 1:
            return qd32, jnp.exp(tot), (v32, kt32), oij
        kvj = [_dot_tn(vv[:, s], ktail[:, s]) for s in heads]
        return qdj, jnp.exp(tot), kvj, oij

    def group(idx, carry):
        qd, dec, kv, oi, r0s, sqs = [], [], [], [], [], []
        for j in range(nu):
            r0 = _chunk_row(idx, j, nu, c)
            qdj, decj, kvj, oij = chunk_terms(r0)
            kv.append(kvj)
            oi.append(oij)
            qd.append(qdj)
            dec.append(decj)
            r0s.append(r0)
            sqs.append(_chunk_seq(idx, j, nu, ncs) if spc == 1 else (idx * nu + j) * spc)

        if spc > 1:
            own_block = (_iota2((c, spc * HD), 1) // HD) == (_iota2((c, spc * HD), 0) // sl)
            outs, finals = [], []
            for j in range(nu):
                v32, kt32 = kv[j]
                o_j = []
                for h, s in enumerate(heads):
                    st_all = jnp.concatenate([st_scr[sqs[j] + q, h] for q in range(spc)], axis=0)
                    own = jnp.where(own_block, _dot_nt(mxu(qd[j][:, s]), mxu(st_all)), 0.0)
                    inter = own[:, 0:HD]
                    for q in range(1, spc):
                        inter = inter + own[:, q * HD:(q + 1) * HD]
                    o_j.append(oi[j][h] + inter)
                    decay = jnp.concatenate(
                        [jnp.broadcast_to(dec[j][q * sl:q * sl + 1, s], (HD, HD)) for q in range(spc)], axis=0)
                    v_spread = mxu(jnp.where(own_block, jnp.concatenate([v32[:, s]] * spc, axis=1), 0.0))
                    st_new = st_all * decay + _dot_tn(v_spread, mxu(kt32[:, s]))
                    finals.extend((sqs[j] + q, h, st_new[q * HD:(q + 1) * HD]) for q in range(spc))
                outs.append(o_j)
            for j in range(nu):
                q_scr[pl.ds(r0s[j], c), :] = jnp.concatenate(outs[j], axis=1)
            for sq, h, st_new in finals:
                st_scr[sq, h] = st_new
            return carry

        own_state = ncs == 1
        loaded = [[st_scr[sqs[j], h] for h in range(H)] for j in range(nu if own_state else 1)]
        outs, finals = [], []
        st = loaded[0]
        for j in range(nu):
            if own_state:
                st = loaded[j]
            outs.append([oi[j][h] + _dot_nt(qd[j][:, heads[h]], mxu(st[h])) for h in range(H)])
            st = [st[h] * dec[j][:, heads[h]] + kv[j][h] for h in range(H)]
            if own_state or j == nu - 1:
                finals.append((sqs[j], st))
        for j in range(nu):
            q_scr[pl.ds(r0s[j], c), :] = jnp.concatenate(outs[j], axis=1)
        for sq, st in finals:
            for h in range(H):
                st_scr[sq, h] = st[h]
        return carry

    _run_groups(group, rows // (nu * c))

    on = _head_rms_gate(q_scr[...], gn_ref[...], g_scr[...])
    o_ref[...] = x + _dot(on.astype(BF16), wo_ref[...])

    @pl.when(t == pl.num_programs(1) - 1)
    def _():
        for b in range(bb):
            for h in range(H):
                sout_ref[b, h] = st_scr[b, h].T


def _hgrn_call(x2d, s0, nw, win, lbnd, gn, wo, *, nseq, seqlen, bb, tt, c, nu=4, row0=0, bcast_state=False):
    assert bb == 1 or tt == seqlen
    assert nseq % bb == 0 and seqlen % tt == 0 and row0 % (bb * tt) == 0
    assert tt % c == 0 or (c % tt == 0 and (bb * tt) % c == 0)
    rows = bb * tt
    nu = min(nu, rows // c)
    nt = seqlen // tt
    blk0 = row0 // rows
    s_idx = (lambda b, t: (0, 0, 0, 0)) if bcast_state else (lambda b, t: (b, 0, 0, 0))
    slab = pltpu.VMEM((rows, D), F32)
    return pl.pallas_call(
        functools.partial(_hgrn_kernel, bb=bb, tt=tt, c=c, nu=nu),
        grid=(nseq // bb, nt),
        in_specs=[
            pl.BlockSpec((rows, D), lambda b, t: (blk0 + b * nt + t, 0)),
            pl.BlockSpec((bb, H, HD, HD), s_idx),
            _const_spec((1, D)),
            _const_spec((D, 4 * D)),
            _const_spec(lbnd.shape),
            _const_spec((1, HD)),
            _const_spec((D, D)),
        ],
        out_specs=[
            pl.BlockSpec((rows, D), lambda b, t: (b * nt + t, 0)),
            pl.BlockSpec((bb, H, HD, HD), lambda b, t: (b, 0, 0, 0)),
        ],
        out_shape=[
            jax.ShapeDtypeStruct((nseq * seqlen, D), F32),
            jax.ShapeDtypeStruct((nseq, H, HD, HD), F32),
        ],
        scratch_shapes=[pltpu.VMEM((bb, H, HD, HD), F32), slab, slab, slab, slab, slab],
        compiler_params=pltpu.CompilerParams(
            dimension_semantics=("arbitrary", "arbitrary"), vmem_limit_bytes=V7X_VMEM_LIMIT),
        name=f"hgrn_mixer_{nseq}x{seqlen}",
    )(x2d, s0, nw, win, lbnd, gn, wo)


def _neumann_inv(ls, eye, levels, mxu):
    xs = [eye - l for l in ls]
    ps = [mxu(l) for l in ls]
    for _ in range(levels - 1):
        ps = [mxu(_dot(p, p)) for p in ps]
        xs = [x + _dot(mxu(x), p) for x, p in zip(xs, ps)]
    return xs


def _unit_lower_inv(ls, c, mxu, span=None):
    span = c if span is None else span
    row = _iota2((c, c), 0)
    col = _iota2((c, c), 1)
    eye = (row == col).astype(F32)
    if span <= 16:
        return _neumann_inv(ls, eye, span.bit_length() - 1, mxu)
    sub = 16
    nblk = c // sub
    assert c == nblk * sub and nblk & (nblk - 1) == 0
    diag = (row // sub) == (col // sub)
    lds = [jnp.where(diag, l, 0.0) for l in ls]
    dinvs = [mxu(d) for d in _neumann_inv(lds, eye, 4, mxu)]
    ms = [_dot(d, mxu(l - ld)) for d, l, ld in zip(dinvs, ls, lds)]
    ys = _neumann_inv(ms, eye, nblk.bit_length() - 1, mxu)
    return [_dot(mxu(y), d) for y, d in zip(ys, dinvs)]


def _gdn_kernel(x_ref, s0_ref, c0_ref, nw_ref, win_ref, cw_ref, par_ref, gn_ref, wo_ref,
                o_ref, sout_ref, cout_ref,
                s_scr, xp_scr, act_scr, z_scr, la_scr, be_scr, o_scr, *, bb, tt, c, nu):
    t = pl.program_id(1)
    rows = bb * tt
    sl = min(c, tt)
    spc = c // sl
    ncs = max(tt // c, 1)
    assert c % sl == 0 and (ncs == 1 or ncs % nu == 0)

    @pl.when(t == 0)
    def _():
        s_scr[...] = s0_ref[...]
        for b in range(bb):
            xp_scr[b, 8 - (CONV - 1):8, :] = c0_ref[b]

    x = x_ref[...]
    xn = _rms(x, nw_ref[...]).astype(BF16)
    pre = _dot(xn, win_ref[:, 0:QKV])
    for b in range(bb):
        xp_scr[b, 8:8 + tt, :] = pre[b * tt:(b + 1) * tt]
    z_scr[...] = _silu(_dot(xn, win_ref[:, QKV:QKV + D]))
    ba = _dot(xn, win_ref[:, QKV + D:QKV + D + LANES])
    par = par_ref[...]
    sp = ba + par[0:1]
    softplus = jnp.maximum(sp, 0.0) + jnp.log(1.0 + jnp.exp(-jnp.abs(sp)))
    la_scr[...] = -jnp.exp(par[1:2]) * softplus
    be_scr[...] = jax.nn.sigmoid(ba)

    cw = cw_ref[...]
    for b in range(bb):
        conv = xp_scr[b, 8 - (CONV - 1):8 - (CONV - 1) + tt, :] * cw[0:1]
        for w in range(1, CONV):
            conv = conv + xp_scr[b, 8 - (CONV - 1) + w:8 - (CONV - 1) + w + tt, :] * cw[w:w + 1]
        act = _silu(conv)
        for j in range(2 * H):
            a = act[:, j * HD:(j + 1) * HD]
            scale = HD ** -0.5 if j < H else 1.0
            act_scr[b * tt:(b + 1) * tt, j * HD:(j + 1) * HD] = (
                a * lax.rsqrt(jnp.sum(a * a, axis=-1, keepdims=True) + EPS) * scale)
        act_scr[b * tt:(b + 1) * tt, 2 * H * HD:] = act[:, 2 * H * HD:]
        tail = xp_scr[b, tt:tt + 8, :]
        xp_scr[b, 0:8, :] = tail

    row = _iota2((c, c), 0)
    col = _iota2((c, c), 1)
    same_seq = (row // sl) == (col // sl)
    incl = (row >= col) & same_seq
    strict = (row > col) & same_seq
    cumsum, mxu = _chunk_ops(incl, c)
    eye_l = (_iota2((LANES, LANES), 0) == _iota2((LANES, LANES), 1)).astype(BF16)
    transpose = lambda m: _sel_dot_nt(eye_l, m)
    if spc > 1:
        seq_last = (col == (row // sl) * sl + (sl - 1)).astype(BF16)
    narrow = mxu if spc == 1 else (lambda a: a)

    heads = [slice(h * HD, (h + 1) * HD) for h in range(H)]

    def group(idx, carry):
        qd, ktail, dtot, lmat, amat, rwu, r0s, sqs = [], [], [], [], [], [], [], []
        for j in range(nu):
            r0 = _chunk_row(idx, j, nu, c)
            gc = cumsum(la_scr[pl.ds(r0, c), :])
            gt = transpose(gc)
            be = be_scr[pl.ds(r0, c), :]
            eg = jnp.exp(gc)
            gl = gc[c - 1:c, :] if spc == 1 else _sel_dot(seq_last, gc)
            et = jnp.exp(gl - gc)
            dt = jnp.exp(gl)
            qq = act_scr[pl.ds(r0, c), 0:H * HD]
            kk = act_scr[pl.ds(r0, c), H * HD:2 * H * HD]
            vv = act_scr[pl.ds(r0, c), 2 * H * HD:]
            qd_j, ktail_j, dtot_j = [], [], []
            for h, s in enumerate(heads):
                g_col = gc[:, 8 + h:9 + h]
                eg_col = eg[:, 8 + h:9 + h]
                be_col = be[:, h:h + 1]
                kh = mxu(kk[:, s])
                qk = _dot_nt(jnp.concatenate([mxu(qq[:, s]), kh], axis=0), kh)
                ed = jnp.exp(g_col - gt[8 + h:9 + h, :])
                lmat.append(be_col * qk[c:] * jnp.where(strict, ed, 0.0))
                amat.append(mxu(qk[:c] * jnp.where(incl, ed, 0.0)))
                rwu.append(mxu(jnp.concatenate([kk[:, s] * (be_col * eg_col), vv[:, s] * be_col], axis=1)))
                qd_j.append(narrow(qq[:, s] * eg_col))
                ktail_j.append(narrow(kk[:, s] * et[:, 8 + h:9 + h]))
                dtot_j.append(dt[:, 8 + h:9 + h])
            qd.append(qd_j)
            ktail.append(ktail_j)
            dtot.append(dtot_j)
            r0s.append(r0)
            sqs.append(_chunk_seq(idx, j, nu, ncs) if spc == 1 else (idx * nu + j) * spc)
        tinv = _unit_lower_inv(lmat, c, mxu, span=sl)
        wu = [_dot(mxu(t_), r_) for t_, r_ in zip(tinv, rwu)]

        if spc > 1:
            own_block = (_iota2((c, spc * HD), 1) // HD) == (_iota2((c, spc * HD), 0) // sl)
            spread = lambda a: mxu(jnp.where(own_block, jnp.concatenate([a] * spc, axis=1), 0.0))
            outs, finals = [], []
            for j in range(nu):
                o_j = []
                for h in range(H):
                    wuh = wu[j * H + h]
                    s_all = jnp.concatenate([s_scr[sqs[j] + q, h] for q in range(spc)], axis=0)
                    qw = _dot(jnp.concatenate([spread(qd[j][h]), spread(wuh[:, :HD])], axis=0), mxu(s_all))
                    u = wuh[:, HD:] - qw[c:]
                    ub = mxu(u)
                    o_j.append(qw[:c] + _dot(amat[j * H + h], ub))
                    decay = jnp.concatenate(
                        [jnp.broadcast_to(dtot[j][h][q * sl:q * sl + 1], (HD, HD)) for q in range(spc)], axis=0)
                    s_new = s_all * decay + _dot_tn(spread(ktail[j][h]), ub)
                    finals.extend((sqs[j] + q, h, s_new[q * HD:(q + 1) * HD]) for q in range(spc))
                outs.append(o_j)
            for j in range(nu):
                o_scr[pl.ds(r0s[j], c), :] = jnp.concatenate(outs[j], axis=1)
            for sq, h, s_new in finals:
                s_scr[sq, h] = s_new
            return carry

        own_state = ncs == 1
        loaded = [[s_scr[sqs[j], h] for h in range(H)] for j in range(nu if own_state else 1)]
        outs, finals = [], []
        st = loaded[0]
        for j in range(nu):
            if own_state:
                st = loaded[j]
            o_j, st_next = [], []
            for h in range(H):
                wuh = wu[j * H + h]
                qw = _dot(jnp.concatenate([qd[j][h], mxu(wuh[:, :HD])], axis=0), mxu(st[h]))
                u = wuh[:, HD:] - qw[c:]
                ub = mxu(u)
                o_j.append(qw[:c] + _dot(amat[j * H + h], ub))
                st_next.append(st[h] * dtot[j][h] + _dot_tn(ktail[j][h], ub))
            st = st_next
            outs.append(o_j)
            if own_state or j == nu - 1:
                finals.append((sqs[j], st))
        for j in range(nu):
            o_scr[pl.ds(r0s[j], c), :] = jnp.concatenate(outs[j], axis=1)
        for sq, st in finals:
            for h in range(H):
                s_scr[sq, h] = st[h]
        return carry

    _run_groups(group, rows // (nu * c))

    on = _head_rms_gate(o_scr[...], gn_ref[...], z_scr[...])
    o_ref[...] = x + _dot(on.astype(BF16), wo_ref[...])

    @pl.when(t == pl.num_programs(1) - 1)
    def _():
        sout_ref[...] = s_scr[...]
        for b in range(bb):
            cout_ref[b] = xp_scr[b, 8 - (CONV - 1):8, :]


def _gdn_call(x2d, s0, c0, nw, win, cw, par, gn, wo, *, nseq, seqlen, bb, tt, c, nu=2, row0=0,
              bcast_state=False):
    assert bb == 1 or tt == seqlen
    assert nseq % bb == 0 and seqlen % tt == 0 and tt % 8 == 0 and row0 % (bb * tt) == 0
    assert tt % c == 0 or (c % tt == 0 and (bb * tt) % c == 0)
    rows = bb * tt
    nt = seqlen // tt
    blk0 = row0 // rows
    s_idx = (lambda b, t: (0, 0, 0, 0)) if bcast_state else (lambda b, t: (b, 0, 0, 0))
    c_idx = (lambda b, t: (0, 0, 0)) if bcast_state else (lambda b, t: (b, 0, 0))
    slab = pltpu.VMEM((rows, D), F32)
    return pl.pallas_call(
        functools.partial(_gdn_kernel, bb=bb, tt=tt, c=c, nu=min(nu, bb * tt // c)),
        grid=(nseq // bb, nt),
        in_specs=[
            pl.BlockSpec((rows, D), lambda b, t: (blk0 + b * nt + t, 0)),
            pl.BlockSpec((bb, H, HD, HD), s_idx),
            pl.BlockSpec((bb, CONV - 1, QKV), c_idx),
            _const_spec((1, D)),
            _const_spec(win.shape),
            _const_spec((CONV, QKV)),
            _const_spec((2, LANES)),
            _const_spec((1, HD)),
            _const_spec((D, D)),
        ],
        out_specs=[
            pl.BlockSpec((rows, D), lambda b, t: (b * nt + t, 0)),
            pl.BlockSpec((bb, H, HD, HD), lambda b, t: (b, 0, 0, 0)),
            pl.BlockSpec((bb, CONV - 1, QKV), lambda b, t: (b, 0, 0)),
        ],
        out_shape=[
            jax.ShapeDtypeStruct((nseq * seqlen, D), F32),
            jax.ShapeDtypeStruct((nseq, H, HD, HD), F32),
            jax.ShapeDtypeStruct((nseq, CONV - 1, QKV), F32),
        ],
        scratch_shapes=[
            pltpu.VMEM((bb, H, HD, HD), F32),
            pltpu.VMEM((bb, tt + 8, QKV), F32),
            pltpu.VMEM((rows, QKV), F32),
            slab,
            pltpu.VMEM((rows, LANES), F32),
            pltpu.VMEM((rows, LANES), F32),
            slab,
        ],
        compiler_params=pltpu.CompilerParams(
            dimension_semantics=("arbitrary", "arbitrary"), vmem_limit_bytes=V7X_VMEM_LIMIT),
        name=f"gdn_mixer_{nseq}x{seqlen}",
    )(x2d, s0, c0, nw, win, cw, par, gn, wo)


def _swiglu(xb, w1_ref, w3_ref, w2_ref):
    acc = None
    for j in range(DFF // FB):
        fs = slice(j * FB, (j + 1) * FB)
        hh = (_silu(_dot(xb, w1_ref[:, fs])) * _dot(xb, w3_ref[:, fs])).astype(BF16)
        part = _dot(hh, w2_ref[fs, :])
        acc = part if acc is None else acc + part
    return acc


def _ffn_kernel(*refs, n_in, nblk_first):
    x_refs = refs[:n_in]
    nw_ref, w1_ref, w3_ref, w2_ref, o_ref = refs[n_in:]
    x = x_refs[0][...]
    if n_in == 2:
        x = jnp.where(pl.program_id(0) < nblk_first, x, x_refs[1][...])
    xb = _rms(x, nw_ref[...]).astype(BF16)
    o_ref[...] = x + _swiglu(xb, w1_ref, w3_ref, w2_ref)


def _two_group_specs(tm, nblk_first, nblk_total):
    last_first = nblk_first - 1
    return [
        pl.BlockSpec((tm, D), lambda i, *_: (jnp.minimum(i, last_first), 0)),
        pl.BlockSpec((tm, D), lambda i, *_: (jnp.maximum(i - nblk_first, 0), 0)),
    ]


def _ffn_call(xs, nw, w1, w3, w2, *, tm):
    nblks = [x.shape[0] // tm for x in xs]
    assert all(x.shape[0] % tm == 0 for x in xs)
    total = sum(nblks)
    if len(xs) == 2:
        x_specs = _two_group_specs(tm, nblks[0], total)
    else:
        x_specs = [pl.BlockSpec((tm, D), lambda i: (i, 0))]
    return pl.pallas_call(
        functools.partial(_ffn_kernel, n_in=len(xs), nblk_first=nblks[0]),
        grid=(total,),
        in_specs=x_specs + [_const_spec((1, D)), _const_spec((D, DFF)), _const_spec((D, DFF)), _const_spec((DFF, D))],
        out_specs=pl.BlockSpec((tm, D), lambda i: (i, 0)),
        out_shape=jax.ShapeDtypeStruct((total * tm, D), F32),
        compiler_params=pltpu.CompilerParams(dimension_semantics=("arbitrary",), vmem_limit_bytes=V7X_VMEM_LIMIT),
        name=f"ffn_dense_{total * tm}",
    )(*xs, nw, w1, w3, w2)


def _router_kernel(xa_ref, xb_ref, nw_ref, wr_ref, gate_ref, lpos_ref, lpost_ref, tab_ref, tot_ref, seg_scr, *,
                   tm, nblk_first):
    i = pl.program_id(0)

    @pl.when(i == 0)
    def _():
        seg_scr[...] = jnp.zeros_like(seg_scr)

    x = jnp.where(i < nblk_first, xa_ref[...], xb_ref[...])
    xn = _rms(x, nw_ref[...])
    logits = _dot_3x(xn, wr_ref[...])
    lane = _iota2((tm, LANES), 1).astype(F32)
    neg = jnp.float32(-jnp.inf)
    l1 = jnp.where(lane < NE, logits, neg)
    m1 = jnp.max(l1, axis=-1, keepdims=True)
    i1 = jnp.min(jnp.where(l1 == m1, lane, float(LANES)), axis=-1, keepdims=True)
    l2 = jnp.where(lane == i1, neg, l1)
    m2 = jnp.max(l2, axis=-1, keepdims=True)
    i2 = jnp.min(jnp.where(l2 == m2, lane, float(LANES)), axis=-1, keepdims=True)
    e2 = jnp.exp(m2 - m1)
    g1 = 1.0 / (1.0 + e2)
    g2 = e2 / (1.0 + e2)

    oh1 = (lane == i1).astype(F32)
    oh2 = (lane == i2).astype(F32)
    below = (_iota2((tm, tm), 0) > _iota2((tm, tm), 1)).astype(BF16)
    c1 = _dot(below, oh1.astype(BF16))
    c2 = _dot(below, oh2.astype(BF16))
    n1 = jnp.sum(oh1, axis=0, keepdims=True)
    n2 = jnp.sum(oh2, axis=0, keepdims=True)
    cnt = jnp.floor((n1 + n2 + (SEG - 1)) * (1.0 / SEG)) * SEG
    before = (_iota2((LANES, LANES), 0) < _iota2((LANES, LANES), 1)).astype(BF16)
    start = _dot(jnp.broadcast_to(cnt, (16, LANES)).astype(BF16), before)[0:1]
    p1 = jnp.sum(oh1 * (start + c1), axis=-1, keepdims=True)
    p2 = jnp.sum(oh2 * (start + n1 + c2), axis=-1, keepdims=True)
    used = seg_scr[...]
    seg_scr[...] = used + cnt
    tot_ref[...] = used + cnt

    gate_ref[:, 0:1] = g1
    gate_ref[:, 1:2] = g2
    lpos_ref[:, 0:1] = p1
    lpos_ref[:, 1:2] = p2
    sub = _iota2((8, LANES), 0)
    tab_ref[...] = jnp.where(sub == 0, cnt, jnp.where(sub == 1, start, jnp.where(sub == 2, used, 0.0)))
    slab = jnp.where(lane == 0.0, p1, jnp.where(lane == 1.0, p2, 0.0))
    eye16 = (_iota2((16, LANES), 0) == _iota2((16, LANES), 1)).astype(BF16)
    lpost_ref[...] = _sel_dot_nt(eye16, slab)[0:8]


def _router_call(xa, xb, nw, wr, *, tm):
    na, nb = xa.shape[0] // tm, xb.shape[0] // tm
    nblk = na + nb
    n = nblk * tm
    return pl.pallas_call(
        functools.partial(_router_kernel, tm=tm, nblk_first=na),
        grid=(nblk,),
        in_specs=_two_group_specs(tm, na, nblk) + [_const_spec((1, D)), _const_spec((D, LANES))],
        out_specs=[
            pl.BlockSpec((tm, 2), lambda i: (i, 0)),
            pl.BlockSpec((tm, 2), lambda i: (i, 0)),
            pl.BlockSpec((8, tm), lambda i: (i, 0)),
            pl.BlockSpec((8, LANES), lambda i: (i, 0)),
            pl.BlockSpec((1, LANES), lambda i: (0, 0)),
        ],
        out_shape=[
            jax.ShapeDtypeStruct((n, 2), F32),
            jax.ShapeDtypeStruct((n, 2), F32),
            jax.ShapeDtypeStruct((nblk * 8, tm), F32),
            jax.ShapeDtypeStruct((nblk * 8, LANES), F32),
            jax.ShapeDtypeStruct((1, LANES), F32),
        ],
        scratch_shapes=[pltpu.VMEM((1, LANES), F32)],
        compiler_params=pltpu.CompilerParams(dimension_semantics=("arbitrary",), vmem_limit_bytes=V7X_VMEM_LIMIT),
        name="moe_router",
    )(xa, xb, nw, wr)


def _local_rows(tm):
    return -(-(2 * tm + NE * (SEG - 1)) // LANES) * LANES


def _segment_copies(tab_ref, tile, local_ref, global_ref, sem, *, to_global, wait):
    for e in range(NE):
        base = (tile * NE + e) * 3
        npiece, lstart, gstart = tab_ref[base], tab_ref[base + 1], tab_ref[base + 2]

        def piece(k, carry, lstart=lstart, gstart=gstart, e=e):
            loc = local_ref.at[pl.ds(pl.multiple_of(lstart + k * SEG, SEG), SEG)]
            glo = global_ref.at[pl.ds(pl.multiple_of(gstart + k * SEG, SEG), SEG)]
            cp = pltpu.make_async_copy(loc, glo, sem) if to_global else pltpu.make_async_copy(glo, loc, sem)
            if wait:
                cp.wait()
            else:
                cp.start(priority=e % 2)
            return carry

        lax.fori_loop(0, npiece, piece, 0)


def _dispatch_kernel(tab_ref, zst_ref, xa_ref, xb_ref, lpost_ref, nw_ref, xs_ref, buf, zero_scr, sem, zsem, *,
                     tm, nblk_first, lr):
    i = pl.program_id(0)
    slot = lax.rem(i, 2)

    @pl.when(i == 0)
    def _():
        zero_scr[...] = jnp.zeros_like(zero_scr)

        def zero_copy(e):
            start = pl.multiple_of(zst_ref[e], TM)
            return pltpu.make_async_copy(zero_scr, xs_ref.at[pl.ds(start, TM)], zsem)

        for e in range(zst_ref.shape[0]):
            pl.when(zst_ref[e] >= 0)(lambda e=e: zero_copy(e).start())
        for e in range(zst_ref.shape[0]):
            pl.when(zst_ref[e] >= 0)(lambda e=e: zero_copy(e).wait())

    x = jnp.where(i < nblk_first, xa_ref[...], xb_ref[...])
    xn = _rms(x, nw_ref[...]).astype(BF16)
    lpt = lpost_ref[...]
    rio = _iota2((lr, tm), 0).astype(F32)
    sel = jnp.where(rio == lpt[0:1, :], 1.0, jnp.where(rio == lpt[1:2, :], 1.0, 0.0)).astype(BF16)
    buf[slot] = _dot(sel, xn)

    copies = functools.partial(_segment_copies, tab_ref, global_ref=xs_ref, to_global=True)
    copies(i, local_ref=buf.at[slot], sem=sem.at[slot], wait=False)
    pl.when(i > 0)(lambda: copies(i - 1, local_ref=buf.at[1 - slot], sem=sem.at[1 - slot], wait=True))
    pl.when(i == pl.num_programs(0) - 1)(
        lambda: copies(i, local_ref=buf.at[slot], sem=sem.at[slot], wait=True))


def _dispatch_call(tab, zst, xa, xb, lpost, nw, *, tm, rmax):
    na, nb = xa.shape[0] // tm, xb.shape[0] // tm
    lr = _local_rows(tm)
    return pl.pallas_call(
        functools.partial(_dispatch_kernel, tm=tm, nblk_first=na, lr=lr),
        grid_spec=pltpu.PrefetchScalarGridSpec(
            num_scalar_prefetch=2,
            grid=(na + nb,),
            in_specs=_two_group_specs(tm, na, na + nb) + [pl.BlockSpec((8, tm), lambda i, *_: (i, 0)),
                                                          pl.BlockSpec((1, D), lambda i, *_: (0, 0))],
            out_specs=pl.BlockSpec(memory_space=pl.ANY),
            scratch_shapes=[pltpu.VMEM((2, lr, D), F32), pltpu.VMEM((TM, D), F32),
                            pltpu.SemaphoreType.DMA((2,)), pltpu.SemaphoreType.DMA],
        ),
        out_shape=jax.ShapeDtypeStruct((rmax, D), F32),
        compiler_params=pltpu.CompilerParams(dimension_semantics=("arbitrary",), vmem_limit_bytes=V7X_VMEM_LIMIT),
        name="moe_dispatch",
    )(tab, zst, xa, xb, lpost, nw)


def _expert_kernel(texp_ref, tsrc_ref, tvalid_ref, xs_ref, w1_ref, w3_ref, w2_ref, ys_ref):
    valid = tvalid_ref[pl.program_id(0)]
    half = TM // 2

    @pl.when(valid > half)
    def _():
        xb = xs_ref[...].astype(BF16)
        ys_ref[...] = _swiglu(xb, w1_ref, w3_ref, w2_ref)

    @pl.when((valid > 0) & (valid <= half))
    def _():
        xb = xs_ref[0:half, :].astype(BF16)
        ys_ref[0:half, :] = _swiglu(xb, w1_ref, w3_ref, w2_ref)
        ys_ref[half:, :] = jnp.zeros((TM - half, D), F32)

    @pl.when(valid == 0)
    def _():
        ys_ref[...] = jnp.zeros_like(ys_ref)


def _expert_call(texp, tsrc, tvalid, xs, w1, w3, w2):
    ntile = xs.shape[0] // TM
    return pl.pallas_call(
        _expert_kernel,
        grid_spec=pltpu.PrefetchScalarGridSpec(
            num_scalar_prefetch=3,
            grid=(ntile,),
            in_specs=[
                pl.BlockSpec((TM, D), lambda i, te, ts, na: (ts[i], 0)),
                pl.BlockSpec((None, D, DFF), lambda i, te, ts, na: (te[i], 0, 0)),
                pl.BlockSpec((None, D, DFF), lambda i, te, ts, na: (te[i], 0, 0)),
                pl.BlockSpec((None, DFF, D), lambda i, te, ts, na: (te[i], 0, 0)),
            ],
            out_specs=pl.BlockSpec((TM, D), lambda i, te, ts, na: (i, 0)),
        ),
        out_shape=jax.ShapeDtypeStruct((xs.shape[0], D), F32),
        compiler_params=pltpu.CompilerParams(dimension_semantics=("arbitrary",), vmem_limit_bytes=V7X_VMEM_LIMIT),
        name="moe_experts",
    )(texp, tsrc, tvalid, xs, w1, w3, w2)


def _combine_kernel(tab_ref, x_ref, gate_ref, lpos_ref, nf_ref, ys_ref, o_ref, buf, sem, *, tm, blk0, lr):
    i = pl.program_id(0)
    slot = lax.rem(i, 2)

    def fetch(step, slot_):
        tile = blk0 + step
        last = (tile * NE + NE - 1) * 3
        used = pl.multiple_of(tab_ref[last + 1] + tab_ref[last] * SEG, SEG)

        def clear(k, carry):
            buf[slot_, pl.ds(pl.multiple_of(used + k * SEG, SEG), SEG), :] = jnp.zeros((SEG, D), F32)
            return carry

        lax.fori_loop(0, (lr - used) // SEG, clear, 0)
        _segment_copies(tab_ref, tile, buf.at[slot_], ys_ref, sem.at[slot_], to_global=False, wait=False)

    pl.when(i == 0)(lambda: fetch(i, slot))
    pl.when(i + 1 < pl.num_programs(0))(lambda: fetch(i + 1, 1 - slot))
    _segment_copies(tab_ref, blk0 + i, buf.at[slot], ys_ref, sem.at[slot], to_global=False, wait=True)

    yb = buf[slot].astype(BF16)
    lp = lpos_ref[...]
    cio = _iota2((tm, lr), 1).astype(F32)
    g = gate_ref[...]
    moe = None
    for s in range(2):
        sel = jnp.where(cio == lp[:, s:s + 1], 1.0, 0.0).astype(BF16)
        picked = _dot(sel, yb)
        moe = g[:, s:s + 1] * picked if moe is None else moe + g[:, s:s + 1] * picked
    o_ref[...] = _rms(x_ref[...] + moe, nf_ref[...])


def _combine_call(tab, x, gate, lpos, nf, ys, *, tm, tok0):
    n = x.shape[0]
    blk0 = tok0 // tm
    lr = _local_rows(tm)
    return pl.pallas_call(
        functools.partial(_combine_kernel, tm=tm, blk0=blk0, lr=lr),
        grid_spec=pltpu.PrefetchScalarGridSpec(
            num_scalar_prefetch=1,
            grid=(n // tm,),
            in_specs=[
                pl.BlockSpec((tm, D), lambda i, *_: (i, 0)),
                pl.BlockSpec((tm, 2), lambda i, *_: (blk0 + i, 0)),
                pl.BlockSpec((tm, 2), lambda i, *_: (blk0 + i, 0)),
                pl.BlockSpec((1, D), lambda i, *_: (0, 0)),
                pl.BlockSpec(memory_space=pl.ANY),
            ],
            out_specs=pl.BlockSpec((tm, D), lambda i, *_: (i, 0)),
            scratch_shapes=[pltpu.VMEM((2, lr, D), F32), pltpu.SemaphoreType.DMA((2,))],
        ),
        out_shape=jax.ShapeDtypeStruct((n, D), F32),
        compiler_params=pltpu.CompilerParams(dimension_semantics=("arbitrary",), vmem_limit_bytes=V7X_VMEM_LIMIT),
        name=f"moe_combine_{n}",
    )(tab, x, gate, lpos, nf, ys)


def kernel(x_prompt, x_sample, state_hgrn, state_gdn, state_gdn_conv, meta_tokens, norm_mix, norm_ffn, norm_final,
           hg_w_in, hg_lower_bounds, hg_g_norm, hg_w_o, gdn_w_in, gdn_conv_w, gdn_a_log, gdn_dt_bias, gdn_g_norm,
           gdn_w_o, ffn_w1, ffn_w3, ffn_w2, moe_w_router, moe_w1, moe_w3, moe_w2):
    bp, tp, _ = x_prompt.shape
    bs, ts, _ = x_sample.shape
    np_, ns_ = bp * tp, bs * ts
    n_tok = np_ + ns_
    row = lambda v: v.reshape(1, -1).astype(F32)

    hg_win = hg_w_in.astype(BF16)
    hg_wo = hg_w_o.astype(BF16)
    gdn_win = jnp.concatenate(
        [gdn_w_in[:, :QKV + D], gdn_w_in[:, QKV + D:], jnp.zeros((D, LANES - 2 * H), gdn_w_in.dtype)],
        axis=1).astype(BF16)
    gdn_wo = gdn_w_o.astype(BF16)
    par = jnp.zeros((2, LANES), F32)
    par = par.at[0, H:2 * H].set(gdn_dt_bias.astype(F32)).at[1, H:2 * H].set(gdn_a_log.astype(F32))
    w1d, w3d, w2d = ffn_w1.astype(BF16), ffn_w3.astype(BF16), ffn_w2.astype(BF16)
    w1e, w3e, w2e = moe_w1.astype(BF16), moe_w3.astype(BF16), moe_w2.astype(BF16)
    wr = jnp.concatenate([moe_w_router.astype(F32), jnp.zeros((D, LANES - NE), F32)], axis=1)
    nm0, nm1 = row(norm_mix[0]), row(norm_mix[1])
    nf0, nf1 = row(norm_ffn[0]), row(norm_ffn[1])
    nfin = row(norm_final)
    hg_gn, gdn_gn = row(hg_g_norm), row(gdn_g_norm)
    lbnd = hg_lower_bounds.astype(F32)
    cw = gdn_conv_w.astype(F32)

    hg_args = (nm0, hg_win, lbnd, hg_gn, hg_wo)
    gdn_args = (nm1, gdn_win, cw, par, gdn_gn, gdn_wo)
    xm = meta_tokens.astype(F32)
    xp2 = x_prompt.reshape(np_, D)
    xs2 = x_sample.reshape(ns_, D)

    zero_state = jnp.zeros((1, H, HD, HD), F32)
    h1m, hg_m = _hgrn_call(xm, zero_state, *hg_args, nseq=1, seqlen=N_META, bb=1, tt=N_META, c=N_META)
    h1p, hg_p = _hgrn_call(xp2, hg_m, *hg_args, nseq=bp, seqlen=tp, bb=1, tt=256, c=64, nu=4, bcast_state=True)
    h1s, hg_s = _hgrn_call(xs2, state_hgrn.astype(F32), *hg_args, nseq=bs, seqlen=ts, bb=8, tt=ts, c=64, nu=1)
    h2m = _ffn_call([h1m], nf0, w1d, w3d, w2d, tm=N_META)
    h2 = _ffn_call([h1p, h1s], nf0, w1d, w3d, w2d, tm=512)

    zero_conv = jnp.zeros((1, CONV - 1, QKV), F32)
    _, gdn_m, conv_m = _gdn_call(h2m, zero_state, zero_conv, *gdn_args, nseq=1, seqlen=N_META, bb=1, tt=N_META, c=16)
    h3p, gdn_p, conv_p = _gdn_call(h2, gdn_m, conv_m, *gdn_args, nseq=bp, seqlen=tp, bb=1, tt=256, c=128, nu=2,
                                   bcast_state=True)
    h3s, gdn_s, conv_s = _gdn_call(h2, state_gdn.astype(F32), state_gdn_conv.astype(F32), *gdn_args,
                                   nseq=bs, seqlen=ts, bb=8, tt=ts, c=64, nu=1, row0=np_)

    yp, ysm = _moe_final(h3p, h3s, nf1, wr, w1e, w3e, w2e, nfin, tm=256)

    sd, gd, cd = state_hgrn.dtype, state_gdn.dtype, state_gdn_conv.dtype
    return (yp.reshape(bp, tp, D), ysm.reshape(bs, ts, D),
            hg_p.astype(sd), hg_s.astype(sd),
            gdn_p.astype(gd), gdn_s.astype(gd), conv_p.astype(cd), conv_s.astype(cd))


def _moe_final(h3p, h3s, nf1, wr, w1e, w3e, w2e, nfin, *, tm):
    np_ = h3p.shape[0]
    n_tok = np_ + h3s.shape[0]
    nblk = n_tok // tm
    gate, lpos, lpost, tab, tot = _router_call(h3p, h3s, nf1, wr, tm=tm)
    totals = tot[0, :NE].astype(jnp.int32)
    padded = ((totals + TM - 1) // TM) * TM
    ends = jnp.cumsum(padded)
    offs = ends - padded
    tab3 = tab.reshape(nblk, 8, LANES)[:, :3, :NE].astype(jnp.int32)
    seg_tab = jnp.stack([tab3[:, 0] // SEG, tab3[:, 1], tab3[:, 2] + offs[None, :]], axis=-1).reshape(-1)
    ntile = -(-(2 * n_tok + nblk * NE * (SEG - 1) + NE * (TM - 1)) // TM)
    nact = ends[-1] // TM
    tiles = jnp.arange(ntile, dtype=jnp.int32)
    texp_all = jnp.minimum(jnp.sum(tiles[:, None] * TM >= ends[None, :], axis=1), NE - 1).astype(jnp.int32)
    last = jnp.maximum(nact - 1, 0)
    active = tiles < nact
    texp = jnp.where(active, texp_all, texp_all[last]).astype(jnp.int32)
    tsrc = jnp.where(active, tiles, last).astype(jnp.int32)
    tvalid = jnp.where(active, jnp.clip(totals[texp_all] - (tiles * TM - offs[texp_all]), 0, TM), 0).astype(jnp.int32)
    tail = nact + jnp.arange(ntile - 2 * n_tok // TM)
    zst = jnp.concatenate([jnp.where(padded > 0, ends - TM, -1),
                           jnp.where(tail < ntile, tail * TM, -1)]).astype(jnp.int32)
    xs_sorted = _dispatch_call(seg_tab, zst, h3p, h3s, lpost, nf1, tm=tm, rmax=ntile * TM)
    ys = _expert_call(texp, tsrc, tvalid, xs_sorted, w1e, w3e, w2e)
    yp = _combine_call(seg_tab, h3p, gate, lpos, nfin, ys, tm=tm, tok0=0)
    ysm = _combine_call(seg_tab, h3s, gate, lpos, nfin, ys, tm=tm, tok0=np_)
    return yp, ysm
```
